```python
import math
import jax, jax.numpy as jnp
from jax import lax
import numpy as np

D_MODEL = 1024
BATCH = 1
SEQ = 16384
DEPTH = 1
DEC_BATCH = 128
DEC_SEQ = 8
PAST_LEN = 8192
PAGE_SIZE = 128

H_A = 8
DH_A = 64
W_A = H_A * DH_A
MOBA_BLOCK = 256
MOBA_TOPK = 3
Q_CHUNK = 128
H_M = 4
DH_M = 128
W_M = H_M * DH_M
MLSTM_CHUNK = 64
N_EXPERTS = 32
TOP_K = 4
D_FF = D_MODEL
SWIGLU_LIMIT = 7.0
SWIGLU_ALPHA = 1.702
MOE_BLOCK = 128
DN_ALPHA = (2.0 * DEPTH) ** 0.25
DN_BETA = (8.0 * DEPTH) ** -0.25
LN_EPS = 1e-5
NEG_INF = -1e30
_SPLITS = (W_A, W_A, W_A, W_M, W_M, W_M, W_M, H_M, H_M, D_MODEL, D_MODEL)
D_IN = sum(_SPLITS)
_SPLIT_AT = [int(s) for s in np.cumsum(_SPLITS)[:-1]]

kernel_name = 'hybrid_moba_mlstm_moe_step'


def _alibi_slopes():
    return jnp.exp2(-8.0 * (jnp.arange(H_A, dtype=jnp.float32) + 1.0) / H_A)


def _layernorm(x, g, b):
    xf = x.astype(jnp.float32)
    mu = jnp.mean(xf, axis=-1, keepdims=True)
    var = jnp.mean(jnp.square(xf - mu), axis=-1, keepdims=True)
    y = (xf - mu) * lax.rsqrt(var + LN_EPS) * g.astype(jnp.float32) + b.astype(jnp.float32)
    return y.astype(x.dtype)


def _heads(t, n_heads):
    b, s, _ = t.shape
    return t.reshape(b, s, n_heads, -1).transpose(0, 2, 1, 3)


def _unheads(t):
    b, h, s, d = t.shape
    return t.transpose(0, 2, 1, 3).reshape(b, s, h * d)


def _pages_to_rows(p):
    b, r, h, ps, d = p.shape
    return p.transpose(0, 2, 1, 3, 4).reshape(b, h, r * ps, d)


def _rows_to_pages(t):
    b, h, s, d = t.shape
    return t.reshape(b, h, s // PAGE_SIZE, PAGE_SIZE, d).transpose(0, 2, 1, 3, 4)


def _moba_core(q, pos_q, k_own, v_own, pos_own, slopes, sel=None):
    scale = DH_A ** -0.5
    sl = slopes[:, None, None]
    s_own = jnp.einsum('bhqd,bhld->bhql', q, k_own, preferred_element_type=jnp.float32) * scale
    dist_own = (pos_q[:, None] - pos_own[None, :]).astype(jnp.float32)
    s_own = jnp.where(dist_own >= 0, s_own - sl * dist_own, NEG_INF)
    if sel is None:
        p = jax.nn.softmax(s_own, axis=-1).astype(v_own.dtype)
        return jnp.einsum('bhql,bhld->bhqd', p, v_own)
    k_sel, v_sel, pos_sel, valid_sel = sel
    b, h, nq, ns, nl, _ = k_sel.shape
    s_sel = jnp.einsum('bhqd,bhqsld->bhqsl', q, k_sel, preferred_element_type=jnp.float32) * scale
    dist_sel = (pos_q[:, None, None] - pos_sel).astype(jnp.float32)
    s_sel = s_sel - sl[..., None] * dist_sel
    if valid_sel is not None:
        s_sel = jnp.where(valid_sel[..., None], s_sel, NEG_INF)
    logits = jnp.concatenate([s_sel.reshape(b, h, nq, ns * nl), s_own], axis=-1)
    p = jax.nn.softmax(logits, axis=-1).astype(v_own.dtype)
    p_sel = p[..., :ns * nl].reshape(b, h, nq, ns, nl)
    return (jnp.einsum('bhqsl,bhqsld->bhqd', p_sel, v_sel)
            + jnp.einsum('bhql,bhld->bhqd', p[..., ns * nl:], v_own))


def _moba_prompt(q, k, v, slopes):
    b, h, t, dh = q.shape
    nb = -(-t // MOBA_BLOCK)
    padw = ((0, 0), (0, 0), (0, nb * MOBA_BLOCK - t), (0, 0))
    kb = jnp.pad(k, padw).reshape(b, h, nb, MOBA_BLOCK, dh)
    vb = jnp.pad(v, padw).reshape(b, h, nb, MOBA_BLOCK, dh)
    k_mean = jnp.mean(kb.astype(jnp.float32), axis=3)
    n_sel = min(MOBA_TOPK, nb)
    b_ix = jnp.arange(b)[:, None, None, None]
    h_ix = jnp.arange(h)[None, :, None, None]
    offs = jnp.arange(MOBA_BLOCK)
    n_chunks = t // Q_CHUNK
    q_chunks = q.reshape(b, h, n_chunks, Q_CHUNK, dh).transpose(2, 0, 1, 3, 4)

    def one_chunk(args):
        qc, c = args
        pos_q = c * Q_CHUNK + jnp.arange(Q_CHUNK)
        j = (c * Q_CHUNK) // MOBA_BLOCK
        scores = jnp.einsum('bhqd,bhnd->bhqn', qc.astype(jnp.float32), k_mean)
        scores = jnp.where(jnp.arange(nb) < j, scores, NEG_INF)
        _, idx = lax.top_k(scores, n_sel)
        sel = (kb[b_ix, h_ix, idx], vb[b_ix, h_ix, idx],
               idx[..., None] * MOBA_BLOCK + offs, idx < j)
        k_own = lax.dynamic_index_in_dim(kb, j, axis=2, keepdims=False)
        v_own = lax.dynamic_index_in_dim(vb, j, axis=2, keepdims=False)
        pos_own = j * MOBA_BLOCK + offs
        return _moba_core(qc, pos_q, k_own, v_own, pos_own, slopes, sel)

    out = lax.map(one_chunk, (q_chunks, jnp.arange(n_chunks)))
    return out.transpose(1, 2, 0, 3, 4).reshape(b, h, t, dh)


def _moba_sample(q, k_new, v_new, cache_k, cache_v, layer, page_table, slopes):
    db, h, t, dh = q.shape
    n_pages = page_table.shape[1]
    ps = cache_k.shape[3]
    ppb = MOBA_BLOCK // ps
    past = n_pages * ps
    npb = past // MOBA_BLOCK
    rem = n_pages - npb * ppb
    pos_q = past + jnp.arange(t)
    own_pages = page_table[:, npb * ppb:]
    k_own = jnp.concatenate([_pages_to_rows(cache_k[layer, own_pages]), k_new], axis=2)
    v_own = jnp.concatenate([_pages_to_rows(cache_v[layer, own_pages]), v_new], axis=2)
    pos_own = jnp.concatenate([npb * MOBA_BLOCK + jnp.arange(rem * ps), pos_q])
    if npb == 0:
        return _moba_core(q, pos_q, k_own, v_own, pos_own, slopes)
    past_pages = page_table[:, :npb * ppb]
    page_sums = jnp.sum(cache_k[layer, past_pages].astype(jnp.float32), axis=3)
    k_mean = page_sums.reshape(db, npb, ppb, h, dh).sum(axis=2).transpose(0, 2, 1, 3) / MOBA_BLOCK
    scores = jnp.einsum('bhtd,bhnd->bhtn', q.astype(jnp.float32), k_mean)
    _, idx = lax.top_k(scores, min(MOBA_TOPK, npb))
    b_ix = jnp.arange(db)[:, None, None, None, None]
    h_ix = jnp.arange(h)[None, :, None, None, None]
    offs_p = jnp.arange(ppb)
    offs = jnp.arange(MOBA_BLOCK)

    def one_tok(args):
        q_t, idx_t, pos_t = args
        ns = idx_t.shape[-1]
        phys = page_table[b_ix, idx_t[..., None] * ppb + offs_p]
        k_sel = cache_k[layer, phys, h_ix].reshape(db, h, 1, ns, MOBA_BLOCK, dh)
        v_sel = cache_v[layer, phys, h_ix].reshape(db, h, 1, ns, MOBA_BLOCK, dh)
        pos_sel = idx_t[..., None] * MOBA_BLOCK + offs
        return _moba_core(q_t, pos_t, k_own, v_own, pos_own, slopes, (k_sel, v_sel, pos_sel, None))

    xs = (jnp.moveaxis(q, 2, 0)[:, :, :, None], jnp.moveaxis(idx, 2, 0)[:, :, :, None], pos_q[:, None])
    out = lax.map(one_tok, xs)
    return jnp.moveaxis(out[:, :, :, 0], 0, 2)


def _mlstm(q, k, v, i_pre, f_pre, c0, n0, m0):
    b, h, t, dk = q.shape
    dv = v.shape[-1]
    L = math.gcd(t, MLSTM_CHUNK)
    nc = t // L
    f32 = jnp.float32

    def chunks(a):
        a = a.astype(f32)
        return jnp.moveaxis(a.reshape(b, h, nc, L, *a.shape[3:]), 2, 0)

    qs, ks, vs = chunks(q), chunks(k) * dk ** -0.5, chunks(v)
    ig = chunks(i_pre)
    lf = chunks(jax.nn.log_sigmoid(f_pre.astype(f32)))
    causal = jnp.tril(jnp.ones((L, L), dtype=bool))

    def step(carry, xs):
        C, n, m = carry
        qc, kc, vc, ic, fc = xs
        bcum = jnp.cumsum(fc, axis=-1)
        a_inter = bcum + m[..., None]
        dmat = jnp.where(causal, bcum[..., :, None] - bcum[..., None, :] + ic[..., None, :], -jnp.inf)
        m_t = jnp.maximum(a_inter, jnp.max(dmat, axis=-1))
        w_inter = jnp.exp(a_inter - m_t)
        s = jnp.einsum('bhtd,bhsd->bhts', qc, kc) * jnp.exp(dmat - m_t[..., None])
        num = w_inter[..., None] * jnp.einsum('bhtd,bhde->bhte', qc, C) + jnp.einsum('bhts,bhse->bhte', s, vc)
        den = w_inter * jnp.einsum('bhtd,bhd->bht', qc, n) + jnp.sum(s, axis=-1)
        h_out = num / jnp.maximum(jnp.abs(den), jnp.exp(-m_t))[..., None]
        m_new = m_t[..., -1]
        g_c = jnp.exp(bcum[..., -1] + m - m_new)
        g_s = jnp.exp(bcum[..., -1:] - bcum + ic - m_new[..., None])
        C_new = g_c[..., None, None] * C + jnp.einsum('bhs,bhsd,bhse->bhde', g_s, kc, vc)
        n_new = g_c[..., None] * n + jnp.einsum('bhs,bhsd->bhd', g_s, kc)
        return (C_new, n_new, m_new), h_out

    (c_t, n_t, m_t), hs = lax.scan(step, (c0.astype(f32), n0.astype(f32), m0.astype(f32)),
                                   (qs, ks, vs, ig, lf))
    hs = jnp.moveaxis(hs, 0, 2).reshape(b, h, t, dv)
    return hs.astype(q.dtype), c_t, n_t, m_t


def _moe(x, w_router, b_router, w_gate, b_gate, w_up, b_up, w_down, b_down):
    shp = x.shape
    xf = x.reshape(-1, shp[-1])
    n_tok = xf.shape[0]
    logits = (xf @ w_router + b_router).astype(jnp.float32)
    top_val, top_idx = lax.top_k(logits, TOP_K)
    gate_w = jax.nn.softmax(top_val, axis=-1)
    m = n_tok * TOP_K
    flat_e = top_idx.reshape(-1)
    order = jnp.argsort(flat_e)
    se = flat_e[order]
    counts = jnp.bincount(flat_e, length=N_EXPERTS)
    starts = jnp.cumsum(counts) - counts
    pcounts = (counts + MOE_BLOCK - 1) // MOE_BLOCK * MOE_BLOCK
    pends = jnp.cumsum(pcounts)
    pstarts = pends - pcounts
    dest = pstarts[se] + jnp.arange(m) - starts[se]
    n_blk = -(-m // MOE_BLOCK) + N_EXPERTS
    slot_tok = jnp.full((n_blk * MOE_BLOCK,), n_tok, jnp.int32).at[dest].set((order // TOP_K).astype(jnp.int32))
    slot_w = jnp.zeros((n_blk * MOE_BLOCK,), jnp.float32).at[dest].set(gate_w.reshape(-1)[order])
    blk_e = jnp.minimum(jnp.searchsorted(pends, jnp.arange(n_blk) * MOE_BLOCK, side='right'), N_EXPERTS - 1)
    x_pad = jnp.concatenate([xf, jnp.zeros((1, xf.shape[1]), xf.dtype)], axis=0)
    xb = x_pad[slot_tok].reshape(n_blk, MOE_BLOCK, xf.shape[1])

    def expert_block(args):
        xe, e = args
        g = jnp.minimum(xe @ w_gate[e] + b_gate[e], SWIGLU_LIMIT)
        u = jnp.clip(xe @ w_up[e] + b_up[e], -SWIGLU_LIMIT, SWIGLU_LIMIT)
        hdn = (u + 1.0) * (g * jax.nn.sigmoid(SWIGLU_ALPHA * g))
        return hdn @ w_down[e] + b_down[e]

    yb = lax.map(expert_block, (xb, blk_e)).reshape(n_blk * MOE_BLOCK, -1)
    yb = yb * slot_w[:, None].astype(yb.dtype)
    y = jnp.zeros((n_tok + 1, yb.shape[1]), yb.dtype).at[slot_tok].add(yb)[:n_tok]
    return y.reshape(shp)


def _project(x, w_in, b_if):
    z = jnp.einsum('btd,de->bte', x, w_in)
    qa, ka, va, qm, km, vm, om, ii, ff, ga, gm = jnp.split(z, _SPLIT_AT, axis=-1)
    attn_qkv = (_heads(qa, H_A), _heads(ka, H_A), _heads(va, H_A))
    mlstm_in = (_heads(qm, H_M), _heads(km, H_M), _heads(vm, H_M),
                (ii + b_if[:H_M]).transpose(0, 2, 1), (ff + b_if[H_M:]).transpose(0, 2, 1))
    return attn_qkv, mlstm_in, om, ga, gm


def _finish(x, a_heads, m_heads, om, ga, gm, w_branch_a, w_branch_m, w_out, ln1_g, ln1_b, moe_p, ln2_g, ln2_b):
    a = _unheads(a_heads)
    mo = _unheads(m_heads) * jax.nn.sigmoid(om)
    merged = jax.nn.sigmoid(ga) * (a @ w_branch_a) + jax.nn.sigmoid(gm) * (mo @ w_branch_m)
    h = _layernorm(DN_ALPHA * x + merged @ w_out, ln1_g, ln1_b)
    return _layernorm(DN_ALPHA * h + _moe(h, *moe_p), ln2_g, ln2_b)


def setup_inputs(seed: int = 0) -> dict:
    key = jax.random.key(seed)
    ks = jax.random.split(key, 26)
    f32 = jnp.float32
    n_pages = PAST_LEN // PAGE_SIZE
    n_used = DEC_BATCH * n_pages
    n_phys = n_used + max(1, n_used // 4)

    def nrm(k, shape, scale=1.0):
        return jax.random.normal(k, shape, f32) * scale

    col_scale = jnp.asarray(np.concatenate(
        [np.full(s, DN_BETA if i in (2, 5) else 1.0) for i, s in enumerate(_SPLITS)]), f32)
    page_table = jax.random.permutation(ks[4], n_phys)[:n_used].reshape(DEC_BATCH, n_pages).astype(jnp.int32)
    b_if = jnp.concatenate([nrm(ks[9], (DEPTH, H_M), 0.1),
                            jnp.linspace(3.0, 6.0, H_M, dtype=f32)[None] + nrm(ks[10], (DEPTH, H_M), 0.1)], axis=-1)
    return {
        'x_prompt': nrm(ks[0], (BATCH, SEQ, D_MODEL)),
        'x_sample': nrm(ks[1], (DEC_BATCH, DEC_SEQ, D_MODEL)),
        'cache_k': nrm(ks[2], (DEPTH, n_phys, H_A, PAGE_SIZE, DH_A)),
        'cache_v': nrm(ks[3], (DEPTH, n_phys, H_A, PAGE_SIZE, DH_A), DN_BETA),
        'page_table': page_table,
        'state_C': nrm(ks[5], (DEPTH, DEC_BATCH, H_M, DH_M, DH_M), 0.3),
        'state_n': nrm(ks[6], (DEPTH, DEC_BATCH, H_M, DH_M), 0.5),
        'state_m': jax.random.uniform(ks[7], (DEPTH, DEC_BATCH, H_M), f32, 0.0, 2.0),
        'w_in': nrm(ks[8], (DEPTH, D_MODEL, D_IN), D_MODEL ** -0.5) * col_scale,
        'b_if': b_if,
        'w_branch_a': nrm(ks[11], (DEPTH, W_A, D_MODEL), W_A ** -0.5),
        'w_branch_m': nrm(ks[12], (DEPTH, W_M, D_MODEL), W_M ** -0.5),
        'w_out': nrm(ks[13], (DEPTH, D_MODEL, D_MODEL), DN_BETA * D_MODEL ** -0.5),
        'ln1_g': 1.0 + nrm(ks[14], (DEPTH, D_MODEL), 0.02),
        'ln1_b': nrm(ks[15], (DEPTH, D_MODEL), 0.02),
        'w_router': nrm(ks[16], (DEPTH, D_MODEL, N_EXPERTS), D_MODEL ** -0.5),
        'b_router': nrm(ks[17], (DEPTH, N_EXPERTS), 0.01),
        'w_gate': nrm(ks[18], (DEPTH, N_EXPERTS, D_MODEL, D_FF), D_MODEL ** -0.5),
        'b_gate': nrm(ks[19], (DEPTH, N_EXPERTS, D_FF), 0.02),
        'w_up': nrm(ks[20], (DEPTH, N_EXPERTS, D_MODEL, D_FF), D_MODEL ** -0.5),
        'b_up': nrm(ks[21], (DEPTH, N_EXPERTS, D_FF), 0.02),
        'w_down': nrm(ks[22], (DEPTH, N_EXPERTS, D_FF, D_MODEL), DN_BETA * D_FF ** -0.5),
        'b_down': nrm(ks[23], (DEPTH, N_EXPERTS, D_MODEL), 0.02 * DN_BETA),
        'ln2_g': 1.0 + nrm(ks[24], (DEPTH, D_MODEL), 0.02),
        'ln2_b': nrm(ks[25], (DEPTH, D_MODEL), 0.02),
    }


def reference(x_prompt, x_sample, cache_k, cache_v, page_table, state_C, state_n, state_m,
              w_in, b_if, w_branch_a, w_branch_m, w_out, ln1_g, ln1_b,
              w_router, b_router, w_gate, b_gate, w_up, b_up, w_down, b_down, ln2_g, ln2_b):
    slopes = _alibi_slopes()
    yp, ys = x_prompt, x_sample
    kp_l, vp_l, ks_l, vs_l = [], [], [], []
    cp_l, np_l, mp_l, cs_l, ns_l, ms_l = [], [], [], [], [], []
    for l in range(DEPTH):
        moe_p = (w_router[l], b_router[l], w_gate[l], b_gate[l], w_up[l], b_up[l], w_down[l], b_down[l])
        post = (w_branch_a[l], w_branch_m[l], w_out[l], ln1_g[l], ln1_b[l], moe_p, ln2_g[l], ln2_b[l])
        (qa, ka, va), m_in, om, ga, gm = _project(yp, w_in[l], b_if[l])
        a = _moba_prompt(qa, ka, va, slopes)
        bp = yp.shape[0]
        mh, c_t, n_t, m_t = _mlstm(*m_in, jnp.zeros((bp, H_M, DH_M, DH_M), jnp.float32),
                                   jnp.zeros((bp, H_M, DH_M), jnp.float32), jnp.zeros((bp, H_M), jnp.float32))
        kp_l.append(_rows_to_pages(ka))
        vp_l.append(_rows_to_pages(va))
        cp_l.append(c_t)
        np_l.append(n_t)
        mp_l.append(m_t)
        yp = _finish(yp, a, mh, om, ga, gm, *post)
        (qa, ka, va), m_in, om, ga, gm = _project(ys, w_in[l], b_if[l])
        a = _moba_sample(qa, ka, va, cache_k, cache_v, l, page_table, slopes)
        mh, c_t, n_t, m_t = _mlstm(*m_in, state_C[l], state_n[l], state_m[l])
        ks_l.append(ka)
        vs_l.append(va)
        cs_l.append(c_t)
        ns_l.append(n_t)
        ms_l.append(m_t)
        ys = _finish(ys, a, mh, om, ga, gm, *post)
    y_prompt, y_sample = yp, ys
    k_prompt, v_prompt = jnp.stack(kp_l), jnp.stack(vp_l)
    k_sample, v_sample = jnp.stack(ks_l), jnp.stack(vs_l)
    C_prompt, n_prompt, m_prompt = jnp.stack(cp_l), jnp.stack(np_l), jnp.stack(mp_l)
    C_sample, n_sample, m_sample = jnp.stack(cs_l), jnp.stack(ns_l), jnp.stack(ms_l)
    return (y_prompt, y_sample, k_prompt, v_prompt, k_sample, v_sample,
            C_prompt, n_prompt, m_prompt, C_sample, n_sample, m_sample)
```

```python
import functools
import math

import jax
import jax.numpy as jnp
from jax import lax
from jax.experimental import pallas as pl
from jax.experimental.pallas import tpu as pltpu

F32 = jnp.float32
BF16 = jnp.bfloat16
HIGHEST = lax.Precision.HIGHEST

D_MODEL = 1024
PAGE_SIZE = 128
H_A = 8
DH_A = 64
W_A = H_A * DH_A
MOBA_BLOCK = 256
MOBA_TOPK = 3
H_M = 4
DH_M = 128
W_M = H_M * DH_M
MLSTM_CHUNK = 64
N_EXPERTS = 32
TOP_K = 4
SWIGLU_LIMIT = 7.0
SWIGLU_ALPHA = 1.702
DN_ALPHA = 2.0 ** 0.25
LN_EPS = 1e-5
NEG_INF = -1e30

LANES = 128
Z_GA, Z_GM, Z_QA, Z_QM, Z_KM, Z_VM, Z_OM = 0, 1024, 2048, 2560, 3072, 3584, 4096
Z_WIDTH = 4608
VMEM_LIMIT = 56 * 1024 * 1024


def _nt_dot(a, b, precision=None):
    return lax.dot_general(a, b, (((1,), (1,)), ((), ())), precision=precision,
                           preferred_element_type=F32)


def _dot(a, b, precision=None):
    return jnp.dot(a, b, precision=precision, preferred_element_type=F32)


def _top_mask(scores, idx_f, k):
    n = scores.shape[-1]
    sel = jnp.zeros(scores.shape, jnp.bool_)
    work = scores
    first_max = None
    for _ in range(k):
        mx = jnp.max(work, axis=-1, keepdims=True)
        if first_max is None:
            first_max = mx
        cand = work == mx
        pick_idx = jnp.min(jnp.where(cand, idx_f, float(n)), axis=-1, keepdims=True)
        pick = idx_f == pick_idx
        sel = jnp.logical_or(sel, pick)
        work = jnp.where(pick, -jnp.inf, work)
    return sel, first_max


def _proj_kernel(x_ref, wz_ref, wkv_ref, wif_ref, bif_ref,
                 z_ref, kvb_ref, kp_ref, vp_ref, gif_ref, kmean_ref, *, ps):
    x = x_ref[...]
    tm = x.shape[0]
    xb = x.astype(BF16)
    gif_ref[...] = _dot(x, wif_ref[...], precision=HIGHEST) + bif_ref[...]
    for c in range(Z_WIDTH // 512):
        z_ref[:, c * 512:(c + 1) * 512] = _dot(xb, wz_ref[:, c * 512:(c + 1) * 512])
    for c, pref in ((0, kp_ref), (1, vp_ref)):
        r = _dot(xb, wkv_ref[:, c * W_A:(c + 1) * W_A])
        kvb_ref[:, c * W_A:(c + 1) * W_A] = r.astype(BF16)
        for h in range(H_A):
            pref[:, h, :, :] = r[:, h * DH_A:(h + 1) * DH_A].reshape(tm // ps, ps, DH_A)
        if c == 0:
            kmean_ref[...] = jnp.mean(r.reshape(tm // MOBA_BLOCK, MOBA_BLOCK, W_A), axis=1, keepdims=True)


def _project(x2d, wz, wkv, wif, bif, *, ps, tm):
    t = x2d.shape[0]
    grid = (t // tm,)
    const = lambda i: (0, 0)
    return pl.pallas_call(
        functools.partial(_proj_kernel, ps=ps),
        grid=grid,
        in_specs=[
            pl.BlockSpec((tm, D_MODEL), lambda i: (i, 0)),
            pl.BlockSpec(wz.shape, const),
            pl.BlockSpec(wkv.shape, const),
            pl.BlockSpec(wif.shape, const),
            pl.BlockSpec(bif.shape, const),
        ],
        out_specs=[
            pl.BlockSpec((tm, Z_WIDTH), lambda i: (i, 0)),
            pl.BlockSpec((tm, 2 * W_A), lambda i: (i, 0)),
            pl.BlockSpec((tm // ps, H_A, ps, DH_A), lambda i: (i, 0, 0, 0)),
            pl.BlockSpec((tm // ps, H_A, ps, DH_A), lambda i: (i, 0, 0, 0)),
            pl.BlockSpec((tm, LANES), lambda i: (i, 0)),
            pl.BlockSpec((tm // MOBA_BLOCK, 1, W_A), lambda i: (i, 0, 0)),
        ],
        out_shape=[
            jax.ShapeDtypeStruct((t, Z_WIDTH), F32),
            jax.ShapeDtypeStruct((t, 2 * W_A), BF16),
            jax.ShapeDtypeStruct((t // ps, H_A, ps, DH_A), F32),
            jax.ShapeDtypeStruct((t // ps, H_A, ps, DH_A), F32),
            jax.ShapeDtypeStruct((t, LANES), F32),
            jax.ShapeDtypeStruct((t // MOBA_BLOCK, 1, W_A), F32),
        ],
        compiler_params=pltpu.CompilerParams(
            dimension_semantics=("arbitrary",), vmem_limit_bytes=VMEM_LIMIT),
        name="proj",
    )(x2d, wz, wkv, wif, bif)


def _moba_prompt_kernel(q_ref, k_ref, v_ref, km_ref, o_ref, m_sc, l_sc, acc_sc, sel_sc, sd_sc):
    hp = pl.program_id(0)
    j = pl.program_id(1)
    bq = MOBA_BLOCK
    nb = km_ref.shape[0]
    q = q_ref[...]
    lane = lax.broadcasted_iota(jnp.int32, (1, LANES), 1)
    blk_f = lax.broadcasted_iota(jnp.int32, (1, nb), 1).astype(F32)
    jf = j.astype(F32)
    rq = lax.broadcasted_iota(jnp.int32, (bq, bq), 0)
    rk = lax.broadcasted_iota(jnp.int32, (bq, bq), 1)
    dist0 = (rq - rk).astype(F32)
    kb = k_ref[pl.ds(pl.multiple_of(j * bq, bq), bq), :]
    vb = v_ref[pl.ds(pl.multiple_of(j * bq, bq), bq), :]
    qbs = []
    slopes = []
    for hh in range(2):
        hmask = jnp.logical_and(lane >= hh * DH_A, lane < (hh + 1) * DH_A)
        qh = jnp.where(hmask, q, 0.0)
        head_f = (2 * hp + hh + 1).astype(F32)
        slope = jnp.exp2(jnp.zeros((1, 1), F32) - head_f)
        slopes.append(slope)
        scores = _nt_dot(qh, km_ref[...], precision=HIGHEST)
        scores = jnp.where(blk_f < jf, scores, NEG_INF)
        sel, _ = _top_mask(scores, blk_f, min(MOBA_TOPK, nb))
        sel = jnp.logical_and(sel, blk_f < jf)
        sel_sc[hh] = jnp.where(sel, 1.0, 0.0).astype(BF16)
        qb = (qh * (DH_A ** -0.5)).astype(BF16)
        qbs.append(qb)
        sd = slope * dist0
        sd_sc[hh] = sd
        s = _nt_dot(qb, kb)
        logits = jnp.where(dist0 >= 0, s - sd, NEG_INF)
        m = jnp.max(logits, axis=-1, keepdims=True)
        p = jnp.exp(logits - m)
        m_sc[hh] = m
        l_sc[hh] = jnp.sum(p, axis=-1, keepdims=True)
        acc_sc[hh] = _dot(p.astype(BF16), vb)

    def body(n, carry):
        kn = k_ref[pl.ds(pl.multiple_of(n * bq, bq), bq), :]
        vn = v_ref[pl.ds(pl.multiple_of(n * bq, bq), bq), :]
        onehot = (lax.broadcasted_iota(jnp.int32, (nb, LANES), 0) == n).astype(BF16)
        off = (j - n).astype(F32) * float(bq)
        for hh in range(2):
            selb = _dot(sel_sc[hh], onehot)
            rb = jnp.where(selb > 0.5, -(slopes[hh] * off), NEG_INF)
            s = _nt_dot(qbs[hh], kn)
            logits = (s - sd_sc[hh]) + jnp.concatenate([rb, rb], axis=1)
            m_old = m_sc[hh]
            m_new = jnp.maximum(m_old, jnp.max(logits, axis=-1, keepdims=True))
            alpha = jnp.exp(m_old - m_new)
            p = jnp.exp(logits - m_new)
            l_sc[hh] = alpha * l_sc[hh] + jnp.sum(p, axis=-1, keepdims=True)
            acc_sc[hh] = alpha * acc_sc[hh] + _dot(p.astype(BF16), vn)
            m_sc[hh] = m_new
        return carry

    lax.fori_loop(0, j, body, 0)
    o0 = acc_sc[0] / l_sc[0]
    o1 = acc_sc[1] / l_sc[1]
    o_ref[...] = jnp.where(lane < DH_A, o0, o1)


def _moba_prompt(z, kvb, kmean):
    t = z.shape[0]
    nb = t // MOBA_BLOCK
    bq = MOBA_BLOCK
    qcol = Z_QA // LANES
    return pl.pallas_call(
        _moba_prompt_kernel,
        grid=(H_A // 2, nb),
        in_specs=[
            pl.BlockSpec((bq, LANES), lambda hp, j: (j, qcol + hp)),
            pl.BlockSpec((t, LANES), lambda hp, j: (0, hp)),
            pl.BlockSpec((t, LANES), lambda hp, j: (0, W_A // LANES + hp)),
            pl.BlockSpec((nb, LANES), lambda hp, j: (0, hp)),
        ],
        out_specs=pl.BlockSpec((bq, LANES), lambda hp, j: (j, hp)),
        out_shape=jax.ShapeDtypeStruct((t, W_A), F32),
        scratch_shapes=[
            pltpu.VMEM((2, bq, 1), F32),
            pltpu.VMEM((2, bq, 1), F32),
            pltpu.VMEM((2, bq, LANES), F32),
            pltpu.VMEM((2, bq, nb), BF16),
            pltpu.VMEM((2, bq, bq), F32),
        ],
        compiler_params=pltpu.CompilerParams(
            dimension_semantics=("arbitrary", "arbitrary"), vmem_limit_bytes=VMEM_LIMIT),
        name="moba_prompt",
    )(z, kvb, kvb, kmean)


def _moba_sample_kernel(pt_ref, q_ref, kn_ref, vn_ref, *refs, ppg, past):
    k_refs = refs[:ppg]
    v_refs = refs[ppg:2 * ppg]
    o_ref = refs[2 * ppg]
    qs_sc, qf_sc, ksum_sc, m_sc, l_sc, o_sc = refs[2 * ppg + 1:]
    g = pl.program_id(1)
    ng = pl.num_programs(1)
    t = q_ref.shape[0]
    npb = ksum_sc.shape[1]
    ppb = MOBA_BLOCK // PAGE_SIZE
    head_f = lax.broadcasted_iota(jnp.int32, (H_A, 1, 1), 0).astype(F32)
    slope = jnp.exp2(-(head_f + 1.0))
    blk_lane = lax.broadcasted_iota(jnp.int32, (1, 1, npb), 2)

    @pl.when(g == 0)
    def _():
        q = q_ref[...]
        for h in range(H_A):
            qh = q[:, h * DH_A:(h + 1) * DH_A]
            qf_sc[h] = qh
            qs_sc[h] = qh * (DH_A ** -0.5)

    qs = qs_sc[...]
    tq = lax.broadcasted_iota(jnp.int32, (1, t, MOBA_BLOCK), 1)
    rk = lax.broadcasted_iota(jnp.int32, (1, t, MOBA_BLOCK), 2)
    for b in range(ppg // ppb):
        n = g * (ppg // ppb) + b
        ks = [k_refs[b * ppb + p][...] for p in range(ppb)]
        vs = [v_refs[b * ppb + p][...] for p in range(ppb)]
        ksum = ks[0].sum(axis=1)
        for p in range(1, ppb):
            ksum = ksum + ks[p].sum(axis=1)
        for h in range(H_A):
            ksum_sc[h, pl.ds(n, 1), :] = ksum[h:h + 1, :]
        s = jnp.concatenate(
            [jnp.einsum('htd,hkd->htk', qs, kp, preferred_element_type=F32) for kp in ks],
            axis=-1)
        dist = (past - n * MOBA_BLOCK + tq - rk).astype(F32)
        logits = s - slope * dist
        m_b = jnp.max(logits, axis=-1, keepdims=True)
        p_ = jnp.exp(logits - m_b)
        l_b = jnp.sum(p_, axis=-1, keepdims=True)
        o_b = jnp.einsum('htk,hkd->htd', p_[:, :, :PAGE_SIZE], vs[0], preferred_element_type=F32)
        for p in range(1, ppb):
            o_b = o_b + jnp.einsum('htk,hkd->htd', p_[:, :, p * PAGE_SIZE:(p + 1) * PAGE_SIZE],
                                   vs[p], preferred_element_type=F32)
        o_sc[n] = o_b
        m_sc[...] = jnp.where(blk_lane == n, m_b, m_sc[...])
        l_sc[...] = jnp.where(blk_lane == n, l_b, l_sc[...])

    @pl.when(g == ng - 1)
    def _():
        kmean = ksum_sc[...] * (1.0 / MOBA_BLOCK)
        scores = jnp.einsum('htd,hnd->htn', qf_sc[...], kmean, precision=HIGHEST,
                            preferred_element_type=F32)
        sel, _ = _top_mask(scores, blk_lane.astype(F32), min(MOBA_TOPK, npb))
        kn = kn_ref[...]
        vn = vn_ref[...]
        s_own = jnp.einsum('htd,hsd->hts', qs, kn, preferred_element_type=F32)
        d_own = (lax.broadcasted_iota(jnp.int32, (1, t, t), 1)
                 - lax.broadcasted_iota(jnp.int32, (1, t, t), 2)).astype(F32)
        lg_own = jnp.where(d_own >= 0, s_own - slope * d_own, NEG_INF)
        m_all = m_sc[...]
        m_sel = jnp.max(jnp.where(sel, m_all, -jnp.inf), axis=-1, keepdims=True)
        m_fin = jnp.maximum(m_sel, jnp.max(lg_own, axis=-1, keepdims=True))
        w = jnp.where(sel, jnp.exp(m_all - m_fin), 0.0)
        p_own = jnp.exp(lg_own - m_fin)
        l_fin = jnp.sum(w * l_sc[...], axis=-1, keepdims=True) + jnp.sum(p_own, axis=-1, keepdims=True)
        o = jnp.einsum('hts,hsd->htd', p_own, vn, preferred_element_type=F32)
        for n in range(npb):
            o = o + w[:, :, n:n + 1] * o_sc[n]
        o = o / l_fin
        for h in range(H_A):
            o_ref[:, h * DH_A:(h + 1) * DH_A] = o[h]


def _moba_sample(z, k_new, v_new, cache_k, cache_v, page_table):
    db, n_pages = page_table.shape
    t = z.shape[0] // db
    ppb = MOBA_BLOCK // PAGE_SIZE
    npb = n_pages // ppb
    assert n_pages == npb * ppb, "cached pages must fill whole MoBA blocks"
    ppg = 8 if n_pages % 8 == 0 else ppb
    ng = n_pages // ppg

    def page_map(b, g, pt, *, i):
        return (pt[b * n_pages + g * ppg + i], 0, 0, 0)

    page_specs = [pl.BlockSpec((None, H_A, PAGE_SIZE, DH_A), functools.partial(page_map, i=i))
                  for i in range(ppg)]
    grid_spec = pltpu.PrefetchScalarGridSpec(
        num_scalar_prefetch=1,
        grid=(db, ng),
        in_specs=[
            pl.BlockSpec((t, W_A), lambda b, g, pt: (b, Z_QA // W_A)),
            pl.BlockSpec((None, H_A, t, DH_A), lambda b, g, pt: (b, 0, 0, 0)),
            pl.BlockSpec((None, H_A, t, DH_A), lambda b, g, pt: (b, 0, 0, 0)),
        ] + page_specs + page_specs,
        out_specs=pl.BlockSpec((t, W_A), lambda b, g, pt: (b, 0)),
        scratch_shapes=[
            pltpu.VMEM((H_A, t, DH_A), F32),
            pltpu.VMEM((H_A, t, DH_A), F32),
            pltpu.VMEM((H_A, npb, DH_A), F32),
            pltpu.VMEM((H_A, t, npb), F32),
            pltpu.VMEM((H_A, t, npb), F32),
            pltpu.VMEM((npb, H_A, t, DH_A), F32),
        ],
    )
    return pl.pallas_call(
        functools.partial(_moba_sample_kernel, ppg=ppg, past=n_pages * PAGE_SIZE),
        grid_spec=grid_spec,
        out_shape=jax.ShapeDtypeStruct((db * t, W_A), F32),
        compiler_params=pltpu.CompilerParams(
            dimension_semantics=("arbitrary", "arbitrary"), vmem_limit_bytes=VMEM_LIMIT),
        name="moba_sample",
    )(page_table.reshape(-1), z, k_new, v_new, *([cache_k] * ppg), *([cache_v] * ppg))


def _mlstm_kernel(q_ref, k_ref, v_ref, g_ref, c0_ref, n0_ref, m0_ref,
                  h_ref, c_ref, n_ref, m_ref, c_sc, n_sc, m_sc, *, lc):
    c = pl.program_id(1)
    nc = pl.num_programs(1)
    lp = max(lc, MLSTM_CHUNK)

    @pl.when(c == 0)
    def _():
        c_sc[...] = c0_ref[0]
        n_sc[...] = n0_ref[0]
        m_sc[...] = m0_ref[0]

    def padded(a):
        if a.shape[0] == lp:
            return a
        return jnp.concatenate([a, jnp.zeros((lp - a.shape[0],) + a.shape[1:], a.dtype)], axis=0)

    q = padded(q_ref[...])
    k = padded(k_ref[...]) * (DH_M ** -0.5)
    v = padded(v_ref[...])
    gts = padded(g_ref[...])
    lf = jax.nn.log_sigmoid(gts)
    row = lax.broadcasted_iota(jnp.int32, (lp, lp), 0)
    col = lax.broadcasted_iota(jnp.int32, (lp, lp), 1)
    causal = row >= col
    bcum = _dot(jnp.where(causal, 1.0, 0.0), lf, precision=HIGHEST)
    pick_r = lax.broadcasted_iota(jnp.int32, (8, LANES), 0)
    pick_l = lax.broadcasted_iota(jnp.int32, (8, LANES), 1)
    bcum_t = _nt_dot(jnp.where(pick_l == pick_r + H_M, 1.0, 0.0), bcum, precision=HIGHEST)
    gts_t = _nt_dot(jnp.where(pick_l == pick_r, 1.0, 0.0), gts, precision=HIGHEST)
    eye = jnp.where(lax.broadcasted_iota(jnp.int32, (DH_M, DH_M), 0)
                    == lax.broadcasted_iota(jnp.int32, (DH_M, DH_M), 1), 1.0, 0.0).astype(BF16)
    rvalid = lax.broadcasted_iota(jnp.int32, (lp, 1), 0) < lc
    for h in range(H_M):
        qh = q[:, h * DH_M:(h + 1) * DH_M]
        kh = k[:, h * DH_M:(h + 1) * DH_M]
        vh = v[:, h * DH_M:(h + 1) * DH_M]
        bc = bcum[:, H_M + h:H_M + h + 1]
        ic = gts[:, h:h + 1]
        br = bcum_t[h:h + 1, :]
        ir = gts_t[h:h + 1, :]
        m_prev = m_sc[h:h + 1, :]
        c_prev = c_sc[h]
        n_prev = n_sc[h:h + 1, :]
        dmat = jnp.where(causal, bc - br + ir, -jnp.inf)
        a_inter = bc + m_prev
        m_t = jnp.maximum(a_inter, jnp.max(dmat, axis=-1, keepdims=True))
        w_inter = jnp.exp(a_inter - m_t)
        qhb = qh.astype(BF16)
        s = _nt_dot(qhb, kh.astype(BF16)) * jnp.exp(dmat - m_t)
        num = w_inter * _dot(qhb, c_prev.astype(BF16)) + _dot(s.astype(BF16), vh.astype(BF16))
        den = (w_inter * jnp.sum(qh * n_prev, axis=-1, keepdims=True)
               + jnp.sum(s, axis=-1, keepdims=True))
        h_out = num / jnp.maximum(jnp.abs(den), jnp.exp(-m_t))
        h_ref[:, h * DH_M:(h + 1) * DH_M] = h_out[:lc]
        m_new = m_t[lc - 1:lc, :]
        b_last = bc[lc - 1:lc, :]
        g_c = jnp.exp(b_last + m_prev - m_new)
        g_s = jnp.where(rvalid, jnp.exp(b_last - bc + ic - m_new), 0.0)
        kg = g_s * kh
        kg_t = _nt_dot(eye, kg.astype(BF16)).astype(BF16)
        c_sc[h] = g_c * c_prev + _dot(kg_t, vh.astype(BF16))
        n_sc[h:h + 1, :] = g_c * n_prev + jnp.sum(kg, axis=0, keepdims=True)
        m_sc[h:h + 1, :] = m_new

    @pl.when(c == nc - 1)
    def _():
        c_ref[0] = c_sc[...]
        n_ref[0] = n_sc[...]
        m_ref[0] = m_sc[...]


def _mlstm(z, gif, c0, n0, m0, *, lc):
    b = c0.shape[0]
    t = z.shape[0] // b
    nc = t // lc
    row = lambda bi, ci: bi * nc + ci
    wcol = lambda off: off // W_M
    return pl.pallas_call(
        functools.partial(_mlstm_kernel, lc=lc),
        grid=(b, nc),
        in_specs=[
            pl.BlockSpec((lc, W_M), lambda bi, ci: (row(bi, ci), wcol(Z_QM))),
            pl.BlockSpec((lc, W_M), lambda bi, ci: (row(bi, ci), wcol(Z_KM))),
            pl.BlockSpec((lc, W_M), lambda bi, ci: (row(bi, ci), wcol(Z_VM))),
            pl.BlockSpec((lc, LANES), lambda bi, ci: (row(bi, ci), 0)),
            pl.BlockSpec((1, H_M, DH_M, DH_M), lambda bi, ci: (bi, 0, 0, 0)),
            pl.BlockSpec((1, H_M, DH_M), lambda bi, ci: (bi, 0, 0)),
            pl.BlockSpec((1, H_M, 1), lambda bi, ci: (bi, 0, 0)),
        ],
        out_specs=[
            pl.BlockSpec((lc, W_M), lambda bi, ci: (row(bi, ci), 0)),
            pl.BlockSpec((1, H_M, DH_M, DH_M), lambda bi, ci: (bi, 0, 0, 0)),
            pl.BlockSpec((1, H_M, DH_M), lambda bi, ci: (bi, 0, 0)),
            pl.BlockSpec((1, H_M, 1), lambda bi, ci: (bi, 0, 0)),
        ],
        out_shape=[
            jax.ShapeDtypeStruct((b * t, W_M), F32),
            jax.ShapeDtypeStruct((b, H_M, DH_M, DH_M), F32),
            jax.ShapeDtypeStruct((b, H_M, DH_M), F32),
            jax.ShapeDtypeStruct((b, H_M, 1), F32),
        ],
        scratch_shapes=[
            pltpu.VMEM((H_M, DH_M, DH_M), F32),
            pltpu.VMEM((H_M, DH_M), F32),
            pltpu.VMEM((H_M, 1), F32),
        ],
        compiler_params=pltpu.CompilerParams(
            dimension_semantics=("arbitrary", "arbitrary"), vmem_limit_bytes=VMEM_LIMIT),
        name="mlstm",
    )(z, z, z, gif, c0, n0, m0[..., None])


def _layernorm(y, g, b):
    mu = jnp.mean(y, axis=-1, keepdims=True)
    var = jnp.mean(jnp.square(y - mu), axis=-1, keepdims=True)
    return (y - mu) * lax.rsqrt(var + LN_EPS) * g + b


def _finish_kernel(x_ref, a_ref, mh_ref, om_ref, ga_ref, gm_ref, wa_ref, wm_ref, wo_ref,
                   g1_ref, b1_ref, wr_ref, br_ref,
                   h_ref, hb_ref, gates_ref, gates_t_ref, cnt_ref):
    a = _dot(a_ref[...].astype(BF16), wa_ref[...])
    mo = mh_ref[...] * jax.nn.sigmoid(om_ref[...])
    m = _dot(mo.astype(BF16), wm_ref[...])
    merged = jax.nn.sigmoid(ga_ref[...]) * a + jax.nn.sigmoid(gm_ref[...]) * m
    y = DN_ALPHA * x_ref[...] + _dot(merged.astype(BF16), wo_ref[...])
    h = _layernorm(y, g1_ref[...], b1_ref[...])
    h_ref[...] = h
    hb_ref[...] = h.astype(BF16)
    logits = _dot(h, wr_ref[...], precision=HIGHEST) + br_ref[...]
    lane_f = lax.broadcasted_iota(jnp.int32, (1, LANES), 1).astype(F32)
    sel, top = _top_mask(logits, lane_f, TOP_K)
    e = jnp.where(sel, jnp.exp(logits - top), 0.0)
    gates = e / jnp.sum(e, axis=-1, keepdims=True)
    gates_ref[...] = gates
    gates_t_ref[...] = gates.T
    cnt_ref[0] = jnp.sum(jnp.where(gates > 0.0, 1.0, 0.0), axis=0, keepdims=True).astype(jnp.int32)


def _finish(x2d, a, mh, z, wa, wm, wo, g1, b1, wr, br, *, tm):
    t = x2d.shape[0]
    nt = t // tm
    const = lambda i: (0, 0)
    return pl.pallas_call(
        _finish_kernel,
        grid=(nt,),
        in_specs=[
            pl.BlockSpec((tm, D_MODEL), lambda i: (i, 0)),
            pl.BlockSpec((tm, W_A), lambda i: (i, 0)),
            pl.BlockSpec((tm, W_M), lambda i: (i, 0)),
            pl.BlockSpec((tm, W_M), lambda i: (i, Z_OM // W_M)),
            pl.BlockSpec((tm, D_MODEL), lambda i: (i, Z_GA // D_MODEL)),
            pl.BlockSpec((tm, D_MODEL), lambda i: (i, Z_GM // D_MODEL)),
            pl.BlockSpec(wa.shape, const),
            pl.BlockSpec(wm.shape, const),
            pl.BlockSpec(wo.shape, const),
            pl.BlockSpec(g1.shape, const),
            pl.BlockSpec(b1.shape, const),
            pl.BlockSpec(wr.shape, const),
            pl.BlockSpec(br.shape, const),
        ],
        out_specs=[
            pl.BlockSpec((tm, D_MODEL), lambda i: (i, 0)),
            pl.BlockSpec((tm, D_MODEL), lambda i: (i, 0)),
            pl.BlockSpec((tm, LANES), lambda i: (i, 0)),
            pl.BlockSpec((LANES, tm), lambda i: (0, i)),
            pl.BlockSpec((1, 1, LANES), lambda i: (i, 0, 0)),
        ],
        out_shape=[
            jax.ShapeDtypeStruct((t, D_MODEL), F32),
            jax.ShapeDtypeStruct((t, D_MODEL), BF16),
            jax.ShapeDtypeStruct((t, LANES), F32),
            jax.ShapeDtypeStruct((LANES, t), F32),
            jax.ShapeDtypeStruct((nt, 1, LANES), jnp.int32),
        ],
        compiler_params=pltpu.CompilerParams(
            dimension_semantics=("arbitrary",), vmem_limit_bytes=VMEM_LIMIT),
        name="finish",
    )(x2d, a, mh, z, z, z, wa, wm, wo, g1, b1, wr, br)


MOE_ROWS = 128
MOE_TILE = 1024


def _moe_kernel(cnt_ref, hb_ref, g_ref, gt_ref, wg_ref, bg_ref, wu_ref, bu_ref, wd_ref, bd_ref,
                y_ref, posc_sc, selc_sc, posr_sc, *, cnt_per_tile):
    i = pl.program_id(0)
    e = pl.program_id(1)
    tt = hb_ref.shape[0]
    nchunk = tt // LANES

    @pl.when(e == 0)
    def _():
        y_ref[...] = jnp.zeros(y_ref.shape, F32)

    cnt = cnt_ref[i * cnt_per_tile, e]
    for r in range(1, cnt_per_tile):
        cnt = cnt + cnt_ref[i * cnt_per_tile + r, e]

    @pl.when(cnt > 0)
    def _():
        lane = lax.broadcasted_iota(jnp.int32, (1, LANES), 1)
        gcol = jnp.sum(jnp.where(lane == e, g_ref[...], 0.0), axis=-1, keepdims=True)
        selc = jnp.broadcast_to(jnp.where(gcol > 0.0, 1.0, 0.0), (tt, LANES))
        selc_sc[...] = selc
        blk = 256
        rr = lax.broadcasted_iota(jnp.int32, (blk, blk), 0)
        cc = lax.broadcasted_iota(jnp.int32, (blk, blk), 1)
        strict_lower = jnp.where(rr > cc, 1.0, 0.0).astype(BF16)
        offs = jnp.zeros((1, LANES), F32)
        for bi in range(tt // blk):
            sb = selc[bi * blk:(bi + 1) * blk]
            posc_sc[bi * blk:(bi + 1) * blk, :] = _dot(strict_lower, sb.astype(BF16)) + offs
            offs = offs + jnp.sum(sb, axis=0, keepdims=True)
        grow = gt_ref[pl.ds(lax.rem(e, 8), 1), :]
        selr = jnp.where(grow > 0.0, 1.0, 0.0)
        ru = lax.broadcasted_iota(jnp.int32, (LANES, LANES), 0)
        cu = lax.broadcasted_iota(jnp.int32, (LANES, LANES), 1)
        strict_upper = jnp.where(ru < cu, 1.0, 0.0).astype(BF16)
        offr = jnp.zeros((16, 1), F32)
        for ci in range(nchunk):
            sr = jnp.broadcast_to(selr[:, ci * LANES:(ci + 1) * LANES], (16, LANES))
            posr_sc[:, ci * LANES:(ci + 1) * LANES] = _dot(sr.astype(BF16), strict_upper) + offr
            offr = offr + jnp.sum(sr, axis=-1, keepdims=True)
        slot_sub = lax.broadcasted_iota(jnp.int32, (MOE_ROWS, LANES), 0).astype(F32)
        slot_lane = lax.broadcasted_iota(jnp.int32, (1, LANES), 1).astype(F32)

        def sub_tile(s, carry):
            base = (s * MOE_ROWS).astype(F32)
            posr = posr_sc[0:1, :] - base
            pm_chunks = []
            for ci in range(nchunk):
                pr = posr[:, ci * LANES:(ci + 1) * LANES]
                sr = selr[:, ci * LANES:(ci + 1) * LANES]
                pm_chunks.append(jnp.logical_and(pr == slot_sub, sr > 0.0))
            pmask = jnp.concatenate(pm_chunks, axis=1)
            gs = jnp.sum(jnp.where(pmask, grow, 0.0), axis=-1, keepdims=True)
            xs = _dot(jnp.where(pmask, 1.0, 0.0).astype(BF16), hb_ref[...]).astype(BF16)
            gp = jnp.minimum(_dot(xs, wg_ref[0]) + bg_ref[0], SWIGLU_LIMIT)
            up = jnp.clip(_dot(xs, wu_ref[0]) + bu_ref[0], -SWIGLU_LIMIT, SWIGLU_LIMIT)
            hdn = (up + 1.0) * (gp * jax.nn.sigmoid(SWIGLU_ALPHA * gp))
            yb = (_dot(hdn.astype(BF16), wd_ref[0]) + bd_ref[0]) * gs
            y_hi = yb.astype(BF16)
            y_lo = (yb - y_hi.astype(F32)).astype(BF16)
            pt = jnp.where(jnp.logical_and(posc_sc[...] - base == slot_lane, selc_sc[...] > 0.0),
                           1.0, 0.0).astype(BF16)
            y_ref[...] += _dot(pt, y_hi) + _dot(pt, y_lo)
            return carry

        lax.fori_loop(0, lax.div(cnt + (MOE_ROWS - 1), MOE_ROWS), sub_tile, 0)


def _moe(hb, gates, gates_t, cnt, wg, bg, wu, bu, wd, bd, *, tt):
    t = hb.shape[0]
    nt = t // tt
    cnt_per_tile = cnt.shape[0] // nt
    wspec = pl.BlockSpec((1, D_MODEL, D_MODEL), lambda i, e, c: (e, 0, 0))
    bspec = pl.BlockSpec((1, 1, D_MODEL), lambda i, e, c: (e, 0, 0))
    grid_spec = pltpu.PrefetchScalarGridSpec(
        num_scalar_prefetch=1,
        grid=(nt, N_EXPERTS),
        in_specs=[
            pl.BlockSpec((tt, D_MODEL), lambda i, e, c: (i, 0)),
            pl.BlockSpec((tt, LANES), lambda i, e, c: (i, 0)),
            pl.BlockSpec((8, tt), lambda i, e, c: (e // 8, i)),
            wspec, bspec, wspec, bspec, wspec, bspec,
        ],
        out_specs=pl.BlockSpec((tt, D_MODEL), lambda i, e, c: (i, 0)),
        scratch_shapes=[
            pltpu.VMEM((tt, LANES), F32),
            pltpu.VMEM((tt, LANES), F32),
            pltpu.VMEM((16, tt), F32),
        ],
    )
    return pl.pallas_call(
        functools.partial(_moe_kernel, cnt_per_tile=cnt_per_tile),
        grid_spec=grid_spec,
        out_shape=jax.ShapeDtypeStruct((t, D_MODEL), F32),
        compiler_params=pltpu.CompilerParams(
            dimension_semantics=("arbitrary", "arbitrary"), vmem_limit_bytes=VMEM_LIMIT),
        name="moe",
    )(cnt, hb, gates, gates_t, wg, bg, wu, bu, wd, bd)


def _ln2_kernel(h_ref, y_ref, g_ref, b_ref, o_ref):
    o_ref[...] = _layernorm(DN_ALPHA * h_ref[...] + y_ref[...], g_ref[...], b_ref[...])


def _ln2(h, y, g, b, *, tm):
    t = h.shape[0]
    spec = pl.BlockSpec((tm, D_MODEL), lambda i: (i, 0))
    const = pl.BlockSpec((1, D_MODEL), lambda i: (0, 0))
    return pl.pallas_call(
        _ln2_kernel,
        grid=(t // tm,),
        in_specs=[spec, spec, const, const],
        out_specs=spec,
        out_shape=jax.ShapeDtypeStruct((t, D_MODEL), F32),
        compiler_params=pltpu.CompilerParams(dimension_semantics=("arbitrary",)),
        name="ln2",
    )(h, y, g, b)


def _group(x2d, weights, *, ps, tm, moe_tile, attn, mlstm_state, lc):
    z, kvb, kpages, vpages, gif, kmean = _project(
        x2d, weights["wz"], weights["wkv"], weights["wif"], weights["bif"], ps=ps, tm=tm)
    a = attn(z, kvb, kmean.reshape(kmean.shape[0], W_A), kpages, vpages)
    mh, c_t, n_t, m_t = _mlstm(z, gif, *mlstm_state, lc=lc)
    h, hb, gates, gates_t, cnt = _finish(
        x2d, a, mh, z, weights["wa"], weights["wm"], weights["wo"], weights["g1"], weights["b1"],
        weights["wr"], weights["br"], tm=tm)
    y = _moe(hb, gates, gates_t, cnt.reshape(cnt.shape[0], LANES),
             weights["wg"], weights["bg"], weights["wu"], weights["bu"], weights["wd"], weights["bd"],
             tt=moe_tile)
    out = _ln2(h, y, weights["g2"], weights["b2"], tm=tm)
    return out, kpages, vpages, c_t, n_t, m_t[..., 0]


def kernel(x_prompt, x_sample, cache_k, cache_v, page_table, state_C, state_n, state_m,
           w_in, b_if, w_branch_a, w_branch_m, w_out, ln1_g, ln1_b,
           w_router, b_router, w_gate, b_gate, w_up, b_up, w_down, b_down, ln2_g, ln2_b):
    depth = w_in.shape[0]
    assert depth == 1, "single-layer step"
    bp, seq, _ = x_prompt.shape
    db, dseq, _ = x_sample.shape
    l = 0
    w = w_in[l]
    o_qa, o_ka, o_va, o_qm, o_om_end = 0, W_A, 2 * W_A, 3 * W_A, 3 * W_A + 4 * W_M
    o_if = o_om_end
    o_ga = o_if + 2 * H_M
    weights = {
        "wz": jnp.concatenate([w[:, o_ga:], w[:, o_qa:o_ka], w[:, o_qm:o_om_end]], axis=1).astype(BF16),
        "wkv": w[:, o_ka:o_qm].astype(BF16),
        "wif": jnp.pad(w[:, o_if:o_ga], ((0, 0), (0, LANES - 2 * H_M))),
        "bif": jnp.pad(b_if[l], (0, LANES - 2 * H_M))[None, :],
        "wa": w_branch_a[l].astype(BF16),
        "wm": w_branch_m[l].astype(BF16),
        "wo": w_out[l].astype(BF16),
        "g1": ln1_g[l][None, :], "b1": ln1_b[l][None, :],
        "wr": jnp.pad(w_router[l], ((0, 0), (0, LANES - N_EXPERTS))),
        "br": jnp.pad(b_router[l], (0, LANES - N_EXPERTS), constant_values=NEG_INF)[None, :],
        "wg": w_gate[l].astype(BF16), "bg": b_gate[l][:, None, :],
        "wu": w_up[l].astype(BF16), "bu": b_up[l][:, None, :],
        "wd": w_down[l].astype(BF16), "bd": b_down[l][:, None, :],
        "g2": ln2_g[l][None, :], "b2": ln2_b[l][None, :],
    }
    assert bp == 1, "prompt batch of one sequence"
    zero_state = (jnp.zeros((bp, H_M, DH_M, DH_M), F32), jnp.zeros((bp, H_M, DH_M), F32),
                  jnp.zeros((bp, H_M), F32))
    yp, kp, vp, cp, np_, mp = _group(
        x_prompt.reshape(bp * seq, D_MODEL), weights, ps=PAGE_SIZE, tm=256,
        moe_tile=min(MOE_TILE, bp * seq),
        attn=lambda z, kvb, kmean, kpg, vpg: _moba_prompt(z, kvb, kmean),
        mlstm_state=zero_state, lc=math.gcd(seq, MLSTM_CHUNK))
    ck = cache_k.reshape(cache_k.shape[1:])
    cv = cache_v.reshape(cache_v.shape[1:])
    ys, ks, vs, cs, ns, ms = _group(
        x_sample.reshape(db * dseq, D_MODEL), weights, ps=dseq, tm=256,
        moe_tile=min(MOE_TILE, db * dseq),
        attn=lambda z, kvb, kmean, kpg, vpg: _moba_sample(z, kpg, vpg, ck, cv, page_table),
        mlstm_state=(state_C[l], state_n[l], state_m[l]), lc=math.gcd(dseq, MLSTM_CHUNK))
    n_pg = seq // PAGE_SIZE
    return (yp.reshape(bp, seq, D_MODEL), ys.reshape(db, dseq, D_MODEL),
            kp.reshape(1, bp, n_pg, H_A, PAGE_SIZE, DH_A), vp.reshape(1, bp, n_pg, H_A, PAGE_SIZE, DH_A),
            ks.reshape(1, db, H_A, dseq, DH_A), vs.reshape(1, db, H_A, dseq, DH_A),
            cp[None], np_[None], mp[None], cs[None], ns[None], ms[None])
```

```python
import functools
import math

import jax
import jax.numpy as jnp
from jax import lax
from jax.experimental import pallas as pl
from jax.experimental.pallas import tpu as pltpu

F32 = jnp.float32
BF16 = jnp.bfloat16
HIGHEST = lax.Precision.HIGHEST

D_MODEL = 1024
PAGE_SIZE = 128
H_A = 8
DH_A = 64
W_A = H_A * DH_A
MOBA_BLOCK = 256
MOBA_TOPK = 3
H_M = 4
DH_M = 128
W_M = H_M * DH_M
MLSTM_CHUNK = 64
N_EXPERTS = 32
TOP_K = 4
SWIGLU_LIMIT = 7.0
SWIGLU_ALPHA = 1.702
DN_ALPHA = 2.0 ** 0.25
LN_EPS = 1e-5
NEG_INF = -1e30

LANES = 128
Z_GA, Z_GM, Z_QA, Z_QM, Z_KM, Z_VM, Z_OM = 0, 1024, 2048, 2560, 3072, 3584, 4096
Z_WIDTH = 4608
VMEM_LIMIT = 56 * 1024 * 1024


def _nt_dot(a, b, precision=None):
    return lax.dot_general(a, b, (((1,), (1,)), ((), ())), precision=precision,
                           preferred_element_type=F32)


def _dot(a, b, precision=None):
    return jnp.dot(a, b, precision=precision, preferred_element_type=F32)


def _top_mask(scores, idx_f, k, axis=-1):
    n = scores.shape[axis]
    sel = jnp.zeros(scores.shape, jnp.bool_)
    work = scores
    first_max = None
    for _ in range(k):
        mx = jnp.max(work, axis=axis, keepdims=True)
        if first_max is None:
            first_max = mx
        cand = work == mx
        pick_idx = jnp.min(jnp.where(cand, idx_f, float(n)), axis=axis, keepdims=True)
        pick = idx_f == pick_idx
        sel = jnp.logical_or(sel, pick)
        work = jnp.where(pick, -jnp.inf, work)
    return sel, first_max


def _proj_kernel(x_ref, wz_ref, wkv_ref, wif_ref, bif_ref,
                 z_ref, kb_ref, vt_ref, kp_ref, vp_ref, gif_ref, kmean_ref, *, ps):
    x = x_ref[...]
    tm = x.shape[0]
    xb = x.astype(BF16)
    gif_ref[...] = _dot(x, wif_ref[...], precision=HIGHEST) + bif_ref[...]
    for c in range(Z_WIDTH // 512):
        z_ref[:, c * 512:(c + 1) * 512] = _dot(xb, wz_ref[:, c * 512:(c + 1) * 512])
    for c, pref in ((0, kp_ref), (1, vp_ref)):
        r = _dot(xb, wkv_ref[:, c * W_A:(c + 1) * W_A])
        if c == 0:
            kb_ref[...] = r.astype(BF16)
        else:
            vt_ref[0] = r.T.astype(BF16)
        for h in range(H_A):
            pref[:, h, :, :] = r[:, h * DH_A:(h + 1) * DH_A].reshape(tm // ps, ps, DH_A)
        if c == 0:
            kmean_ref[...] = jnp.mean(r.reshape(tm // MOBA_BLOCK, MOBA_BLOCK, W_A), axis=1, keepdims=True)


def _project(x2d, wz, wkv, wif, bif, *, ps, tm):
    t = x2d.shape[0]
    assert tm == MOBA_BLOCK, "one MoBA block of keys per row tile"
    grid = (t // tm,)
    const = lambda i: (0, 0)
    return pl.pallas_call(
        functools.partial(_proj_kernel, ps=ps),
        grid=grid,
        in_specs=[
            pl.BlockSpec((tm, D_MODEL), lambda i: (i, 0)),
            pl.BlockSpec(wz.shape, const),
            pl.BlockSpec(wkv.shape, const),
            pl.BlockSpec(wif.shape, const),
            pl.BlockSpec(bif.shape, const),
        ],
        out_specs=[
            pl.BlockSpec((tm, Z_WIDTH), lambda i: (i, 0)),
            pl.BlockSpec((tm, W_A), lambda i: (i, 0)),
            pl.BlockSpec((1, W_A, tm), lambda i: (i, 0, 0)),
            pl.BlockSpec((tm // ps, H_A, ps, DH_A), lambda i: (i, 0, 0, 0)),
            pl.BlockSpec((tm // ps, H_A, ps, DH_A), lambda i: (i, 0, 0, 0)),
            pl.BlockSpec((tm, LANES), lambda i: (i, 0)),
            pl.BlockSpec((tm // MOBA_BLOCK, 1, W_A), lambda i: (i, 0, 0)),
        ],
        out_shape=[
            jax.ShapeDtypeStruct((t, Z_WIDTH), F32),
            jax.ShapeDtypeStruct((t, W_A), BF16),
            jax.ShapeDtypeStruct((t // tm, W_A, tm), BF16),
            jax.ShapeDtypeStruct((t // ps, H_A, ps, DH_A), F32),
            jax.ShapeDtypeStruct((t // ps, H_A, ps, DH_A), F32),
            jax.ShapeDtypeStruct((t, LANES), F32),
            jax.ShapeDtypeStruct((t // MOBA_BLOCK, 1, W_A), F32),
        ],
        compiler_params=pltpu.CompilerParams(
            dimension_semantics=("arbitrary",), vmem_limit_bytes=VMEM_LIMIT),
        name="proj",
    )(x2d, wz, wkv, wif, bif)


PAST_UNROLL = 4


LOG2E = 1.4426950408889634


def _split3(x):
    hi = x.astype(BF16)
    r1 = x - hi.astype(F32)
    mid = r1.astype(BF16)
    lo = (r1 - mid.astype(F32)).astype(BF16)
    return hi, mid, lo


def _moba_prompt_kernel(q_ref, k_ref, vt_ref, km_ref, o_ref, m_sc, acc_sc, sel_sc):
    hp = pl.program_id(0)
    j = pl.program_id(1)
    bq = MOBA_BLOCK
    nb = km_ref.shape[0]
    q = q_ref[...]
    lane = lax.broadcasted_iota(jnp.int32, (1, LANES), 1)
    blk_f = lax.broadcasted_iota(jnp.int32, (nb, 1), 0).astype(F32)
    jf = j.astype(F32)
    rk = lax.broadcasted_iota(jnp.int32, (bq, bq), 0)
    rq = lax.broadcasted_iota(jnp.int32, (bq, bq), 1)
    causal = rq >= rk
    pos = lax.broadcasted_iota(jnp.int32, (bq, 1), 0).astype(F32)
    k_extra = jnp.where(lane < 3, pos, jnp.where(lane < 6, 1.0, 0.0)).astype(BF16)
    vrow = lax.broadcasted_iota(jnp.int32, (LANES, 1), 0)
    ones_row = (DH_A, 0)
    vrow_b = lax.broadcasted_iota(jnp.int32, (LANES, bq), 0).astype(F32).astype(BF16)
    is_ones_row = [vrow_b == r for r in ones_row]
    k_own = jnp.concatenate([k_ref[pl.ds(pl.multiple_of(j * bq, bq), bq), :], k_extra], axis=1)
    vt_own = vt_ref[j]
    q_augs = []
    coefs = []
    for hh in range(2):
        hmask = jnp.logical_and(lane >= hh * DH_A, lane < (hh + 1) * DH_A)
        qh = jnp.where(hmask, q, 0.0)
        head_f = (2 * hp + hh + 1).astype(F32)
        coef = jnp.exp2(jnp.zeros((1, 1), F32) - head_f) * LOG2E
        coefs.append(coef)
        scores = _nt_dot(km_ref[...], qh, precision=HIGHEST)
        scores = jnp.where(blk_f < jf, scores, NEG_INF)
        sel, _ = _top_mask(scores, blk_f, min(MOBA_TOPK, nb), axis=0)
        sel = jnp.logical_and(sel, blk_f < jf)
        sel_sc[hh] = jnp.where(sel, 1.0, 0.0)
        c3 = _split3(jnp.broadcast_to(coef, (bq, 1)))
        q3 = _split3(-(coef * pos))
        q_extra = jnp.zeros((bq, LANES), F32)
        for li, term in enumerate(c3 + q3):
            q_extra = jnp.where(lane == li, term.astype(F32), q_extra)
        qb = (qh * (DH_A ** -0.5 * LOG2E)).astype(BF16)
        q_aug = jnp.concatenate([qb, q_extra.astype(BF16)], axis=1)
        q_augs.append(q_aug)
        logits = jnp.where(causal, _nt_dot(k_own, q_aug), NEG_INF)
        m = jnp.max(logits, axis=0, keepdims=True)
        p = jnp.exp2(logits - m)
        m_sc[hh] = m
        vt_h = jnp.where(is_ones_row[hh], jnp.ones((), BF16), vt_own)
        acc_sc[hh] = _dot(vt_h, p.astype(BF16))

    def body(it, carry):
        ns, kns, vtns, offs, valids = [], [], [], [], []
        for u in range(PAST_UNROLL):
            n_raw = it * PAST_UNROLL + u
            n = jnp.minimum(n_raw, j - 1)
            ns.append(n)
            valids.append(n_raw < j)
            kns.append(jnp.concatenate([k_ref[pl.ds(pl.multiple_of(n * bq, bq), bq), :], k_extra], axis=1))
            vtns.append(vt_ref[n])
            offs.append((j - n).astype(F32) * float(bq))
        for hh in range(2):
            logits = []
            for u in range(PAST_UNROLL):
                picked = sel_sc[hh, pl.ds(ns[u], 1), :]
                keep = jnp.logical_and(picked > 0.5, valids[u])
                rb = jnp.where(keep, -(coefs[hh] * offs[u]), NEG_INF)
                logits.append(_nt_dot(kns[u], q_augs[hh]) + rb)
            m_old = m_sc[hh]
            m_new = m_old
            for lg in logits:
                m_new = jnp.maximum(m_new, jnp.max(lg, axis=0, keepdims=True))
            acc_new = jnp.exp2(m_old - m_new) * acc_sc[hh]
            for u in range(PAST_UNROLL):
                p = jnp.exp2(logits[u] - m_new)
                vt_h = jnp.where(is_ones_row[hh], jnp.ones((), BF16), vtns[u])
                acc_new = acc_new + _dot(vt_h, p.astype(BF16))
            acc_sc[hh] = acc_new
            m_sc[hh] = m_new
        return carry

    lax.fori_loop(0, lax.div(j + (PAST_UNROLL - 1), PAST_UNROLL), body, 0)
    acc0 = acc_sc[0]
    acc1 = acc_sc[1]
    o0 = acc0 / acc0[ones_row[0]:ones_row[0] + 1, :]
    o1 = acc1 / acc1[ones_row[1]:ones_row[1] + 1, :]
    o_ref[...] = jnp.where(vrow < DH_A, o0, o1).T


def _moba_prompt(z, kb, vt, kmean):
    t = z.shape[0]
    nb = t // MOBA_BLOCK
    bq = MOBA_BLOCK
    qcol = Z_QA // LANES
    return pl.pallas_call(
        _moba_prompt_kernel,
        grid=(H_A // 2, nb),
        in_specs=[
            pl.BlockSpec((bq, LANES), lambda hp, j: (j, qcol + hp)),
            pl.BlockSpec((t, LANES), lambda hp, j: (0, hp)),
            pl.BlockSpec((nb, LANES, bq), lambda hp, j: (0, hp, 0)),
            pl.BlockSpec((nb, LANES), lambda hp, j: (0, hp)),
        ],
        out_specs=pl.BlockSpec((bq, LANES), lambda hp, j: (j, hp)),
        out_shape=jax.ShapeDtypeStruct((t, W_A), F32),
        scratch_shapes=[
            pltpu.VMEM((2, 1, bq), F32),
            pltpu.VMEM((2, LANES, bq), F32),
            pltpu.VMEM((2, nb, bq), F32),
        ],
        compiler_params=pltpu.CompilerParams(
            dimension_semantics=("arbitrary", "arbitrary"), vmem_limit_bytes=VMEM_LIMIT),
        name="moba_prompt",
    )(z, kb, vt, kmean)


def _moba_sample_kernel(pt_ref, q_ref, kn_ref, vn_ref, *refs, ppg, past):
    k_refs = refs[:ppg]
    v_refs = refs[ppg:2 * ppg]
    o_ref = refs[2 * ppg]
    qs_sc, qf_sc, ksum_sc, m_sc, l_sc, o_sc = refs[2 * ppg + 1:]
    g = pl.program_id(1)
    ng = pl.num_programs(1)
    t = q_ref.shape[0]
    rows = H_A * t
    npb = o_sc.shape[0]
    ppb = MOBA_BLOCK // PAGE_SIZE
    row_i = lax.broadcasted_iota(jnp.int32, (rows, 1), 0)
    slope = jnp.exp2(-(lax.div(row_i, t) + 1).astype(F32))
    tok = lax.rem(row_i, t)
    blk_lane = lax.broadcasted_iota(jnp.int32, (1, LANES), 1)
    feat_head = lax.div(lax.broadcasted_iota(jnp.int32, (1, W_A), 1), DH_A)

    @pl.when(g == 0)
    def _():
        q = q_ref[...]
        qbd = jnp.concatenate([jnp.where(feat_head == h, q, 0.0) for h in range(H_A)], axis=0)
        qf_sc[...] = qbd
        qs_sc[...] = qbd * (DH_A ** -0.5)
        ksum_sc[...] = jnp.zeros(ksum_sc.shape, F32)
        m_sc[...] = jnp.zeros(m_sc.shape, F32)
        l_sc[...] = jnp.zeros(l_sc.shape, F32)

    qs = qs_sc[...]
    rk = lax.broadcasted_iota(jnp.int32, (1, MOBA_BLOCK), 1)
    for b in range(ppg // ppb):
        n = g * (ppg // ppb) + b
        kts = [k_refs[b * ppb + p][...].reshape(W_A, PAGE_SIZE) for p in range(ppb)]
        vts = [v_refs[b * ppb + p][...].reshape(W_A, PAGE_SIZE) for p in range(ppb)]
        kt_sum = kts[0]
        for p in range(1, ppb):
            kt_sum = kt_sum + kts[p]
        ksum = jnp.sum(kt_sum, axis=-1, keepdims=True)
        ksum_sc[...] = jnp.where(blk_lane == n, ksum, ksum_sc[...])
        s = jnp.concatenate([_dot(qs, kt) for kt in kts], axis=-1)
        dist = ((past - n * MOBA_BLOCK) + tok - rk).astype(F32)
        logits = s - slope * dist
        m_b = jnp.max(logits, axis=-1, keepdims=True)
        p_ = jnp.exp(logits - m_b)
        l_b = jnp.sum(p_, axis=-1, keepdims=True)
        o_b = _nt_dot(p_[:, :PAGE_SIZE], vts[0])
        for p in range(1, ppb):
            o_b = o_b + _nt_dot(p_[:, p * PAGE_SIZE:(p + 1) * PAGE_SIZE], vts[p])
        o_sc[n] = o_b
        m_sc[...] = jnp.where(blk_lane == n, m_b, m_sc[...])
        l_sc[...] = jnp.where(blk_lane == n, l_b, l_sc[...])

    @pl.when(g == ng - 1)
    def _():
        kmean = ksum_sc[...] * (1.0 / MOBA_BLOCK)
        scores = _dot(qf_sc[...], kmean, precision=HIGHEST)
        scores = jnp.where(blk_lane < npb, scores, -jnp.inf)
        sel, _ = _top_mask(scores, blk_lane.astype(F32), min(MOBA_TOPK, npb))
        kn = jnp.concatenate([kn_ref[h] for h in range(H_A)], axis=1)
        vn = jnp.concatenate([vn_ref[h] for h in range(H_A)], axis=1)
        s_own = _nt_dot(qs, kn)
        d_own = (tok - lax.broadcasted_iota(jnp.int32, (1, t), 1)).astype(F32)
        lg_own = jnp.where(d_own >= 0, s_own - slope * d_own, NEG_INF)
        m_all = m_sc[...]
        m_sel = jnp.max(jnp.where(sel, m_all, -jnp.inf), axis=-1, keepdims=True)
        m_fin = jnp.maximum(m_sel, jnp.max(lg_own, axis=-1, keepdims=True))
        w = jnp.where(sel, jnp.exp(m_all - m_fin), 0.0)
        p_own = jnp.exp(lg_own - m_fin)
        l_fin = jnp.sum(w * l_sc[...], axis=-1, keepdims=True) + jnp.sum(p_own, axis=-1, keepdims=True)
        o = _dot(p_own, vn)
        for n in range(npb):
            o = o + w[:, n:n + 1] * o_sc[n]
        o = o / l_fin
        out = jnp.zeros((t, W_A), F32)
        for h in range(H_A):
            out = out + jnp.where(feat_head == h, o[h * t:(h + 1) * t, :], 0.0)
        o_ref[...] = out


def _moba_sample(z, k_new, v_new, cache_k, cache_v, page_table):
    db, n_pages = page_table.shape
    t = z.shape[0] // db
    ppb = MOBA_BLOCK // PAGE_SIZE
    npb = n_pages // ppb
    assert n_pages == npb * ppb, "cached pages must fill whole MoBA blocks"
    ppg = 8 if n_pages % 8 == 0 else ppb
    ng = n_pages // ppg

    def page_map(b, g, pt, *, i):
        return (pt[b * n_pages + g * ppg + i], 0, 0, 0)

    page_specs = [pl.BlockSpec((None, H_A, DH_A, PAGE_SIZE), functools.partial(page_map, i=i))
                  for i in range(ppg)]
    grid_spec = pltpu.PrefetchScalarGridSpec(
        num_scalar_prefetch=1,
        grid=(db, ng),
        in_specs=[
            pl.BlockSpec((t, W_A), lambda b, g, pt: (b, Z_QA // W_A)),
            pl.BlockSpec((None, H_A, t, DH_A), lambda b, g, pt: (b, 0, 0, 0)),
            pl.BlockSpec((None, H_A, t, DH_A), lambda b, g, pt: (b, 0, 0, 0)),
        ] + page_specs + page_specs,
        out_specs=pl.BlockSpec((t, W_A), lambda b, g, pt: (b, 0)),
        scratch_shapes=[
            pltpu.VMEM((H_A * t, W_A), F32),
            pltpu.VMEM((H_A * t, W_A), F32),
            pltpu.VMEM((W_A, LANES), F32),
            pltpu.VMEM((H_A * t, LANES), F32),
            pltpu.VMEM((H_A * t, LANES), F32),
            pltpu.VMEM((npb, H_A * t, W_A), F32),
        ],
    )
    return pl.pallas_call(
        functools.partial(_moba_sample_kernel, ppg=ppg, past=n_pages * PAGE_SIZE),
        grid_spec=grid_spec,
        out_shape=jax.ShapeDtypeStruct((db * t, W_A), F32),
        compiler_params=pltpu.CompilerParams(
            dimension_semantics=("arbitrary", "arbitrary"), vmem_limit_bytes=VMEM_LIMIT),
        name="moba_sample",
    )(page_table.reshape(-1), z, k_new, v_new, *([cache_k] * ppg), *([cache_v] * ppg))


def _mlstm_kernel(q_ref, k_ref, v_ref, g_ref, c0_ref, n0_ref, m0_ref,
                  h_ref, c_ref, n_ref, m_ref, c_sc, n_sc, m_sc, *, lc):
    c = pl.program_id(1)
    nc = pl.num_programs(1)
    lp = max(lc, MLSTM_CHUNK)

    @pl.when(c == 0)
    def _():
        c_sc[...] = c0_ref[0]
        n_sc[...] = n0_ref[0]
        m_sc[...] = m0_ref[0]

    def padded(a):
        if a.shape[0] == lp:
            return a
        return jnp.concatenate([a, jnp.zeros((lp - a.shape[0],) + a.shape[1:], a.dtype)], axis=0)

    q = padded(q_ref[...])
    k = padded(k_ref[...]) * (DH_M ** -0.5)
    v = padded(v_ref[...])
    gts = padded(g_ref[...])
    lf = jax.nn.log_sigmoid(gts)
    row = lax.broadcasted_iota(jnp.int32, (lp, lp), 0)
    col = lax.broadcasted_iota(jnp.int32, (lp, lp), 1)
    causal = row >= col
    bcum = _dot(jnp.where(causal, 1.0, 0.0), lf, precision=HIGHEST)
    pick_r = lax.broadcasted_iota(jnp.int32, (8, LANES), 0)
    pick_l = lax.broadcasted_iota(jnp.int32, (8, LANES), 1)
    bcum_t = _nt_dot(jnp.where(pick_l == pick_r + H_M, 1.0, 0.0), bcum, precision=HIGHEST)
    gts_t = _nt_dot(jnp.where(pick_l == pick_r, 1.0, 0.0), gts, precision=HIGHEST)
    eye = jnp.where(lax.broadcasted_iota(jnp.int32, (DH_M, DH_M), 0)
                    == lax.broadcasted_iota(jnp.int32, (DH_M, DH_M), 1), 1.0, 0.0).astype(BF16)
    rvalid = lax.broadcasted_iota(jnp.int32, (lp, 1), 0) < lc
    for h in range(H_M):
        qh = q[:, h * DH_M:(h + 1) * DH_M]
        kh = k[:, h * DH_M:(h + 1) * DH_M]
        vh = v[:, h * DH_M:(h + 1) * DH_M]
        bc = bcum[:, H_M + h:H_M + h + 1]
        ic = gts[:, h:h + 1]
        br = bcum_t[h:h + 1, :]
        ir = gts_t[h:h + 1, :]
        m_prev = m_sc[h:h + 1, :]
        c_prev = c_sc[h]
        n_prev = n_sc[h:h + 1, :]
        dmat = jnp.where(causal, bc - br + ir, -jnp.inf)
        a_inter = bc + m_prev
        m_t = jnp.maximum(a_inter, jnp.max(dmat, axis=-1, keepdims=True))
        w_inter = jnp.exp(a_inter - m_t)
        qhb = qh.astype(BF16)
        s = _nt_dot(qhb, kh.astype(BF16)) * jnp.exp(dmat - m_t)
        num = w_inter * _dot(qhb, c_prev.astype(BF16)) + _dot(s.astype(BF16), vh.astype(BF16))
        den = (w_inter * jnp.sum(qh * n_prev, axis=-1, keepdims=True)
               + jnp.sum(s, axis=-1, keepdims=True))
        h_out = num / jnp.maximum(jnp.abs(den), jnp.exp(-m_t))
        h_ref[:, h * DH_M:(h + 1) * DH_M] = h_out[:lc]
        m_new = m_t[lc - 1:lc, :]
        b_last = bc[lc - 1:lc, :]
        g_c = jnp.exp(b_last + m_prev - m_new)
        g_s = jnp.where(rvalid, jnp.exp(b_last - bc + ic - m_new), 0.0)
        kg = g_s * kh
        kg_t = _nt_dot(eye, kg.astype(BF16)).astype(BF16)
        c_sc[h] = g_c * c_prev + _dot(kg_t, vh.astype(BF16))
        n_sc[h:h + 1, :] = g_c * n_prev + jnp.sum(kg, axis=0, keepdims=True)
        m_sc[h:h + 1, :] = m_new

    @pl.when(c == nc - 1)
    def _():
        c_ref[0] = c_sc[...]
        n_ref[0] = n_sc[...]
        m_ref[0] = m_sc[...]


def _mlstm(z, gif, c0, n0, m0, *, lc):
    b = c0.shape[0]
    t = z.shape[0] // b
    nc = t // lc
    row = lambda bi, ci: bi * nc + ci
    wcol = lambda off: off // W_M
    return pl.pallas_call(
        functools.partial(_mlstm_kernel, lc=lc),
        grid=(b, nc),
        in_specs=[
            pl.BlockSpec((lc, W_M), lambda bi, ci: (row(bi, ci), wcol(Z_QM))),
            pl.BlockSpec((lc, W_M), lambda bi, ci: (row(bi, ci), wcol(Z_KM))),
            pl.BlockSpec((lc, W_M), lambda bi, ci: (row(bi, ci), wcol(Z_VM))),
            pl.BlockSpec((lc, LANES), lambda bi, ci: (row(bi, ci), 0)),
            pl.BlockSpec((1, H_M, DH_M, DH_M), lambda bi, ci: (bi, 0, 0, 0)),
            pl.BlockSpec((1, H_M, DH_M), lambda bi, ci: (bi, 0, 0)),
            pl.BlockSpec((1, H_M, 1), lambda bi, ci: (bi, 0, 0)),
        ],
        out_specs=[
            pl.BlockSpec((lc, W_M), lambda bi, ci: (row(bi, ci), 0)),
            pl.BlockSpec((1, H_M, DH_M, DH_M), lambda bi, ci: (bi, 0, 0, 0)),
            pl.BlockSpec((1, H_M, DH_M), lambda bi, ci: (bi, 0, 0)),
            pl.BlockSpec((1, H_M, 1), lambda bi, ci: (bi, 0, 0)),
        ],
        out_shape=[
            jax.ShapeDtypeStruct((b * t, W_M), F32),
            jax.ShapeDtypeStruct((b, H_M, DH_M, DH_M), F32),
            jax.ShapeDtypeStruct((b, H_M, DH_M), F32),
            jax.ShapeDtypeStruct((b, H_M, 1), F32),
        ],
        scratch_shapes=[
            pltpu.VMEM((H_M, DH_M, DH_M), F32),
            pltpu.VMEM((H_M, DH_M), F32),
            pltpu.VMEM((H_M, 1), F32),
        ],
        compiler_params=pltpu.CompilerParams(
            dimension_semantics=("arbitrary", "arbitrary"), vmem_limit_bytes=VMEM_LIMIT),
        name="mlstm",
    )(z, z, z, gif, c0, n0, m0[..., None])


def _layernorm(y, g, b):
    mu = jnp.mean(y, axis=-1, keepdims=True)
    var = jnp.mean(jnp.square(y - mu), axis=-1, keepdims=True)
    return (y - mu) * lax.rsqrt(var + LN_EPS) * g + b


def _finish_kernel(x_ref, a_ref, mh_ref, om_ref, ga_ref, gm_ref, wa_ref, wm_ref, wo_ref,
                   g1_ref, b1_ref, wr_ref, br_ref,
                   h_ref, hb_ref, gates_ref, gates_t_ref, cnt_ref):
    a = _dot(a_ref[...].astype(BF16), wa_ref[...])
    mo = mh_ref[...] * jax.nn.sigmoid(om_ref[...])
    m = _dot(mo.astype(BF16), wm_ref[...])
    merged = jax.nn.sigmoid(ga_ref[...]) * a + jax.nn.sigmoid(gm_ref[...]) * m
    y = DN_ALPHA * x_ref[...] + _dot(merged.astype(BF16), wo_ref[...])
    h = _layernorm(y, g1_ref[...], b1_ref[...])
    h_ref[...] = h
    hb_ref[...] = h.astype(BF16)
    logits = _dot(h, wr_ref[...], precision=HIGHEST) + br_ref[...]
    lane_f = lax.broadcasted_iota(jnp.int32, (1, LANES), 1).astype(F32)
    sel, top = _top_mask(logits, lane_f, TOP_K)
    e = jnp.where(sel, jnp.exp(logits - top), 0.0)
    gates = e / jnp.sum(e, axis=-1, keepdims=True)
    gates_ref[...] = gates
    gates_t_ref[...] = gates.T
    cnt_ref[0] = jnp.sum(jnp.where(gates > 0.0, 1.0, 0.0), axis=0, keepdims=True).astype(jnp.int32)


def _finish(x2d, a, mh, z, wa, wm, wo, g1, b1, wr, br, *, tm):
    t = x2d.shape[0]
    nt = t // tm
    const = lambda i: (0, 0)
    return pl.pallas_call(
        _finish_kernel,
        grid=(nt,),
        in_specs=[
            pl.BlockSpec((tm, D_MODEL), lambda i: (i, 0)),
            pl.BlockSpec((tm, W_A), lambda i: (i, 0)),
            pl.BlockSpec((tm, W_M), lambda i: (i, 0)),
            pl.BlockSpec((tm, W_M), lambda i: (i, Z_OM // W_M)),
            pl.BlockSpec((tm, D_MODEL), lambda i: (i, Z_GA // D_MODEL)),
            pl.BlockSpec((tm, D_MODEL), lambda i: (i, Z_GM // D_MODEL)),
            pl.BlockSpec(wa.shape, const),
            pl.BlockSpec(wm.shape, const),
            pl.BlockSpec(wo.shape, const),
            pl.BlockSpec(g1.shape, const),
            pl.BlockSpec(b1.shape, const),
            pl.BlockSpec(wr.shape, const),
            pl.BlockSpec(br.shape, const),
        ],
        out_specs=[
            pl.BlockSpec((tm, D_MODEL), lambda i: (i, 0)),
            pl.BlockSpec((tm, D_MODEL), lambda i: (i, 0)),
            pl.BlockSpec((tm, LANES), lambda i: (i, 0)),
            pl.BlockSpec((LANES, tm), lambda i: (0, i)),
            pl.BlockSpec((1, 1, LANES), lambda i: (i, 0, 0)),
        ],
        out_shape=[
            jax.ShapeDtypeStruct((t, D_MODEL), F32),
            jax.ShapeDtypeStruct((t, D_MODEL), BF16),
            jax.ShapeDtypeStruct((t, LANES), F32),
            jax.ShapeDtypeStruct((LANES, t), F32),
            jax.ShapeDtypeStruct((nt, 1, LANES), jnp.int32),
        ],
        compiler_params=pltpu.CompilerParams(
            dimension_semantics=("arbitrary",), vmem_limit_bytes=VMEM_LIMIT),
        name="finish",
    )(x2d, a, mh, z, z, z, wa, wm, wo, g1, b1, wr, br)


MOE_ROWS = 128
MOE_TILE = 1024


def _moe_kernel(cnt_ref, hb_ref, g_ref, gt_ref, wg_ref, bg_ref, wu_ref, bu_ref, wd_ref, bd_ref,
                y_ref, posc_sc, selc_sc, posr_sc, *, cnt_per_tile):
    i = pl.program_id(0)
    e = pl.program_id(1)
    tt = hb_ref.shape[0]
    nchunk = tt // LANES

    @pl.when(e == 0)
    def _():
        y_ref[...] = jnp.zeros(y_ref.shape, F32)

    cnt = cnt_ref[i * cnt_per_tile, e]
    for r in range(1, cnt_per_tile):
        cnt = cnt + cnt_ref[i * cnt_per_tile + r, e]

    @pl.when(cnt > 0)
    def _():
        lane = lax.broadcasted_iota(jnp.int32, (1, LANES), 1)
        gcol = jnp.sum(jnp.where(lane == e, g_ref[...], 0.0), axis=-1, keepdims=True)
        selc = jnp.broadcast_to(jnp.where(gcol > 0.0, 1.0, 0.0), (tt, LANES))
        selc_sc[...] = selc
        blk = 256
        rr = lax.broadcasted_iota(jnp.int32, (blk, blk), 0)
        cc = lax.broadcasted_iota(jnp.int32, (blk, blk), 1)
        strict_lower = jnp.where(rr > cc, 1.0, 0.0).astype(BF16)
        offs = jnp.zeros((1, LANES), F32)
        for bi in range(tt // blk):
            sb = selc[bi * blk:(bi + 1) * blk]
            posc_sc[bi * blk:(bi + 1) * blk, :] = _dot(strict_lower, sb.astype(BF16)) + offs
            offs = offs + jnp.sum(sb, axis=0, keepdims=True)
        grow = gt_ref[pl.ds(lax.rem(e, 8), 1), :]
        selr = jnp.where(grow > 0.0, 1.0, 0.0)
        ru = lax.broadcasted_iota(jnp.int32, (LANES, LANES), 0)
        cu = lax.broadcasted_iota(jnp.int32, (LANES, LANES), 1)
        strict_upper = jnp.where(ru < cu, 1.0, 0.0).astype(BF16)
        offr = jnp.zeros((16, 1), F32)
        for ci in range(nchunk):
            sr = jnp.broadcast_to(selr[:, ci * LANES:(ci + 1) * LANES], (16, LANES))
            posr_sc[:, ci * LANES:(ci + 1) * LANES] = _dot(sr.astype(BF16), strict_upper) + offr
            offr = offr + jnp.sum(sr, axis=-1, keepdims=True)
        slot_sub = lax.broadcasted_iota(jnp.int32, (MOE_ROWS, LANES), 0).astype(F32)
        slot_lane = lax.broadcasted_iota(jnp.int32, (1, LANES), 1).astype(F32)

        def sub_tile(s, carry):
            base = (s * MOE_ROWS).astype(F32)
            posr = posr_sc[0:1, :] - base
            pm_chunks = []
            for ci in range(nchunk):
                pr = posr[:, ci * LANES:(ci + 1) * LANES]
                sr = selr[:, ci * LANES:(ci + 1) * LANES]
                pm_chunks.append(jnp.logical_and(pr == slot_sub, sr > 0.0))
            pmask = jnp.concatenate(pm_chunks, axis=1)
            gs = jnp.sum(jnp.where(pmask, grow, 0.0), axis=-1, keepdims=True)
            xs = _dot(jnp.where(pmask, 1.0, 0.0).astype(BF16), hb_ref[...]).astype(BF16)
            gp = jnp.minimum(_dot(xs, wg_ref[0]) + bg_ref[0], SWIGLU_LIMIT)
            up = jnp.clip(_dot(xs, wu_ref[0]) + bu_ref[0], -SWIGLU_LIMIT, SWIGLU_LIMIT)
            hdn = (up + 1.0) * (gp * jax.nn.sigmoid(SWIGLU_ALPHA * gp))
            yb = (_dot(hdn.astype(BF16), wd_ref[0]) + bd_ref[0]) * gs
            pt = jnp.where(jnp.logical_and(posc_sc[...] - base == slot_lane, selc_sc[...] > 0.0),
                           1.0, 0.0).astype(BF16)
            y_ref[...] += _dot(pt, yb.astype(BF16))
            return carry

        lax.fori_loop(0, lax.div(cnt + (MOE_ROWS - 1), MOE_ROWS), sub_tile, 0)


def _moe(hb, gates, gates_t, cnt, wg, bg, wu, bu, wd, bd, *, tt):
    t = hb.shape[0]
    nt = t // tt
    cnt_per_tile = cnt.shape[0] // nt
    wspec = pl.BlockSpec((1, D_MODEL, D_MODEL), lambda i, e, c: (e, 0, 0))
    bspec = pl.BlockSpec((1, 1, D_MODEL), lambda i, e, c: (e, 0, 0))
    grid_spec = pltpu.PrefetchScalarGridSpec(
        num_scalar_prefetch=1,
        grid=(nt, N_EXPERTS),
        in_specs=[
            pl.BlockSpec((tt, D_MODEL), lambda i, e, c: (i, 0)),
            pl.BlockSpec((tt, LANES), lambda i, e, c: (i, 0)),
            pl.BlockSpec((8, tt), lambda i, e, c: (e // 8, i)),
            wspec, bspec, wspec, bspec, wspec, bspec,
        ],
        out_specs=pl.BlockSpec((tt, D_MODEL), lambda i, e, c: (i, 0)),
        scratch_shapes=[
            pltpu.VMEM((tt, LANES), F32),
            pltpu.VMEM((tt, LANES), F32),
            pltpu.VMEM((16, tt), F32),
        ],
    )
    return pl.pallas_call(
        functools.partial(_moe_kernel, cnt_per_tile=cnt_per_tile),
        grid_spec=grid_spec,
        out_shape=jax.ShapeDtypeStruct((t, D_MODEL), F32),
        compiler_params=pltpu.CompilerParams(
            dimension_semantics=("arbitrary", "arbitrary"), vmem_limit_bytes=VMEM_LIMIT),
        name="moe",
    )(cnt, hb, gates, gates_t, wg, bg, wu, bu, wd, bd)


def _ln2_kernel(h_ref, y_ref, g_ref, b_ref, o_ref):
    o_ref[...] = _layernorm(DN_ALPHA * h_ref[...] + y_ref[...], g_ref[...], b_ref[...])


def _ln2(h, y, g, b, *, tm):
    t = h.shape[0]
    spec = pl.BlockSpec((tm, D_MODEL), lambda i: (i, 0))
    const = pl.BlockSpec((1, D_MODEL), lambda i: (0, 0))
    return pl.pallas_call(
        _ln2_kernel,
        grid=(t // tm,),
        in_specs=[spec, spec, const, const],
        out_specs=spec,
        out_shape=jax.ShapeDtypeStruct((t, D_MODEL), F32),
        compiler_params=pltpu.CompilerParams(dimension_semantics=("arbitrary",)),
        name="ln2",
    )(h, y, g, b)


def _group(x2d, weights, *, ps, tm, moe_tile, attn, mlstm_state, lc):
    z, kb, vt, kpages, vpages, gif, kmean = _project(
        x2d, weights["wz"], weights["wkv"], weights["wif"], weights["bif"], ps=ps, tm=tm)
    a = attn(z, kb, vt, kmean.reshape(kmean.shape[0], W_A), kpages, vpages)
    mh, c_t, n_t, m_t = _mlstm(z, gif, *mlstm_state, lc=lc)
    h, hb, gates, gates_t, cnt = _finish(
        x2d, a, mh, z, weights["wa"], weights["wm"], weights["wo"], weights["g1"], weights["b1"],
        weights["wr"], weights["br"], tm=tm)
    y = _moe(hb, gates, gates_t, cnt.reshape(cnt.shape[0], LANES),
             weights["wg"], weights["bg"], weights["wu"], weights["bu"], weights["wd"], weights["bd"],
             tt=moe_tile)
    out = _ln2(h, y, weights["g2"], weights["b2"], tm=tm)
    return out, kpages, vpages, c_t, n_t, m_t[..., 0]


def kernel(x_prompt, x_sample, cache_k, cache_v, page_table, state_C, state_n, state_m,
           w_in, b_if, w_branch_a, w_branch_m, w_out, ln1_g, ln1_b,
           w_router, b_router, w_gate, b_gate, w_up, b_up, w_down, b_down, ln2_g, ln2_b):
    depth = w_in.shape[0]
    assert depth == 1, "single-layer step"
    bp, seq, _ = x_prompt.shape
    db, dseq, _ = x_sample.shape
    l = 0
    w = w_in[l]
    o_qa, o_ka, o_va, o_qm, o_om_end = 0, W_A, 2 * W_A, 3 * W_A, 3 * W_A + 4 * W_M
    o_if = o_om_end
    o_ga = o_if + 2 * H_M
    weights = {
        "wz": jnp.concatenate([w[:, o_ga:], w[:, o_qa:o_ka], w[:, o_qm:o_om_end]], axis=1).astype(BF16),
        "wkv": w[:, o_ka:o_qm].astype(BF16),
        "wif": jnp.pad(w[:, o_if:o_ga], ((0, 0), (0, LANES - 2 * H_M))),
        "bif": jnp.pad(b_if[l], (0, LANES - 2 * H_M))[None, :],
        "wa": w_branch_a[l].astype(BF16),
        "wm": w_branch_m[l].astype(BF16),
        "wo": w_out[l].astype(BF16),
        "g1": ln1_g[l][None, :], "b1": ln1_b[l][None, :],
        "wr": jnp.pad(w_router[l], ((0, 0), (0, LANES - N_EXPERTS))),
        "br": jnp.pad(b_router[l], (0, LANES - N_EXPERTS), constant_values=NEG_INF)[None, :],
        "wg": w_gate[l].astype(BF16), "bg": b_gate[l][:, None, :],
        "wu": w_up[l].astype(BF16), "bu": b_up[l][:, None, :],
        "wd": w_down[l].astype(BF16), "bd": b_down[l][:, None, :],
        "g2": ln2_g[l][None, :], "b2": ln2_b[l][None, :],
    }
    assert bp == 1, "prompt batch of one sequence"
    zero_state = (jnp.zeros((bp, H_M, DH_M, DH_M), F32), jnp.zeros((bp, H_M, DH_M), F32),
                  jnp.zeros((bp, H_M), F32))
    yp, kp, vp, cp, np_, mp = _group(
        x_prompt.reshape(bp * seq, D_MODEL), weights, ps=PAGE_SIZE, tm=256,
        moe_tile=min(MOE_TILE, bp * seq),
        attn=lambda z, kb, vt, kmean, kpg, vpg: _moba_prompt(z, kb, vt, kmean),
        mlstm_state=zero_state, lc=math.gcd(seq, MLSTM_CHUNK))
    ck = jnp.swapaxes(cache_k.reshape(cache_k.shape[1:]), -1, -2)
    cv = jnp.swapaxes(cache_v.reshape(cache_v.shape[1:]), -1, -2)
    ys, ks, vs, cs, ns, ms = _group(
        x_sample.reshape(db * dseq, D_MODEL), weights, ps=dseq, tm=256,
        moe_tile=min(MOE_TILE, db * dseq),
        attn=lambda z, kb, vt, kmean, kpg, vpg: _moba_sample(z, kpg, vpg, ck, cv, page_table),
        mlstm_state=(state_C[l], state_n[l], state_m[l]), lc=math.gcd(dseq, MLSTM_CHUNK))
    n_pg = seq // PAGE_SIZE
    return (yp.reshape(bp, seq, D_MODEL), ys.reshape(db, dseq, D_MODEL),
            kp.reshape(1, bp, n_pg, H_A, PAGE_SIZE, DH_A), vp.reshape(1, bp, n_pg, H_A, PAGE_SIZE, DH_A),
            ks.reshape(1, db, H_A, dseq, DH_A), vs.reshape(1, db, H_A, dseq, DH_A),
            cp[None], np_[None], mp[None], cs[None], ns[None], ms[None])
```

```python
import functools
import math

import jax
import jax.numpy as jnp
from jax import lax
from jax.experimental import pallas as pl
from jax.experimental.pallas import tpu as pltpu

F32 = jnp.float32
BF16 = jnp.bfloat16
HIGHEST = lax.Precision.HIGHEST

D_MODEL = 1024
PAGE_SIZE = 128
H_A = 8
DH_A = 64
W_A = H_A * DH_A
MOBA_BLOCK = 256
MOBA_TOPK = 3
H_M = 4
DH_M = 128
W_M = H_M * DH_M
MLSTM_CHUNK = 64
N_EXPERTS = 32
TOP_K = 4
SWIGLU_LIMIT = 7.0
SWIGLU_ALPHA = 1.702
DN_ALPHA = 2.0 ** 0.25
LN_EPS = 1e-5
NEG_INF = -1e30

LANES = 128
Z_GA, Z_GM, Z_QA, Z_QM, Z_KM, Z_VM, Z_OM = 0, 1024, 2048, 2560, 3072, 3584, 4096
Z_WIDTH = 4608
VMEM_LIMIT = 56 * 1024 * 1024


def _nt_dot(a, b, precision=None):
    return lax.dot_general(a, b, (((1,), (1,)), ((), ())), precision=precision,
                           preferred_element_type=F32)


def _dot(a, b, precision=None):
    return jnp.dot(a, b, precision=precision, preferred_element_type=F32)


def _top_mask(scores, idx_f, k, axis=-1):
    n = scores.shape[axis]
    sel = jnp.zeros(scores.shape, jnp.bool_)
    work = scores
    first_max = None
    for _ in range(k):
        mx = jnp.max(work, axis=axis, keepdims=True)
        if first_max is None:
            first_max = mx
        cand = work == mx
        pick_idx = jnp.min(jnp.where(cand, idx_f, float(n)), axis=axis, keepdims=True)
        pick = idx_f == pick_idx
        sel = jnp.logical_or(sel, pick)
        work = jnp.where(pick, -jnp.inf, work)
    return sel, first_max


def _proj_kernel(x_ref, wz_ref, wkv_ref, wif_ref, bif_ref,
                 z_ref, kb_ref, vt_ref, kp_ref, vp_ref, gif_ref, kmean_ref, *, ps):
    x = x_ref[...]
    tm = x.shape[0]
    xb = x.astype(BF16)
    gif_ref[...] = _dot(x, wif_ref[...], precision=HIGHEST) + bif_ref[...]
    for c in range(Z_WIDTH // 512):
        z_ref[:, c * 512:(c + 1) * 512] = _dot(xb, wz_ref[:, c * 512:(c + 1) * 512])
    for c, pref in ((0, kp_ref), (1, vp_ref)):
        r = _dot(xb, wkv_ref[:, c * W_A:(c + 1) * W_A])
        if c == 0:
            kb_ref[...] = r.astype(BF16)
        else:
            vt_ref[0] = r.T.astype(BF16)
        for h in range(H_A):
            pref[:, h, :, :] = r[:, h * DH_A:(h + 1) * DH_A].reshape(tm // ps, ps, DH_A)
        if c == 0:
            kmean_ref[...] = jnp.mean(r.reshape(tm // MOBA_BLOCK, MOBA_BLOCK, W_A), axis=1, keepdims=True)


def _project(x2d, wz, wkv, wif, bif, *, ps, tm):
    t = x2d.shape[0]
    assert tm == MOBA_BLOCK, "one MoBA block of keys per row tile"
    grid = (t // tm,)
    const = lambda i: (0, 0)
    return pl.pallas_call(
        functools.partial(_proj_kernel, ps=ps),
        grid=grid,
        in_specs=[
            pl.BlockSpec((tm, D_MODEL), lambda i: (i, 0)),
            pl.BlockSpec(wz.shape, const),
            pl.BlockSpec(wkv.shape, const),
            pl.BlockSpec(wif.shape, const),
            pl.BlockSpec(bif.shape, const),
        ],
        out_specs=[
            pl.BlockSpec((tm, Z_WIDTH), lambda i: (i, 0)),
            pl.BlockSpec((tm, W_A), lambda i: (i, 0)),
            pl.BlockSpec((1, W_A, tm), lambda i: (i, 0, 0)),
            pl.BlockSpec((tm // ps, H_A, ps, DH_A), lambda i: (i, 0, 0, 0)),
            pl.BlockSpec((tm // ps, H_A, ps, DH_A), lambda i: (i, 0, 0, 0)),
            pl.BlockSpec((tm, LANES), lambda i: (i, 0)),
            pl.BlockSpec((tm // MOBA_BLOCK, 1, W_A), lambda i: (i, 0, 0)),
        ],
        out_shape=[
            jax.ShapeDtypeStruct((t, Z_WIDTH), F32),
            jax.ShapeDtypeStruct((t, W_A), BF16),
            jax.ShapeDtypeStruct((t // tm, W_A, tm), BF16),
            jax.ShapeDtypeStruct((t // ps, H_A, ps, DH_A), F32),
            jax.ShapeDtypeStruct((t // ps, H_A, ps, DH_A), F32),
            jax.ShapeDtypeStruct((t, LANES), F32),
            jax.ShapeDtypeStruct((t // MOBA_BLOCK, 1, W_A), F32),
        ],
        compiler_params=pltpu.CompilerParams(
            dimension_semantics=("arbitrary",), vmem_limit_bytes=VMEM_LIMIT),
        name="proj",
    )(x2d, wz, wkv, wif, bif)


PAST_UNROLL = 8


LOG2E = 1.4426950408889634


def _split3(x):
    hi = x.astype(BF16)
    r1 = x - hi.astype(F32)
    mid = r1.astype(BF16)
    lo = (r1 - mid.astype(F32)).astype(BF16)
    return hi, mid, lo


def _moba_prompt_kernel(q_ref, k_ref, vt_ref, km_ref, o_ref, m_sc, acc_sc, sel_sc):
    hp = pl.program_id(0)
    j = pl.program_id(1)
    bq = MOBA_BLOCK
    nb = km_ref.shape[0]
    q = q_ref[...]
    lane = lax.broadcasted_iota(jnp.int32, (1, LANES), 1)
    blk_f = lax.broadcasted_iota(jnp.int32, (nb, 1), 0).astype(F32)
    jf = j.astype(F32)
    rk = lax.broadcasted_iota(jnp.int32, (bq, bq), 0)
    rq = lax.broadcasted_iota(jnp.int32, (bq, bq), 1)
    causal = rq >= rk
    pos = lax.broadcasted_iota(jnp.int32, (bq, 1), 0).astype(F32)
    k_extra = jnp.where(lane < 3, pos, jnp.where(lane < 6, 1.0, 0.0)).astype(BF16)
    vrow = lax.broadcasted_iota(jnp.int32, (LANES, 1), 0)
    ones_row = (DH_A, 0)
    vrow_b = lax.broadcasted_iota(jnp.int32, (LANES, bq), 0).astype(F32).astype(BF16)
    is_ones_row = [vrow_b == r for r in ones_row]
    k_own = jnp.concatenate([k_ref[pl.ds(pl.multiple_of(j * bq, bq), bq), :], k_extra], axis=1)
    vt_own = vt_ref[j]
    q_augs = []
    coefs = []
    for hh in range(2):
        hmask = jnp.logical_and(lane >= hh * DH_A, lane < (hh + 1) * DH_A)
        qh = jnp.where(hmask, q, 0.0)
        head_f = (2 * hp + hh + 1).astype(F32)
        coef = jnp.exp2(jnp.zeros((1, 1), F32) - head_f) * LOG2E
        coefs.append(coef)
        scores = _nt_dot(km_ref[...], qh, precision=HIGHEST)
        scores = jnp.where(blk_f < jf, scores, NEG_INF)
        sel, _ = _top_mask(scores, blk_f, min(MOBA_TOPK, nb), axis=0)
        sel = jnp.logical_and(sel, blk_f < jf)
        sel_sc[hh] = jnp.where(sel, 1.0, 0.0)
        c3 = _split3(jnp.broadcast_to(coef, (bq, 1)))
        q3 = _split3(-(coef * pos))
        q_extra = jnp.zeros((bq, LANES), F32)
        for li, term in enumerate(c3 + q3):
            q_extra = jnp.where(lane == li, term.astype(F32), q_extra)
        qb = (qh * (DH_A ** -0.5 * LOG2E)).astype(BF16)
        q_aug = jnp.concatenate([qb, q_extra.astype(BF16)], axis=1)
        q_augs.append(q_aug)
        logits = jnp.where(causal, _nt_dot(k_own, q_aug), NEG_INF)
        m = jnp.max(logits, axis=0, keepdims=True)
        p = jnp.exp2(logits - m)
        m_sc[hh] = m
        vt_h = jnp.where(is_ones_row[hh], jnp.ones((), BF16), vt_own)
        acc_sc[hh] = _dot(vt_h, p.astype(BF16))

    def body(it, carry):
        ns, kns, vtns, offs, valids = [], [], [], [], []
        for u in range(PAST_UNROLL):
            n_raw = it * PAST_UNROLL + u
            n = jnp.minimum(n_raw, j - 1)
            ns.append(n)
            valids.append(n_raw < j)
            kns.append(jnp.concatenate([k_ref[pl.ds(pl.multiple_of(n * bq, bq), bq), :], k_extra], axis=1))
            vtns.append(vt_ref[n])
            offs.append((j - n).astype(F32) * float(bq))
        for hh in range(2):
            logits = []
            for u in range(PAST_UNROLL):
                picked = sel_sc[hh, pl.ds(ns[u], 1), :]
                keep = jnp.logical_and(picked > 0.5, valids[u])
                rb = jnp.where(keep, -(coefs[hh] * offs[u]), NEG_INF)
                logits.append(_nt_dot(kns[u], q_augs[hh]) + rb)
            m_old = m_sc[hh]
            m_new = m_old
            for lg in logits:
                m_new = jnp.maximum(m_new, jnp.max(lg, axis=0, keepdims=True))
            acc_new = jnp.exp2(m_old - m_new) * acc_sc[hh]
            for u in range(PAST_UNROLL):
                p = jnp.exp2(logits[u] - m_new)
                vt_h = jnp.where(is_ones_row[hh], jnp.ones((), BF16), vtns[u])
                acc_new = acc_new + _dot(vt_h, p.astype(BF16))
            acc_sc[hh] = acc_new
            m_sc[hh] = m_new
        return carry

    lax.fori_loop(0, lax.div(j + (PAST_UNROLL - 1), PAST_UNROLL), body, 0)
    acc0 = acc_sc[0]
    acc1 = acc_sc[1]
    o0 = acc0 / acc0[ones_row[0]:ones_row[0] + 1, :]
    o1 = acc1 / acc1[ones_row[1]:ones_row[1] + 1, :]
    o_ref[...] = jnp.where(vrow < DH_A, o0, o1).T


def _moba_prompt(z, kb, vt, kmean):
    t = z.shape[0]
    nb = t // MOBA_BLOCK
    bq = MOBA_BLOCK
    qcol = Z_QA // LANES
    return pl.pallas_call(
        _moba_prompt_kernel,
        grid=(H_A // 2, nb),
        in_specs=[
            pl.BlockSpec((bq, LANES), lambda hp, j: (j, qcol + hp)),
            pl.BlockSpec((t, LANES), lambda hp, j: (0, hp)),
            pl.BlockSpec((nb, LANES, bq), lambda hp, j: (0, hp, 0)),
            pl.BlockSpec((nb, LANES), lambda hp, j: (0, hp)),
        ],
        out_specs=pl.BlockSpec((bq, LANES), lambda hp, j: (j, hp)),
        out_shape=jax.ShapeDtypeStruct((t, W_A), F32),
        scratch_shapes=[
            pltpu.VMEM((2, 1, bq), F32),
            pltpu.VMEM((2, LANES, bq), F32),
            pltpu.VMEM((2, nb, bq), F32),
        ],
        compiler_params=pltpu.CompilerParams(
            dimension_semantics=("arbitrary", "arbitrary"), vmem_limit_bytes=VMEM_LIMIT),
        name="moba_prompt",
    )(z, kb, vt, kmean)


def _moba_sample_kernel(pt_ref, q_ref, kn_ref, vn_ref, *refs, ppg, past):
    k_refs = refs[:ppg]
    v_refs = refs[ppg:2 * ppg]
    o_ref = refs[2 * ppg]
    qs_sc, qf_sc, ksum_sc, m_sc, l_sc, o_sc = refs[2 * ppg + 1:]
    g = pl.program_id(1)
    ng = pl.num_programs(1)
    t = q_ref.shape[0]
    rows = H_A * t
    npb = o_sc.shape[0]
    ppb = MOBA_BLOCK // PAGE_SIZE
    row_i = lax.broadcasted_iota(jnp.int32, (rows, 1), 0)
    slope = jnp.exp2(-(lax.div(row_i, t) + 1).astype(F32))
    tok = lax.rem(row_i, t)
    blk_lane = lax.broadcasted_iota(jnp.int32, (1, LANES), 1)
    feat_head = lax.div(lax.broadcasted_iota(jnp.int32, (1, W_A), 1), DH_A)

    @pl.when(g == 0)
    def _():
        q = q_ref[...]
        qbd = jnp.concatenate([jnp.where(feat_head == h, q, 0.0) for h in range(H_A)], axis=0)
        qf_sc[...] = qbd
        qs_sc[...] = qbd * (DH_A ** -0.5)
        ksum_sc[...] = jnp.zeros(ksum_sc.shape, F32)
        m_sc[...] = jnp.zeros(m_sc.shape, F32)
        l_sc[...] = jnp.zeros(l_sc.shape, F32)

    qs = qs_sc[...]
    rk = lax.broadcasted_iota(jnp.int32, (1, MOBA_BLOCK), 1)
    for b in range(ppg // ppb):
        n = g * (ppg // ppb) + b
        kts = [k_refs[b * ppb + p][...].reshape(W_A, PAGE_SIZE) for p in range(ppb)]
        vts = [v_refs[b * ppb + p][...].reshape(W_A, PAGE_SIZE) for p in range(ppb)]
        kt_sum = kts[0]
        for p in range(1, ppb):
            kt_sum = kt_sum + kts[p]
        ksum = jnp.sum(kt_sum, axis=-1, keepdims=True)
        ksum_sc[...] = jnp.where(blk_lane == n, ksum, ksum_sc[...])
        s = jnp.concatenate([_dot(qs, kt) for kt in kts], axis=-1)
        dist = ((past - n * MOBA_BLOCK) + tok - rk).astype(F32)
        logits = s - slope * dist
        m_b = jnp.max(logits, axis=-1, keepdims=True)
        p_ = jnp.exp(logits - m_b)
        l_b = jnp.sum(p_, axis=-1, keepdims=True)
        o_b = _nt_dot(p_[:, :PAGE_SIZE], vts[0])
        for p in range(1, ppb):
            o_b = o_b + _nt_dot(p_[:, p * PAGE_SIZE:(p + 1) * PAGE_SIZE], vts[p])
        o_sc[n] = o_b
        m_sc[...] = jnp.where(blk_lane == n, m_b, m_sc[...])
        l_sc[...] = jnp.where(blk_lane == n, l_b, l_sc[...])

    @pl.when(g == ng - 1)
    def _():
        kmean = ksum_sc[...] * (1.0 / MOBA_BLOCK)
        scores = _dot(qf_sc[...], kmean, precision=HIGHEST)
        scores = jnp.where(blk_lane < npb, scores, -jnp.inf)
        sel, _ = _top_mask(scores, blk_lane.astype(F32), min(MOBA_TOPK, npb))
        kn = jnp.concatenate([kn_ref[h] for h in range(H_A)], axis=1)
        vn = jnp.concatenate([vn_ref[h] for h in range(H_A)], axis=1)
        s_own = _nt_dot(qs, kn)
        d_own = (tok - lax.broadcasted_iota(jnp.int32, (1, t), 1)).astype(F32)
        lg_own = jnp.where(d_own >= 0, s_own - slope * d_own, NEG_INF)
        m_all = m_sc[...]
        m_sel = jnp.max(jnp.where(sel, m_all, -jnp.inf), axis=-1, keepdims=True)
        m_fin = jnp.maximum(m_sel, jnp.max(lg_own, axis=-1, keepdims=True))
        w = jnp.where(sel, jnp.exp(m_all - m_fin), 0.0)
        p_own = jnp.exp(lg_own - m_fin)
        l_fin = jnp.sum(w * l_sc[...], axis=-1, keepdims=True) + jnp.sum(p_own, axis=-1, keepdims=True)
        o = _dot(p_own, vn)
        for n in range(npb):
            o = o + w[:, n:n + 1] * o_sc[n]
        o = o / l_fin
        out = jnp.zeros((t, W_A), F32)
        for h in range(H_A):
            out = out + jnp.where(feat_head == h, o[h * t:(h + 1) * t, :], 0.0)
        o_ref[...] = out


def _moba_sample(z, k_new, v_new, cache_k, cache_v, page_table):
    db, n_pages = page_table.shape
    t = z.shape[0] // db
    ppb = MOBA_BLOCK // PAGE_SIZE
    npb = n_pages // ppb
    assert n_pages == npb * ppb, "cached pages must fill whole MoBA blocks"
    ppg = 8 if n_pages % 8 == 0 else ppb
    ng = n_pages // ppg

    def page_map(b, g, pt, *, i):
        return (pt[b * n_pages + g * ppg + i], 0, 0, 0)

    page_specs = [pl.BlockSpec((None, H_A, DH_A, PAGE_SIZE), functools.partial(page_map, i=i))
                  for i in range(ppg)]
    grid_spec = pltpu.PrefetchScalarGridSpec(
        num_scalar_prefetch=1,
        grid=(db, ng),
        in_specs=[
            pl.BlockSpec((t, W_A), lambda b, g, pt: (b, Z_QA // W_A)),
            pl.BlockSpec((None, H_A, t, DH_A), lambda b, g, pt: (b, 0, 0, 0)),
            pl.BlockSpec((None, H_A, t, DH_A), lambda b, g, pt: (b, 0, 0, 0)),
        ] + page_specs + page_specs,
        out_specs=pl.BlockSpec((t, W_A), lambda b, g, pt: (b, 0)),
        scratch_shapes=[
            pltpu.VMEM((H_A * t, W_A), F32),
            pltpu.VMEM((H_A * t, W_A), F32),
            pltpu.VMEM((W_A, LANES), F32),
            pltpu.VMEM((H_A * t, LANES), F32),
            pltpu.VMEM((H_A * t, LANES), F32),
            pltpu.VMEM((npb, H_A * t, W_A), F32),
        ],
    )
    return pl.pallas_call(
        functools.partial(_moba_sample_kernel, ppg=ppg, past=n_pages * PAGE_SIZE),
        grid_spec=grid_spec,
        out_shape=jax.ShapeDtypeStruct((db * t, W_A), F32),
        compiler_params=pltpu.CompilerParams(
            dimension_semantics=("arbitrary", "arbitrary"), vmem_limit_bytes=VMEM_LIMIT),
        name="moba_sample",
    )(page_table.reshape(-1), z, k_new, v_new, *([cache_k] * ppg), *([cache_v] * ppg))


def _mlstm_kernel(q_ref, k_ref, v_ref, g_ref, c0_ref, n0_ref, m0_ref,
                  h_ref, c_ref, n_ref, m_ref, c_sc, n_sc, m_sc, *, lc, cps):
    step = pl.program_id(1)
    nsteps = pl.num_programs(1)
    lp = max(lc, MLSTM_CHUNK)

    @pl.when(step == 0)
    def _():
        c_sc[...] = c0_ref[0]
        n_sc[...] = n0_ref[0]
        m_sc[...] = m0_ref[0]

    def padded(a):
        if a.shape[0] == lp:
            return a
        return jnp.concatenate([a, jnp.zeros((lp - a.shape[0],) + a.shape[1:], a.dtype)], axis=0)

    row = lax.broadcasted_iota(jnp.int32, (lp, lp), 0)
    col = lax.broadcasted_iota(jnp.int32, (lp, lp), 1)
    causal = row >= col
    tril = jnp.where(causal, 1.0, 0.0).astype(BF16)
    pick_8 = jnp.where(lax.broadcasted_iota(jnp.int32, (16, LANES), 0)
                       == lax.broadcasted_iota(jnp.int32, (16, LANES), 1), 1.0, 0.0).astype(BF16)
    gate_lane = lax.broadcasted_iota(jnp.int32, (1, LANES), 1)
    eye = jnp.where(lax.broadcasted_iota(jnp.int32, (DH_M, DH_M), 0)
                    == lax.broadcasted_iota(jnp.int32, (DH_M, DH_M), 1), 1.0, 0.0).astype(BF16)
    rvalid = lax.broadcasted_iota(jnp.int32, (lp, 1), 0) < lc
    m_all = m_sc[...]
    n_all = n_sc[...]
    c_all = [c_sc[h] for h in range(H_M)]
    for ci in range(cps):
        rs = slice(ci * lc, (ci + 1) * lc)
        q = padded(q_ref[rs, :])
        k = padded(k_ref[rs, :]) * (DH_M ** -0.5)
        v = padded(v_ref[rs, :])
        gts = padded(g_ref[rs, :])
        lf = jax.nn.log_sigmoid(gts)
        bcum = sum(_dot(tril, part) for part in _split3(lf))
        comb_t = sum(_nt_dot(pick_8, part) for part in _split3(jnp.where(gate_lane < H_M, gts, bcum)))
        m_out, n_out, c_out = [], [], []
        for h in range(H_M):
            qh = q[:, h * DH_M:(h + 1) * DH_M]
            kh = k[:, h * DH_M:(h + 1) * DH_M]
            vh = v[:, h * DH_M:(h + 1) * DH_M]
            bc = bcum[:, H_M + h:H_M + h + 1]
            ic = gts[:, h:h + 1]
            br = comb_t[H_M + h:H_M + h + 1, :]
            ir = comb_t[h:h + 1, :]
            m_prev = m_all[h:h + 1, :]
            c_prev = c_all[h]
            n_prev = n_all[h:h + 1, :]
            dmat = jnp.where(causal, bc - br + ir, -jnp.inf)
            a_inter = bc + m_prev
            m_t = jnp.maximum(a_inter, jnp.max(dmat, axis=-1, keepdims=True))
            w_inter = jnp.exp(a_inter - m_t)
            qhb = qh.astype(BF16)
            s = _nt_dot(qhb, kh.astype(BF16)) * jnp.exp(dmat - m_t)
            num = w_inter * _dot(qhb, c_prev.astype(BF16)) + _dot(s.astype(BF16), vh.astype(BF16))
            den = (w_inter * jnp.sum(qh * n_prev, axis=-1, keepdims=True)
                   + jnp.sum(s, axis=-1, keepdims=True))
            h_out = num / jnp.maximum(jnp.abs(den), jnp.exp(-m_t))
            h_ref[rs, h * DH_M:(h + 1) * DH_M] = h_out[:lc]
            m_new = m_t[lc - 1:lc, :]
            b_last = bc[lc - 1:lc, :]
            g_c = jnp.exp(b_last + m_prev - m_new)
            g_s = jnp.where(rvalid, jnp.exp(b_last - bc + ic - m_new), 0.0)
            kg = g_s * kh
            kg_t = _nt_dot(eye, kg.astype(BF16)).astype(BF16)
            c_out.append(g_c * c_prev + _dot(kg_t, vh.astype(BF16)))
            n_out.append(g_c * n_prev + jnp.sum(kg, axis=0, keepdims=True))
            m_out.append(m_new)
        m_all = jnp.concatenate(m_out, axis=0)
        n_all = jnp.concatenate(n_out, axis=0)
        c_all = c_out
    for h in range(H_M):
        c_sc[h] = c_all[h]
    n_sc[...] = n_all
    m_sc[...] = m_all

    @pl.when(step == nsteps - 1)
    def _():
        c_ref[0] = c_sc[...]
        n_ref[0] = n_sc[...]
        m_ref[0] = m_sc[...]


def _mlstm(z, gif, c0, n0, m0, *, lc, cps):
    b = c0.shape[0]
    t = z.shape[0] // b
    rows = lc * cps
    nc = t // rows
    assert nc * rows == t
    row = lambda bi, ci: bi * nc + ci
    wcol = lambda off: off // W_M
    return pl.pallas_call(
        functools.partial(_mlstm_kernel, lc=lc, cps=cps),
        grid=(b, nc),
        in_specs=[
            pl.BlockSpec((rows, W_M), lambda bi, ci: (row(bi, ci), wcol(Z_QM))),
            pl.BlockSpec((rows, W_M), lambda bi, ci: (row(bi, ci), wcol(Z_KM))),
            pl.BlockSpec((rows, W_M), lambda bi, ci: (row(bi, ci), wcol(Z_VM))),
            pl.BlockSpec((rows, LANES), lambda bi, ci: (row(bi, ci), 0)),
            pl.BlockSpec((1, H_M, DH_M, DH_M), lambda bi, ci: (bi, 0, 0, 0)),
            pl.BlockSpec((1, H_M, DH_M), lambda bi, ci: (bi, 0, 0)),
            pl.BlockSpec((1, H_M, 1), lambda bi, ci: (bi, 0, 0)),
        ],
        out_specs=[
            pl.BlockSpec((rows, W_M), lambda bi, ci: (row(bi, ci), 0)),
            pl.BlockSpec((1, H_M, DH_M, DH_M), lambda bi, ci: (bi, 0, 0, 0)),
            pl.BlockSpec((1, H_M, DH_M), lambda bi, ci: (bi, 0, 0)),
            pl.BlockSpec((1, H_M, 1), lambda bi, ci: (bi, 0, 0)),
        ],
        out_shape=[
            jax.ShapeDtypeStruct((b * t, W_M), F32),
            jax.ShapeDtypeStruct((b, H_M, DH_M, DH_M), F32),
            jax.ShapeDtypeStruct((b, H_M, DH_M), F32),
            jax.ShapeDtypeStruct((b, H_M, 1), F32),
        ],
        scratch_shapes=[
            pltpu.VMEM((H_M, DH_M, DH_M), F32),
            pltpu.VMEM((H_M, DH_M), F32),
            pltpu.VMEM((H_M, 1), F32),
        ],
        compiler_params=pltpu.CompilerParams(
            dimension_semantics=("arbitrary", "arbitrary"), vmem_limit_bytes=VMEM_LIMIT),
        name="mlstm",
    )(z, z, z, gif, c0, n0, m0[..., None])


def _layernorm(y, g, b):
    mu = jnp.mean(y, axis=-1, keepdims=True)
    var = jnp.mean(jnp.square(y - mu), axis=-1, keepdims=True)
    return (y - mu) * lax.rsqrt(var + LN_EPS) * g + b


def _finish_kernel(x_ref, a_ref, mh_ref, om_ref, ga_ref, gm_ref, wa_ref, wm_ref, wo_ref,
                   g1_ref, b1_ref, wr_ref, br_ref,
                   h_ref, hb_ref, gates_ref, gates_t_ref, cnt_ref):
    a = _dot(a_ref[...].astype(BF16), wa_ref[...])
    mo = mh_ref[...] * jax.nn.sigmoid(om_ref[...])
    m = _dot(mo.astype(BF16), wm_ref[...])
    merged = jax.nn.sigmoid(ga_ref[...]) * a + jax.nn.sigmoid(gm_ref[...]) * m
    y = DN_ALPHA * x_ref[...] + _dot(merged.astype(BF16), wo_ref[...])
    h = _layernorm(y, g1_ref[...], b1_ref[...])
    h_ref[...] = h
    hb_ref[...] = h.astype(BF16)
    logits = _dot(h, wr_ref[...], precision=HIGHEST) + br_ref[...]
    lane_f = lax.broadcasted_iota(jnp.int32, (1, LANES), 1).astype(F32)
    sel, top = _top_mask(logits, lane_f, TOP_K)
    e = jnp.where(sel, jnp.exp(logits - top), 0.0)
    gates = e / jnp.sum(e, axis=-1, keepdims=True)
    gates_ref[...] = gates
    gates_t_ref[...] = gates.T
    cnt_ref[0] = jnp.sum(jnp.where(gates > 0.0, 1.0, 0.0), axis=0, keepdims=True).astype(jnp.int32)


def _finish(x2d, a, mh, z, wa, wm, wo, g1, b1, wr, br, *, tm):
    t = x2d.shape[0]
    nt = t // tm
    const = lambda i: (0, 0)
    return pl.pallas_call(
        _finish_kernel,
        grid=(nt,),
        in_specs=[
            pl.BlockSpec((tm, D_MODEL), lambda i: (i, 0)),
            pl.BlockSpec((tm, W_A), lambda i: (i, 0)),
            pl.BlockSpec((tm, W_M), lambda i: (i, 0)),
            pl.BlockSpec((tm, W_M), lambda i: (i, Z_OM // W_M)),
            pl.BlockSpec((tm, D_MODEL), lambda i: (i, Z_GA // D_MODEL)),
            pl.BlockSpec((tm, D_MODEL), lambda i: (i, Z_GM // D_MODEL)),
            pl.BlockSpec(wa.shape, const),
            pl.BlockSpec(wm.shape, const),
            pl.BlockSpec(wo.shape, const),
            pl.BlockSpec(g1.shape, const),
            pl.BlockSpec(b1.shape, const),
            pl.BlockSpec(wr.shape, const),
            pl.BlockSpec(br.shape, const),
        ],
        out_specs=[
            pl.BlockSpec((tm, D_MODEL), lambda i: (i, 0)),
            pl.BlockSpec((tm, D_MODEL), lambda i: (i, 0)),
            pl.BlockSpec((tm, LANES), lambda i: (i, 0)),
            pl.BlockSpec((LANES, tm), lambda i: (0, i)),
            pl.BlockSpec((1, 1, LANES), lambda i: (i, 0, 0)),
        ],
        out_shape=[
            jax.ShapeDtypeStruct((t, D_MODEL), F32),
            jax.ShapeDtypeStruct((t, D_MODEL), BF16),
            jax.ShapeDtypeStruct((t, LANES), F32),
            jax.ShapeDtypeStruct((LANES, t), F32),
            jax.ShapeDtypeStruct((nt, 1, LANES), jnp.int32),
        ],
        compiler_params=pltpu.CompilerParams(
            dimension_semantics=("arbitrary",), vmem_limit_bytes=VMEM_LIMIT),
        name="finish",
    )(x2d, a, mh, z, z, z, wa, wm, wo, g1, b1, wr, br)


MLSTM_CHUNKS_PER_STEP = 4
MOE_TILE = 1024
MOE_SEG = 512
MOE_SEG_ROWS = 80
MOE_RANK_BLOCK = 256


def _moe_kernel(cnt_ref, hb_ref, g_ref, gt_ref, wg_ref, bg_ref, wu_ref, bu_ref, wd_ref, bd_ref,
                y_ref, posc_sc, selc_sc, posr_sc, *, cnt_per_seg, seg):
    i = pl.program_id(0)
    e = pl.program_id(1)
    tt = hb_ref.shape[0]
    nseg = tt // seg
    rseg = MOE_SEG_ROWS
    blk = MOE_RANK_BLOCK

    @pl.when(e == 0)
    def _():
        y_ref[...] = jnp.zeros(y_ref.shape, F32)

    cnt_max = 0
    for sg in range(nseg):
        first = (i * nseg + sg) * cnt_per_seg
        cnt_seg = cnt_ref[first, e]
        for r in range(1, cnt_per_seg):
            cnt_seg = cnt_seg + cnt_ref[first + r, e]
        cnt_max = jnp.maximum(cnt_max, cnt_seg)

    @pl.when(cnt_max > 0)
    def _():
        lane = lax.broadcasted_iota(jnp.int32, (1, LANES), 1)
        gcol = jnp.sum(jnp.where(lane == e, g_ref[...], 0.0), axis=-1, keepdims=True)
        selc = jnp.broadcast_to(jnp.where(gcol > 0.0, 1.0, 0.0), (tt, LANES))
        selc_sc[...] = selc
        rr = lax.broadcasted_iota(jnp.int32, (blk, blk), 0)
        cc = lax.broadcasted_iota(jnp.int32, (blk, blk), 1)
        strict_lower = jnp.where(rr > cc, 1.0, 0.0).astype(BF16)
        for bi in range(tt // blk):
            if (bi * blk) % seg == 0:
                offs = jnp.zeros((1, LANES), F32)
            sb = selc[bi * blk:(bi + 1) * blk]
            posc_sc[bi * blk:(bi + 1) * blk, :] = _dot(strict_lower, sb.astype(BF16)) + offs
            offs = offs + jnp.sum(sb, axis=0, keepdims=True)
        grow = gt_ref[pl.ds(lax.rem(e, 8), 1), :]
        selr = jnp.where(grow > 0.0, 1.0, 0.0)
        ru = lax.broadcasted_iota(jnp.int32, (LANES, LANES), 0)
        cu = lax.broadcasted_iota(jnp.int32, (LANES, LANES), 1)
        strict_upper = jnp.where(ru < cu, 1.0, 0.0).astype(BF16)
        for ci in range(tt // LANES):
            if (ci * LANES) % seg == 0:
                offr = jnp.zeros((16, 1), F32)
            sr = jnp.broadcast_to(selr[:, ci * LANES:(ci + 1) * LANES], (16, LANES))
            posr_sc[:, ci * LANES:(ci + 1) * LANES] = _dot(sr.astype(BF16), strict_upper) + offr
            offr = offr + jnp.sum(sr, axis=-1, keepdims=True)
        slot_sub = lax.broadcasted_iota(jnp.int32, (rseg, LANES), 0).astype(F32)
        slot_lane = lax.broadcasted_iota(jnp.int32, (1, LANES), 1).astype(F32)

        def one_pass(s, carry):
            base = (s * rseg).astype(F32)
            xs_parts, gs_parts = [], []
            for sg in range(nseg):
                pm_chunks = []
                for ci in range(sg * seg // LANES, (sg + 1) * seg // LANES):
                    pr = posr_sc[0:1, ci * LANES:(ci + 1) * LANES] - base
                    sr = selr[:, ci * LANES:(ci + 1) * LANES]
                    pm_chunks.append(jnp.logical_and(pr == slot_sub, sr > 0.0))
                pmask = jnp.concatenate(pm_chunks, axis=1)
                gs_parts.append(jnp.sum(jnp.where(pmask, grow[:, sg * seg:(sg + 1) * seg], 0.0),
                                        axis=-1, keepdims=True))
                xs_parts.append(_dot(jnp.where(pmask, 1.0, 0.0).astype(BF16),
                                     hb_ref[sg * seg:(sg + 1) * seg, :]).astype(BF16))
            xs = jnp.concatenate(xs_parts, axis=0)
            gs = jnp.concatenate(gs_parts, axis=0)
            gp = jnp.minimum(_dot(xs, wg_ref[0]) + bg_ref[0], SWIGLU_LIMIT)
            up = jnp.clip(_dot(xs, wu_ref[0]) + bu_ref[0], -SWIGLU_LIMIT, SWIGLU_LIMIT)
            hdn = (up + 1.0) * (gp * jax.nn.sigmoid(SWIGLU_ALPHA * gp))
            yb = ((_dot(hdn.astype(BF16), wd_ref[0]) + bd_ref[0]) * gs).astype(BF16)
            pad = jnp.zeros((LANES - rseg, yb.shape[1]), BF16)
            for sg in range(nseg):
                rows = slice(sg * seg, (sg + 1) * seg)
                hit = jnp.logical_and(posc_sc[rows, :] - base == slot_lane, slot_lane < float(rseg))
                pt = jnp.where(jnp.logical_and(hit, selc_sc[rows, :] > 0.0), 1.0, 0.0).astype(BF16)
                y_seg = jnp.concatenate([yb[sg * rseg:(sg + 1) * rseg], pad], axis=0)
                y_ref[rows, :] += _dot(pt, y_seg)
            return carry

        lax.fori_loop(0, lax.div(cnt_max + (rseg - 1), rseg), one_pass, 0)


def _moe(hb, gates, gates_t, cnt, wg, bg, wu, bu, wd, bd, *, tt):
    t = hb.shape[0]
    nt = t // tt
    seg = min(MOE_SEG, tt)
    cnt_per_seg = cnt.shape[0] * seg // t
    assert cnt_per_seg * t == cnt.shape[0] * seg and tt % seg == 0 and seg % MOE_RANK_BLOCK == 0
    wspec = pl.BlockSpec((1, D_MODEL, D_MODEL), lambda i, e, c: (e, 0, 0))
    bspec = pl.BlockSpec((1, 1, D_MODEL), lambda i, e, c: (e, 0, 0))
    grid_spec = pltpu.PrefetchScalarGridSpec(
        num_scalar_prefetch=1,
        grid=(nt, N_EXPERTS),
        in_specs=[
            pl.BlockSpec((tt, D_MODEL), lambda i, e, c: (i, 0)),
            pl.BlockSpec((tt, LANES), lambda i, e, c: (i, 0)),
            pl.BlockSpec((8, tt), lambda i, e, c: (e // 8, i)),
            wspec, bspec, wspec, bspec, wspec, bspec,
        ],
        out_specs=pl.BlockSpec((tt, D_MODEL), lambda i, e, c: (i, 0)),
        scratch_shapes=[
            pltpu.VMEM((tt, LANES), F32),
            pltpu.VMEM((tt, LANES), F32),
            pltpu.VMEM((16, tt), F32),
        ],
    )
    return pl.pallas_call(
        functools.partial(_moe_kernel, cnt_per_seg=cnt_per_seg, seg=seg),
        grid_spec=grid_spec,
        out_shape=jax.ShapeDtypeStruct((t, D_MODEL), F32),
        compiler_params=pltpu.CompilerParams(
            dimension_semantics=("arbitrary", "arbitrary"), vmem_limit_bytes=VMEM_LIMIT),
        name="moe",
    )(cnt, hb, gates, gates_t, wg, bg, wu, bu, wd, bd)


def _ln2_kernel(h_ref, y_ref, g_ref, b_ref, o_ref):
    o_ref[...] = _layernorm(DN_ALPHA * h_ref[...] + y_ref[...], g_ref[...], b_ref[...])


def _ln2(h, y, g, b, *, tm):
    t = h.shape[0]
    spec = pl.BlockSpec((tm, D_MODEL), lambda i: (i, 0))
    const = pl.BlockSpec((1, D_MODEL), lambda i: (0, 0))
    return pl.pallas_call(
        _ln2_kernel,
        grid=(t // tm,),
        in_specs=[spec, spec, const, const],
        out_specs=spec,
        out_shape=jax.ShapeDtypeStruct((t, D_MODEL), F32),
        compiler_params=pltpu.CompilerParams(dimension_semantics=("arbitrary",)),
        name="ln2",
    )(h, y, g, b)


def _group(x2d, weights, *, ps, tm, moe_tile, attn, mlstm_state, lc):
    z, kb, vt, kpages, vpages, gif, kmean = _project(
        x2d, weights["wz"], weights["wkv"], weights["wif"], weights["bif"], ps=ps, tm=tm)
    a = attn(z, kb, vt, kmean.reshape(kmean.shape[0], W_A), kpages, vpages)
    t_seq = x2d.shape[0] // mlstm_state[0].shape[0]
    cps = MLSTM_CHUNKS_PER_STEP if (t_seq // lc) % MLSTM_CHUNKS_PER_STEP == 0 else 1
    mh, c_t, n_t, m_t = _mlstm(z, gif, *mlstm_state, lc=lc, cps=cps)
    h, hb, gates, gates_t, cnt = _finish(
        x2d, a, mh, z, weights["wa"], weights["wm"], weights["wo"], weights["g1"], weights["b1"],
        weights["wr"], weights["br"], tm=tm)
    y = _moe(hb, gates, gates_t, cnt.reshape(cnt.shape[0], LANES),
             weights["wg"], weights["bg"], weights["wu"], weights["bu"], weights["wd"], weights["bd"],
             tt=moe_tile)
    out = _ln2(h, y, weights["g2"], weights["b2"], tm=tm)
    return out, kpages, vpages, c_t, n_t, m_t[..., 0]


def kernel(x_prompt, x_sample, cache_k, cache_v, page_table, state_C, state_n, state_m,
           w_in, b_if, w_branch_a, w_branch_m, w_out, ln1_g, ln1_b,
           w_router, b_router, w_gate, b_gate, w_up, b_up, w_down, b_down, ln2_g, ln2_b):
    depth = w_in.shape[0]
    assert depth == 1, "single-layer step"
    bp, seq, _ = x_prompt.shape
    db, dseq, _ = x_sample.shape
    l = 0
    w = w_in[l]
    o_qa, o_ka, o_va, o_qm, o_om_end = 0, W_A, 2 * W_A, 3 * W_A, 3 * W_A + 4 * W_M
    o_if = o_om_end
    o_ga = o_if + 2 * H_M
    weights = {
        "wz": jnp.concatenate([w[:, o_ga:], w[:, o_qa:o_ka], w[:, o_qm:o_om_end]], axis=1).astype(BF16),
        "wkv": w[:, o_ka:o_qm].astype(BF16),
        "wif": jnp.pad(w[:, o_if:o_ga], ((0, 0), (0, LANES - 2 * H_M))),
        "bif": jnp.pad(b_if[l], (0, LANES - 2 * H_M))[None, :],
        "wa": w_branch_a[l].astype(BF16),
        "wm": w_branch_m[l].astype(BF16),
        "wo": w_out[l].astype(BF16),
        "g1": ln1_g[l][None, :], "b1": ln1_b[l][None, :],
        "wr": jnp.pad(w_router[l], ((0, 0), (0, LANES - N_EXPERTS))),
        "br": jnp.pad(b_router[l], (0, LANES - N_EXPERTS), constant_values=NEG_INF)[None, :],
        "wg": w_gate[l].astype(BF16), "bg": b_gate[l][:, None, :],
        "wu": w_up[l].astype(BF16), "bu": b_up[l][:, None, :],
        "wd": w_down[l].astype(BF16), "bd": b_down[l][:, None, :],
        "g2": ln2_g[l][None, :], "b2": ln2_b[l][None, :],
    }
    assert bp == 1, "prompt batch of one sequence"
    zero_state = (jnp.zeros((bp, H_M, DH_M, DH_M), F32), jnp.zeros((bp, H_M, DH_M), F32),
                  jnp.zeros((bp, H_M), F32))
    yp, kp, vp, cp, np_, mp = _group(
        x_prompt.reshape(bp * seq, D_MODEL), weights, ps=PAGE_SIZE, tm=256,
        moe_tile=min(MOE_TILE, bp * seq),
        attn=lambda z, kb, vt, kmean, kpg, vpg: _moba_prompt(z, kb, vt, kmean),
        mlstm_state=zero_state, lc=math.gcd(seq, MLSTM_CHUNK))
    ck = jnp.swapaxes(cache_k.reshape(cache_k.shape[1:]), -1, -2)
    cv = jnp.swapaxes(cache_v.reshape(cache_v.shape[1:]), -1, -2)
    ys, ks, vs, cs, ns, ms = _group(
        x_sample.reshape(db * dseq, D_MODEL), weights, ps=dseq, tm=256,
        moe_tile=min(MOE_TILE, db * dseq),
        attn=lambda z, kb, vt, kmean, kpg, vpg: _moba_sample(z, kpg, vpg, ck, cv, page_table),
        mlstm_state=(state_C[l], state_n[l], state_m[l]), lc=math.gcd(dseq, MLSTM_CHUNK))
    n_pg = seq // PAGE_SIZE
    return (yp.reshape(bp, seq, D_MODEL), ys.reshape(db, dseq, D_MODEL),
            kp.reshape(1, bp, n_pg, H_A, PAGE_SIZE, DH_A), vp.reshape(1, bp, n_pg, H_A, PAGE_SIZE, DH_A),
            ks.reshape(1, db, H_A, dseq, DH_A), vs.reshape(1, db, H_A, dseq, DH_A),
            cp[None], np_[None], mp[None], cs[None], ns[None], ms[None])
```

```python
import functools
import math

import jax
import jax.numpy as jnp
from jax import lax
from jax.experimental import pallas as pl
from jax.experimental.pallas import tpu as pltpu

F32 = jnp.float32
BF16 = jnp.bfloat16
HIGHEST = lax.Precision.HIGHEST

D_MODEL = 1024
PAGE_SIZE = 128
H_A = 8
DH_A = 64
W_A = H_A * DH_A
MOBA_BLOCK = 256
MOBA_TOPK = 3
H_M = 4
DH_M = 128
W_M = H_M * DH_M
MLSTM_CHUNK = 64
N_EXPERTS = 32
TOP_K = 4
SWIGLU_LIMIT = 7.0
SWIGLU_ALPHA = 1.702
DN_ALPHA = 2.0 ** 0.25
LN_EPS = 1e-5
NEG_INF = -1e30

LANES = 128
Z_GA, Z_GM, Z_QA, Z_QM, Z_KM, Z_VM, Z_OM = 0, 1024, 2048, 2560, 3072, 3584, 4096
Z_WIDTH = 4608
VMEM_LIMIT = 56 * 1024 * 1024


def _nt_dot(a, b, precision=None):
    return lax.dot_general(a, b, (((1,), (1,)), ((), ())), precision=precision,
                           preferred_element_type=F32)


def _dot(a, b, precision=None):
    return jnp.dot(a, b, precision=precision, preferred_element_type=F32)


def _top_mask(scores, idx_f, k, axis=-1):
    n = scores.shape[axis]
    sel = jnp.zeros(scores.shape, jnp.bool_)
    work = scores
    first_max = None
    for _ in range(k):
        mx = jnp.max(work, axis=axis, keepdims=True)
        if first_max is None:
            first_max = mx
        cand = work == mx
        pick_idx = jnp.min(jnp.where(cand, idx_f, float(n)), axis=axis, keepdims=True)
        pick = idx_f == pick_idx
        sel = jnp.logical_or(sel, pick)
        work = jnp.where(pick, -jnp.inf, work)
    return sel, first_max


def _proj_kernel(x_ref, wz_ref, wkv_ref, wif_ref, bif_ref,
                 z_ref, kb_ref, vt_ref, kp_ref, vp_ref, gif_ref, kmean_ref, *, ps):
    x = x_ref[...]
    tm = x.shape[0]
    xb = x.astype(BF16)
    gif_ref[...] = _dot(x, wif_ref[...], precision=HIGHEST) + bif_ref[...]
    for c in range(Z_WIDTH // 512):
        z_ref[:, c * 512:(c + 1) * 512] = _dot(xb, wz_ref[:, c * 512:(c + 1) * 512])
    for c, pref in ((0, kp_ref), (1, vp_ref)):
        r = _dot(xb, wkv_ref[:, c * W_A:(c + 1) * W_A])
        if c == 0:
            kb_ref[...] = r.astype(BF16)
        else:
            vt_ref[0] = r.T.astype(BF16)
        for h in range(H_A):
            pref[:, h, :, :] = r[:, h * DH_A:(h + 1) * DH_A].reshape(tm // ps, ps, DH_A)
        if c == 0:
            kmean_ref[...] = jnp.mean(r.reshape(tm // MOBA_BLOCK, MOBA_BLOCK, W_A), axis=1, keepdims=True)


def _project(x2d, wz, wkv, wif, bif, *, ps, tm):
    t = x2d.shape[0]
    assert tm == MOBA_BLOCK, "one MoBA block of keys per row tile"
    grid = (t // tm,)
    const = lambda i: (0, 0)
    return pl.pallas_call(
        functools.partial(_proj_kernel, ps=ps),
        grid=grid,
        in_specs=[
            pl.BlockSpec((tm, D_MODEL), lambda i: (i, 0)),
            pl.BlockSpec(wz.shape, const),
            pl.BlockSpec(wkv.shape, const),
            pl.BlockSpec(wif.shape, const),
            pl.BlockSpec(bif.shape, const),
        ],
        out_specs=[
            pl.BlockSpec((tm, Z_WIDTH), lambda i: (i, 0)),
            pl.BlockSpec((tm, W_A), lambda i: (i, 0)),
            pl.BlockSpec((1, W_A, tm), lambda i: (i, 0, 0)),
            pl.BlockSpec((tm // ps, H_A, ps, DH_A), lambda i: (i, 0, 0, 0)),
            pl.BlockSpec((tm // ps, H_A, ps, DH_A), lambda i: (i, 0, 0, 0)),
            pl.BlockSpec((tm, LANES), lambda i: (i, 0)),
            pl.BlockSpec((tm // MOBA_BLOCK, 1, W_A), lambda i: (i, 0, 0)),
        ],
        out_shape=[
            jax.ShapeDtypeStruct((t, Z_WIDTH), F32),
            jax.ShapeDtypeStruct((t, W_A), BF16),
            jax.ShapeDtypeStruct((t // tm, W_A, tm), BF16),
            jax.ShapeDtypeStruct((t // ps, H_A, ps, DH_A), F32),
            jax.ShapeDtypeStruct((t // ps, H_A, ps, DH_A), F32),
            jax.ShapeDtypeStruct((t, LANES), F32),
            jax.ShapeDtypeStruct((t // MOBA_BLOCK, 1, W_A), F32),
        ],
        compiler_params=pltpu.CompilerParams(
            dimension_semantics=("arbitrary",), vmem_limit_bytes=VMEM_LIMIT),
        name="proj",
    )(x2d, wz, wkv, wif, bif)


PAST_UNROLL = 8


LOG2E = 1.4426950408889634


def _split3(x):
    hi = x.astype(BF16)
    r1 = x - hi.astype(F32)
    mid = r1.astype(BF16)
    lo = (r1 - mid.astype(F32)).astype(BF16)
    return hi, mid, lo


def _moba_prompt_kernel(q_ref, k_ref, vt_ref, km_ref, o_ref, m_sc, acc_sc, sel_sc):
    hp = pl.program_id(0)
    j = pl.program_id(1)
    bq = MOBA_BLOCK
    nb = km_ref.shape[0]
    q = q_ref[...]
    lane = lax.broadcasted_iota(jnp.int32, (1, LANES), 1)
    blk_f = lax.broadcasted_iota(jnp.int32, (nb, 1), 0).astype(F32)
    jf = j.astype(F32)
    rk = lax.broadcasted_iota(jnp.int32, (bq, bq), 0)
    rq = lax.broadcasted_iota(jnp.int32, (bq, bq), 1)
    causal = rq >= rk
    pos = lax.broadcasted_iota(jnp.int32, (bq, 1), 0).astype(F32)
    k_extra = jnp.where(lane < 3, pos, jnp.where(lane < 6, 1.0, 0.0)).astype(BF16)
    vrow = lax.broadcasted_iota(jnp.int32, (LANES, 1), 0)
    ones_row = (DH_A, 0)
    vrow_b = lax.broadcasted_iota(jnp.int32, (LANES, bq), 0).astype(F32).astype(BF16)
    is_ones_row = [vrow_b == r for r in ones_row]
    k_own = jnp.concatenate([k_ref[pl.ds(pl.multiple_of(j * bq, bq), bq), :], k_extra], axis=1)
    vt_own = vt_ref[j]
    q_augs = []
    coefs = []
    for hh in range(2):
        hmask = jnp.logical_and(lane >= hh * DH_A, lane < (hh + 1) * DH_A)
        qh = jnp.where(hmask, q, 0.0)
        head_f = (2 * hp + hh + 1).astype(F32)
        coef = jnp.exp2(jnp.zeros((1, 1), F32) - head_f) * LOG2E
        coefs.append(coef)
        scores = _nt_dot(km_ref[...], qh, precision=HIGHEST)
        scores = jnp.where(blk_f < jf, scores, NEG_INF)
        sel, _ = _top_mask(scores, blk_f, min(MOBA_TOPK, nb), axis=0)
        sel = jnp.logical_and(sel, blk_f < jf)
        sel_sc[hh] = jnp.where(sel, 1.0, 0.0)
        c3 = _split3(jnp.broadcast_to(coef, (bq, 1)))
        q3 = _split3(-(coef * pos))
        q_extra = jnp.zeros((bq, LANES), F32)
        for li, term in enumerate(c3 + q3):
            q_extra = jnp.where(lane == li, term.astype(F32), q_extra)
        qb = (qh * (DH_A ** -0.5 * LOG2E)).astype(BF16)
        q_aug = jnp.concatenate([qb, q_extra.astype(BF16)], axis=1)
        q_augs.append(q_aug)
        logits = jnp.where(causal, _nt_dot(k_own, q_aug), NEG_INF)
        m = jnp.max(logits, axis=0, keepdims=True)
        p = jnp.exp2(logits - m)
        m_sc[hh] = m
        vt_h = jnp.where(is_ones_row[hh], jnp.ones((), BF16), vt_own)
        acc_sc[hh] = _dot(vt_h, p.astype(BF16))

    def body(it, carry):
        ns, kns, vtns, offs, valids = [], [], [], [], []
        for u in range(PAST_UNROLL):
            n_raw = it * PAST_UNROLL + u
            n = jnp.minimum(n_raw, j - 1)
            ns.append(n)
            valids.append(n_raw < j)
            kns.append(jnp.concatenate([k_ref[pl.ds(pl.multiple_of(n * bq, bq), bq), :], k_extra], axis=1))
            vtns.append(vt_ref[n])
            offs.append((j - n).astype(F32) * float(bq))
        logits = [[], []]
        for hh in range(2):
            for u in range(PAST_UNROLL):
                picked = sel_sc[hh, pl.ds(ns[u], 1), :]
                keep = jnp.logical_and(picked > 0.5, valids[u])
                rb = jnp.where(keep, -(coefs[hh] * offs[u]), NEG_INF)
                logits[hh].append(_nt_dot(kns[u], q_augs[hh]) + rb)
        m_news, acc_news = [], []
        for hh in range(2):
            m_old = m_sc[hh]
            m_new = m_old
            for lg in logits[hh]:
                m_new = jnp.maximum(m_new, jnp.max(lg, axis=0, keepdims=True))
            m_news.append(m_new)
            acc_news.append(jnp.exp2(m_old - m_new) * acc_sc[hh])
        for hh in range(2):
            acc_new = acc_news[hh]
            for u in range(PAST_UNROLL):
                p = jnp.exp2(logits[hh][u] - m_news[hh])
                vt_h = jnp.where(is_ones_row[hh], jnp.ones((), BF16), vtns[u])
                acc_new = acc_new + _dot(vt_h, p.astype(BF16))
            acc_sc[hh] = acc_new
            m_sc[hh] = m_news[hh]
        return carry

    lax.fori_loop(0, lax.div(j + (PAST_UNROLL - 1), PAST_UNROLL), body, 0)
    acc0 = acc_sc[0]
    acc1 = acc_sc[1]
    o0 = acc0 / acc0[ones_row[0]:ones_row[0] + 1, :]
    o1 = acc1 / acc1[ones_row[1]:ones_row[1] + 1, :]
    o_ref[...] = jnp.where(vrow < DH_A, o0, o1).T


def _moba_prompt(z, kb, vt, kmean):
    t = z.shape[0]
    nb = t // MOBA_BLOCK
    bq = MOBA_BLOCK
    qcol = Z_QA // LANES
    return pl.pallas_call(
        _moba_prompt_kernel,
        grid=(H_A // 2, nb),
        in_specs=[
            pl.BlockSpec((bq, LANES), lambda hp, j: (j, qcol + hp)),
            pl.BlockSpec((t, LANES), lambda hp, j: (0, hp)),
            pl.BlockSpec((nb, LANES, bq), lambda hp, j: (0, hp, 0)),
            pl.BlockSpec((nb, LANES), lambda hp, j: (0, hp)),
        ],
        out_specs=pl.BlockSpec((bq, LANES), lambda hp, j: (j, hp)),
        out_shape=jax.ShapeDtypeStruct((t, W_A), F32),
        scratch_shapes=[
            pltpu.VMEM((2, 1, bq), F32),
            pltpu.VMEM((2, LANES, bq), F32),
            pltpu.VMEM((2, nb, bq), F32),
        ],
        compiler_params=pltpu.CompilerParams(
            dimension_semantics=("arbitrary", "arbitrary"), vmem_limit_bytes=VMEM_LIMIT),
        name="moba_prompt",
    )(z, kb, vt, kmean)


def _moba_sample_kernel(pt_ref, q_ref, kn_ref, vn_ref, *refs, ppg, past):
    k_refs = refs[:ppg]
    v_refs = refs[ppg:2 * ppg]
    o_ref = refs[2 * ppg]
    qs_sc, qf_sc, ksum_sc, m_sc, l_sc, o_sc = refs[2 * ppg + 1:]
    g = pl.program_id(1)
    ng = pl.num_programs(1)
    t = q_ref.shape[0]
    rows = H_A * t
    npb = o_sc.shape[0]
    ppb = MOBA_BLOCK // PAGE_SIZE
    row_i = lax.broadcasted_iota(jnp.int32, (rows, 1), 0)
    slope = jnp.exp2(-(lax.div(row_i, t) + 1).astype(F32))
    tok = lax.rem(row_i, t)
    blk_lane = lax.broadcasted_iota(jnp.int32, (1, LANES), 1)
    feat_head = lax.div(lax.broadcasted_iota(jnp.int32, (1, W_A), 1), DH_A)

    @pl.when(g == 0)
    def _():
        q = q_ref[...]
        qbd = jnp.concatenate([jnp.where(feat_head == h, q, 0.0) for h in range(H_A)], axis=0)
        qf_sc[...] = qbd
        qs_sc[...] = qbd * (DH_A ** -0.5)
        ksum_sc[...] = jnp.zeros(ksum_sc.shape, F32)
        m_sc[...] = jnp.zeros(m_sc.shape, F32)
        l_sc[...] = jnp.zeros(l_sc.shape, F32)

    qs = qs_sc[...]
    rk = lax.broadcasted_iota(jnp.int32, (1, MOBA_BLOCK), 1)
    ksum_all = ksum_sc[...]
    m_all = m_sc[...]
    l_all = l_sc[...]
    nblk = ppg // ppb
    s_blks = []
    for b in range(nblk):
        kt = jnp.concatenate([k_refs[b * ppb + p][...].reshape(W_A, PAGE_SIZE) for p in range(ppb)], axis=1)
        s_blks.append(_dot(qs, kt))
        ksum = jnp.sum(kt, axis=-1, keepdims=True)
        ksum_all = jnp.where(blk_lane == g * nblk + b, ksum, ksum_all)
    p_blks = []
    for b in range(nblk):
        n = g * nblk + b
        dist = ((past - n * MOBA_BLOCK) + tok - rk).astype(F32)
        logits = s_blks[b] - slope * dist
        m_b = jnp.max(logits, axis=-1, keepdims=True)
        p_ = jnp.exp(logits - m_b)
        p_blks.append(p_)
        m_all = jnp.where(blk_lane == n, m_b, m_all)
        l_all = jnp.where(blk_lane == n, jnp.sum(p_, axis=-1, keepdims=True), l_all)
    for b in range(nblk):
        vt = jnp.concatenate([v_refs[b * ppb + p][...].reshape(W_A, PAGE_SIZE) for p in range(ppb)], axis=1)
        o_sc[g * nblk + b] = _nt_dot(p_blks[b], vt)
    ksum_sc[...] = ksum_all
    m_sc[...] = m_all
    l_sc[...] = l_all

    @pl.when(g == ng - 1)
    def _():
        kmean = ksum_sc[...] * (1.0 / MOBA_BLOCK)
        scores = _dot(qf_sc[...], kmean, precision=HIGHEST)
        scores = jnp.where(blk_lane < npb, scores, -jnp.inf)
        sel, _ = _top_mask(scores, blk_lane.astype(F32), min(MOBA_TOPK, npb))
        kn = jnp.concatenate([kn_ref[h] for h in range(H_A)], axis=1)
        vn = jnp.concatenate([vn_ref[h] for h in range(H_A)], axis=1)
        s_own = _nt_dot(qs, kn)
        d_own = (tok - lax.broadcasted_iota(jnp.int32, (1, t), 1)).astype(F32)
        lg_own = jnp.where(d_own >= 0, s_own - slope * d_own, NEG_INF)
        m_all = m_sc[...]
        m_sel = jnp.max(jnp.where(sel, m_all, -jnp.inf), axis=-1, keepdims=True)
        m_fin = jnp.maximum(m_sel, jnp.max(lg_own, axis=-1, keepdims=True))
        w = jnp.where(sel, jnp.exp(m_all - m_fin), 0.0)
        p_own = jnp.exp(lg_own - m_fin)
        l_fin = jnp.sum(w * l_sc[...], axis=-1, keepdims=True) + jnp.sum(p_own, axis=-1, keepdims=True)
        o = _dot(p_own, vn)
        for n in range(npb):
            o = o + w[:, n:n + 1] * o_sc[n]
        o = o / l_fin
        out = jnp.zeros((t, W_A), F32)
        for h in range(H_A):
            out = out + jnp.where(feat_head == h, o[h * t:(h + 1) * t, :], 0.0)
        o_ref[...] = out


def _moba_sample(z, k_new, v_new, cache_k, cache_v, page_table):
    db, n_pages = page_table.shape
    t = z.shape[0] // db
    ppb = MOBA_BLOCK // PAGE_SIZE
    npb = n_pages // ppb
    assert n_pages == npb * ppb, "cached pages must fill whole MoBA blocks"
    ppg = 8 if n_pages % 8 == 0 else ppb
    ng = n_pages // ppg

    def page_map(b, g, pt, *, i):
        return (pt[b * n_pages + g * ppg + i], 0, 0, 0)

    page_specs = [pl.BlockSpec((None, H_A, DH_A, PAGE_SIZE), functools.partial(page_map, i=i))
                  for i in range(ppg)]
    grid_spec = pltpu.PrefetchScalarGridSpec(
        num_scalar_prefetch=1,
        grid=(db, ng),
        in_specs=[
            pl.BlockSpec((t, W_A), lambda b, g, pt: (b, Z_QA // W_A)),
            pl.BlockSpec((None, H_A, t, DH_A), lambda b, g, pt: (b, 0, 0, 0)),
            pl.BlockSpec((None, H_A, t, DH_A), lambda b, g, pt: (b, 0, 0, 0)),
        ] + page_specs + page_specs,
        out_specs=pl.BlockSpec((t, W_A), lambda b, g, pt: (b, 0)),
        scratch_shapes=[
            pltpu.VMEM((H_A * t, W_A), F32),
            pltpu.VMEM((H_A * t, W_A), F32),
            pltpu.VMEM((W_A, LANES), F32),
            pltpu.VMEM((H_A * t, LANES), F32),
            pltpu.VMEM((H_A * t, LANES), F32),
            pltpu.VMEM((npb, H_A * t, W_A), F32),
        ],
    )
    return pl.pallas_call(
        functools.partial(_moba_sample_kernel, ppg=ppg, past=n_pages * PAGE_SIZE),
        grid_spec=grid_spec,
        out_shape=jax.ShapeDtypeStruct((db * t, W_A), F32),
        compiler_params=pltpu.CompilerParams(
            dimension_semantics=("arbitrary", "arbitrary"), vmem_limit_bytes=VMEM_LIMIT),
        name="moba_sample",
    )(page_table.reshape(-1), z, k_new, v_new, *([cache_k] * ppg), *([cache_v] * ppg))


def _mlstm_kernel(q_ref, k_ref, v_ref, g_ref, c0_ref, n0_ref, m0_ref,
                  h_ref, c_ref, n_ref, m_ref, c_sc, n_sc, m_sc, *, lc, cps):
    step = pl.program_id(1)
    nsteps = pl.num_programs(1)
    lp = max(lc, MLSTM_CHUNK)

    @pl.when(step == 0)
    def _():
        c_sc[...] = c0_ref[0]
        n_sc[...] = n0_ref[0]
        m_sc[...] = m0_ref[0]

    def padded(a):
        if a.shape[0] == lp:
            return a
        return jnp.concatenate([a, jnp.zeros((lp - a.shape[0],) + a.shape[1:], a.dtype)], axis=0)

    row = lax.broadcasted_iota(jnp.int32, (lp, lp), 0)
    col = lax.broadcasted_iota(jnp.int32, (lp, lp), 1)
    causal = row >= col
    tril = jnp.where(causal, 1.0, 0.0).astype(BF16)
    pick_8 = jnp.where(lax.broadcasted_iota(jnp.int32, (16, LANES), 0)
                       == lax.broadcasted_iota(jnp.int32, (16, LANES), 1), 1.0, 0.0).astype(BF16)
    gate_lane = lax.broadcasted_iota(jnp.int32, (1, LANES), 1)
    eye = jnp.where(lax.broadcasted_iota(jnp.int32, (DH_M, DH_M), 0)
                    == lax.broadcasted_iota(jnp.int32, (DH_M, DH_M), 1), 1.0, 0.0).astype(BF16)
    rvalid = lax.broadcasted_iota(jnp.int32, (lp, 1), 0) < lc
    cvalid = lax.broadcasted_iota(jnp.int32, (1, lp), 1) < lc
    m_all = m_sc[...]
    n_all = n_sc[...]
    c_all = [c_sc[h] for h in range(H_M)]
    pre = []
    for ci in range(cps):
        rs = slice(ci * lc, (ci + 1) * lc)
        q = padded(q_ref[rs, :])
        k = padded(k_ref[rs, :]) * (DH_M ** -0.5)
        v = padded(v_ref[rs, :])
        gts = padded(g_ref[rs, :])
        lf = jax.nn.log_sigmoid(gts)
        bcum = sum(_dot(tril, part) for part in _split3(lf))
        comb_t = sum(_nt_dot(pick_8, part) for part in _split3(jnp.where(gate_lane < H_M, gts, bcum)))
        heads = []
        for h in range(H_M):
            qh = q[:, h * DH_M:(h + 1) * DH_M]
            kh = k[:, h * DH_M:(h + 1) * DH_M]
            vh = v[:, h * DH_M:(h + 1) * DH_M]
            bc = bcum[:, H_M + h:H_M + h + 1]
            ic = gts[:, h:h + 1]
            br = comb_t[H_M + h:H_M + h + 1, :]
            ir = comb_t[h:h + 1, :]
            dmat = jnp.where(causal, bc - br + ir, -jnp.inf)
            qhb = qh.astype(BF16)
            heads.append(dict(
                qh=qh, qhb=qhb, kh=kh, vhb=vh.astype(BF16), bc=bc, ic=ic, br=br, ir=ir, dmat=dmat,
                m_intra=jnp.max(dmat, axis=-1, keepdims=True),
                qk=_nt_dot(qhb, kh.astype(BF16)),
                kh_t=sum(_nt_dot(eye, part) for part in _split3(kh)),
            ))
        pre.append(heads)
    for ci in range(cps):
        rs = slice(ci * lc, (ci + 1) * lc)
        m_out, n_out, c_out = [], [], []
        for h in range(H_M):
            d = pre[ci][h]
            bc, ic, dmat = d["bc"], d["ic"], d["dmat"]
            m_prev = m_all[h:h + 1, :]
            c_prev = c_all[h]
            n_prev = n_all[h:h + 1, :]
            a_inter = bc + m_prev
            m_t = jnp.maximum(a_inter, d["m_intra"])
            w_inter = jnp.exp(a_inter - m_t)
            s = d["qk"] * jnp.exp(dmat - m_t)
            num = w_inter * _dot(d["qhb"], c_prev.astype(BF16)) + _dot(s.astype(BF16), d["vhb"])
            den = (w_inter * jnp.sum(d["qh"] * n_prev, axis=-1, keepdims=True)
                   + jnp.sum(s, axis=-1, keepdims=True))
            h_out = num / jnp.maximum(jnp.abs(den), jnp.exp(-m_t))
            h_ref[rs, h * DH_M:(h + 1) * DH_M] = h_out[:lc]
            m_new = m_t[lc - 1:lc, :]
            b_last = bc[lc - 1:lc, :]
            g_c = jnp.exp(b_last + m_prev - m_new)
            g_s = jnp.where(rvalid, jnp.exp(b_last - bc + ic - m_new), 0.0)
            g_s_row = jnp.where(cvalid, jnp.exp(b_last - d["br"] + d["ir"] - m_new), 0.0)
            kg_t = (d["kh_t"] * g_s_row).astype(BF16)
            c_out.append(g_c * c_prev + _dot(kg_t, d["vhb"]))
            n_out.append(g_c * n_prev + jnp.sum(g_s * d["kh"], axis=0, keepdims=True))
            m_out.append(m_new)
        m_all = jnp.concatenate(m_out, axis=0)
        n_all = jnp.concatenate(n_out, axis=0)
        c_all = c_out
    for h in range(H_M):
        c_sc[h] = c_all[h]
    n_sc[...] = n_all
    m_sc[...] = m_all

    @pl.when(step == nsteps - 1)
    def _():
        c_ref[0] = c_sc[...]
        n_ref[0] = n_sc[...]
        m_ref[0] = m_sc[...]


def _mlstm(z, gif, c0, n0, m0, *, lc, cps):
    b = c0.shape[0]
    t = z.shape[0] // b
    rows = lc * cps
    nc = t // rows
    assert nc * rows == t
    row = lambda bi, ci: bi * nc + ci
    wcol = lambda off: off // W_M
    return pl.pallas_call(
        functools.partial(_mlstm_kernel, lc=lc, cps=cps),
        grid=(b, nc),
        in_specs=[
            pl.BlockSpec((rows, W_M), lambda bi, ci: (row(bi, ci), wcol(Z_QM))),
            pl.BlockSpec((rows, W_M), lambda bi, ci: (row(bi, ci), wcol(Z_KM))),
            pl.BlockSpec((rows, W_M), lambda bi, ci: (row(bi, ci), wcol(Z_VM))),
            pl.BlockSpec((rows, LANES), lambda bi, ci: (row(bi, ci), 0)),
            pl.BlockSpec((1, H_M, DH_M, DH_M), lambda bi, ci: (bi, 0, 0, 0)),
            pl.BlockSpec((1, H_M, DH_M), lambda bi, ci: (bi, 0, 0)),
            pl.BlockSpec((1, H_M, 1), lambda bi, ci: (bi, 0, 0)),
        ],
        out_specs=[
            pl.BlockSpec((rows, W_M), lambda bi, ci: (row(bi, ci), 0)),
            pl.BlockSpec((1, H_M, DH_M, DH_M), lambda bi, ci: (bi, 0, 0, 0)),
            pl.BlockSpec((1, H_M, DH_M), lambda bi, ci: (bi, 0, 0)),
            pl.BlockSpec((1, H_M, 1), lambda bi, ci: (bi, 0, 0)),
        ],
        out_shape=[
            jax.ShapeDtypeStruct((b * t, W_M), F32),
            jax.ShapeDtypeStruct((b, H_M, DH_M, DH_M), F32),
            jax.ShapeDtypeStruct((b, H_M, DH_M), F32),
            jax.ShapeDtypeStruct((b, H_M, 1), F32),
        ],
        scratch_shapes=[
            pltpu.VMEM((H_M, DH_M, DH_M), F32),
            pltpu.VMEM((H_M, DH_M), F32),
            pltpu.VMEM((H_M, 1), F32),
        ],
        compiler_params=pltpu.CompilerParams(
            dimension_semantics=("arbitrary", "arbitrary"), vmem_limit_bytes=VMEM_LIMIT),
        name="mlstm",
    )(z, z, z, gif, c0, n0, m0[..., None])


def _layernorm(y, g, b):
    mu = jnp.mean(y, axis=-1, keepdims=True)
    var = jnp.mean(jnp.square(y - mu), axis=-1, keepdims=True)
    return (y - mu) * lax.rsqrt(var + LN_EPS) * g + b


def _finish_kernel(x_ref, a_ref, mh_ref, om_ref, ga_ref, gm_ref, wa_ref, wm_ref, wo_ref,
                   g1_ref, b1_ref, wr_ref, br_ref,
                   h_ref, hb_ref, gates_ref, gates_t_ref, cnt_ref):
    a = _dot(a_ref[...].astype(BF16), wa_ref[...])
    mo = mh_ref[...] * jax.nn.sigmoid(om_ref[...])
    m = _dot(mo.astype(BF16), wm_ref[...])
    merged = jax.nn.sigmoid(ga_ref[...]) * a + jax.nn.sigmoid(gm_ref[...]) * m
    y = DN_ALPHA * x_ref[...] + _dot(merged.astype(BF16), wo_ref[...])
    h = _layernorm(y, g1_ref[...], b1_ref[...])
    h_ref[...] = h
    hb_ref[...] = h.astype(BF16)
    logits = _dot(h, wr_ref[...], precision=HIGHEST) + br_ref[...]
    lane_f = lax.broadcasted_iota(jnp.int32, (1, LANES), 1).astype(F32)
    sel, top = _top_mask(logits, lane_f, TOP_K)
    e = jnp.where(sel, jnp.exp(logits - top), 0.0)
    gates = e / jnp.sum(e, axis=-1, keepdims=True)
    gates_ref[...] = gates
    gates_t_ref[...] = gates.T
    cnt_ref[0] = jnp.sum(jnp.where(gates > 0.0, 1.0, 0.0), axis=0, keepdims=True).astype(jnp.int32)


def _finish(x2d, a, mh, z, wa, wm, wo, g1, b1, wr, br, *, tm):
    t = x2d.shape[0]
    nt = t // tm
    const = lambda i: (0, 0)
    return pl.pallas_call(
        _finish_kernel,
        grid=(nt,),
        in_specs=[
            pl.BlockSpec((tm, D_MODEL), lambda i: (i, 0)),
            pl.BlockSpec((tm, W_A), lambda i: (i, 0)),
            pl.BlockSpec((tm, W_M), lambda i: (i, 0)),
            pl.BlockSpec((tm, W_M), lambda i: (i, Z_OM // W_M)),
            pl.BlockSpec((tm, D_MODEL), lambda i: (i, Z_GA // D_MODEL)),
            pl.BlockSpec((tm, D_MODEL), lambda i: (i, Z_GM // D_MODEL)),
            pl.BlockSpec(wa.shape, const),
            pl.BlockSpec(wm.shape, const),
            pl.BlockSpec(wo.shape, const),
            pl.BlockSpec(g1.shape, const),
            pl.BlockSpec(b1.shape, const),
            pl.BlockSpec(wr.shape, const),
            pl.BlockSpec(br.shape, const),
        ],
        out_specs=[
            pl.BlockSpec((tm, D_MODEL), lambda i: (i, 0)),
            pl.BlockSpec((tm, D_MODEL), lambda i: (i, 0)),
            pl.BlockSpec((tm, LANES), lambda i: (i, 0)),
            pl.BlockSpec((LANES, tm), lambda i: (0, i)),
            pl.BlockSpec((1, 1, LANES), lambda i: (i, 0, 0)),
        ],
        out_shape=[
            jax.ShapeDtypeStruct((t, D_MODEL), F32),
            jax.ShapeDtypeStruct((t, D_MODEL), BF16),
            jax.ShapeDtypeStruct((t, LANES), F32),
            jax.ShapeDtypeStruct((LANES, t), F32),
            jax.ShapeDtypeStruct((nt, 1, LANES), jnp.int32),
        ],
        compiler_params=pltpu.CompilerParams(
            dimension_semantics=("arbitrary",), vmem_limit_bytes=VMEM_LIMIT),
        name="finish",
    )(x2d, a, mh, z, z, z, wa, wm, wo, g1, b1, wr, br)


MLSTM_CHUNKS_PER_STEP = 4
MOE_TILE = 1024
MOE_SEG = 512
MOE_SEG_ROWS = 80
MOE_RANK_BLOCK = 256


def _moe_kernel(cnt_ref, hb_ref, g_ref, gt_ref, wg_ref, bg_ref, wu_ref, bu_ref, wd_ref, bd_ref,
                y_ref, posc_sc, selc_sc, posr_sc, *, cnt_per_seg, seg):
    i = pl.program_id(0)
    e = pl.program_id(1)
    tt = hb_ref.shape[0]
    nseg = tt // seg
    rseg = MOE_SEG_ROWS
    blk = MOE_RANK_BLOCK

    @pl.when(e == 0)
    def _():
        y_ref[...] = jnp.zeros(y_ref.shape, F32)

    cnt_max = 0
    for sg in range(nseg):
        first = (i * nseg + sg) * cnt_per_seg
        cnt_seg = cnt_ref[first, e]
        for r in range(1, cnt_per_seg):
            cnt_seg = cnt_seg + cnt_ref[first + r, e]
        cnt_max = jnp.maximum(cnt_max, cnt_seg)

    @pl.when(cnt_max > 0)
    def _():
        lane = lax.broadcasted_iota(jnp.int32, (1, LANES), 1)
        gcol = jnp.sum(jnp.where(lane == e, g_ref[...], 0.0), axis=-1, keepdims=True)
        selc = jnp.broadcast_to(jnp.where(gcol > 0.0, 1.0, 0.0), (tt, LANES))
        selc_sc[...] = selc
        rr = lax.broadcasted_iota(jnp.int32, (blk, blk), 0)
        cc = lax.broadcasted_iota(jnp.int32, (blk, blk), 1)
        strict_lower = jnp.where(rr > cc, 1.0, 0.0).astype(BF16)
        for bi in range(tt // blk):
            if (bi * blk) % seg == 0:
                offs = jnp.zeros((1, LANES), F32)
            sb = selc[bi * blk:(bi + 1) * blk]
            posc_sc[bi * blk:(bi + 1) * blk, :] = _dot(strict_lower, sb.astype(BF16)) + offs
            offs = offs + jnp.sum(sb, axis=0, keepdims=True)
        grow = gt_ref[pl.ds(lax.rem(e, 8), 1), :]
        selr = jnp.where(grow > 0.0, 1.0, 0.0)
        ru = lax.broadcasted_iota(jnp.int32, (LANES, LANES), 0)
        cu = lax.broadcasted_iota(jnp.int32, (LANES, LANES), 1)
        strict_upper = jnp.where(ru < cu, 1.0, 0.0).astype(BF16)
        for ci in range(tt // LANES):
            if (ci * LANES) % seg == 0:
                offr = jnp.zeros((16, 1), F32)
            sr = jnp.broadcast_to(selr[:, ci * LANES:(ci + 1) * LANES], (16, LANES))
            posr_sc[:, ci * LANES:(ci + 1) * LANES] = _dot(sr.astype(BF16), strict_upper) + offr
            offr = offr + jnp.sum(sr, axis=-1, keepdims=True)
        slot_sub = lax.broadcasted_iota(jnp.int32, (rseg, LANES), 0).astype(F32)
        slot_lane = lax.broadcasted_iota(jnp.int32, (1, LANES), 1).astype(F32)

        def one_pass(s, carry):
            base = (s * rseg).astype(F32)
            xs_parts, gs_parts = [], []
            for sg in range(nseg):
                pm_chunks = []
                for ci in range(sg * seg // LANES, (sg + 1) * seg // LANES):
                    pr = posr_sc[0:1, ci * LANES:(ci + 1) * LANES] - base
                    sr = selr[:, ci * LANES:(ci + 1) * LANES]
                    pm_chunks.append(jnp.logical_and(pr == slot_sub, sr > 0.0))
                pmask = jnp.concatenate(pm_chunks, axis=1)
                gs_parts.append(jnp.sum(jnp.where(pmask, grow[:, sg * seg:(sg + 1) * seg], 0.0),
                                        axis=-1, keepdims=True))
                xs_parts.append(_dot(jnp.where(pmask, 1.0, 0.0).astype(BF16),
                                     hb_ref[sg * seg:(sg + 1) * seg, :]).astype(BF16))
            xs = jnp.concatenate(xs_parts, axis=0)
            gs = jnp.concatenate(gs_parts, axis=0)
            gp = jnp.minimum(_dot(xs, wg_ref[0]) + bg_ref[0], SWIGLU_LIMIT)
            up = jnp.clip(_dot(xs, wu_ref[0]) + bu_ref[0], -SWIGLU_LIMIT, SWIGLU_LIMIT)
            hdn = (up + 1.0) * (gp * jax.nn.sigmoid(SWIGLU_ALPHA * gp))
            yb = ((_dot(hdn.astype(BF16), wd_ref[0]) + bd_ref[0]) * gs).astype(BF16)
            pad = jnp.zeros((LANES - rseg, yb.shape[1]), BF16)
            for sg in range(nseg):
                rows = slice(sg * seg, (sg + 1) * seg)
                hit = jnp.logical_and(posc_sc[rows, :] - base == slot_lane, slot_lane < float(rseg))
                pt = jnp.where(jnp.logical_and(hit, selc_sc[rows, :] > 0.0), 1.0, 0.0).astype(BF16)
                y_seg = jnp.concatenate([yb[sg * rseg:(sg + 1) * rseg], pad], axis=0)
                y_ref[rows, :] += _dot(pt, y_seg)
            return carry

        lax.fori_loop(0, lax.div(cnt_max + (rseg - 1), rseg), one_pass, 0)


def _moe(hb, gates, gates_t, cnt, wg, bg, wu, bu, wd, bd, *, tt):
    t = hb.shape[0]
    nt = t // tt
    seg = min(MOE_SEG, tt)
    cnt_per_seg = cnt.shape[0] * seg // t
    assert cnt_per_seg * t == cnt.shape[0] * seg and tt % seg == 0 and seg % MOE_RANK_BLOCK == 0
    wspec = pl.BlockSpec((1, D_MODEL, D_MODEL), lambda i, e, c: (e, 0, 0))
    bspec = pl.BlockSpec((1, 1, D_MODEL), lambda i, e, c: (e, 0, 0))
    grid_spec = pltpu.PrefetchScalarGridSpec(
        num_scalar_prefetch=1,
        grid=(nt, N_EXPERTS),
        in_specs=[
            pl.BlockSpec((tt, D_MODEL), lambda i, e, c: (i, 0)),
            pl.BlockSpec((tt, LANES), lambda i, e, c: (i, 0)),
            pl.BlockSpec((8, tt), lambda i, e, c: (e // 8, i)),
            wspec, bspec, wspec, bspec, wspec, bspec,
        ],
        out_specs=pl.BlockSpec((tt, D_MODEL), lambda i, e, c: (i, 0)),
        scratch_shapes=[
            pltpu.VMEM((tt, LANES), F32),
            pltpu.VMEM((tt, LANES), F32),
            pltpu.VMEM((16, tt), F32),
        ],
    )
    return pl.pallas_call(
        functools.partial(_moe_kernel, cnt_per_seg=cnt_per_seg, seg=seg),
        grid_spec=grid_spec,
        out_shape=jax.ShapeDtypeStruct((t, D_MODEL), F32),
        compiler_params=pltpu.CompilerParams(
            dimension_semantics=("arbitrary", "arbitrary"), vmem_limit_bytes=VMEM_LIMIT),
        name="moe",
    )(cnt, hb, gates, gates_t, wg, bg, wu, bu, wd, bd)


def _ln2_kernel(h_ref, y_ref, g_ref, b_ref, o_ref):
    o_ref[...] = _layernorm(DN_ALPHA * h_ref[...] + y_ref[...], g_ref[...], b_ref[...])


def _ln2(h, y, g, b, *, tm):
    t = h.shape[0]
    spec = pl.BlockSpec((tm, D_MODEL), lambda i: (i, 0))
    const = pl.BlockSpec((1, D_MODEL), lambda i: (0, 0))
    return pl.pallas_call(
        _ln2_kernel,
        grid=(t // tm,),
        in_specs=[spec, spec, const, const],
        out_specs=spec,
        out_shape=jax.ShapeDtypeStruct((t, D_MODEL), F32),
        compiler_params=pltpu.CompilerParams(dimension_semantics=("arbitrary",)),
        name="ln2",
    )(h, y, g, b)


def _group(x2d, weights, *, ps, tm, moe_tile, attn, mlstm_state, lc):
    z, kb, vt, kpages, vpages, gif, kmean = _project(
        x2d, weights["wz"], weights["wkv"], weights["wif"], weights["bif"], ps=ps, tm=tm)
    a = attn(z, kb, vt, kmean.reshape(kmean.shape[0], W_A), kpages, vpages)
    t_seq = x2d.shape[0] // mlstm_state[0].shape[0]
    cps = MLSTM_CHUNKS_PER_STEP if (t_seq // lc) % MLSTM_CHUNKS_PER_STEP == 0 else 1
    mh, c_t, n_t, m_t = _mlstm(z, gif, *mlstm_state, lc=lc, cps=cps)
    h, hb, gates, gates_t, cnt = _finish(
        x2d, a, mh, z, weights["wa"], weights["wm"], weights["wo"], weights["g1"], weights["b1"],
        weights["wr"], weights["br"], tm=tm)
    y = _moe(hb, gates, gates_t, cnt.reshape(cnt.shape[0], LANES),
             weights["wg"], weights["bg"], weights["wu"], weights["bu"], weights["wd"], weights["bd"],
             tt=moe_tile)
    out = _ln2(h, y, weights["g2"], weights["b2"], tm=tm)
    return out, kpages, vpages, c_t, n_t, m_t[..., 0]


def kernel(x_prompt, x_sample, cache_k, cache_v, page_table, state_C, state_n, state_m,
           w_in, b_if, w_branch_a, w_branch_m, w_out, ln1_g, ln1_b,
           w_router, b_router, w_gate, b_gate, w_up, b_up, w_down, b_down, ln2_g, ln2_b):
    depth = w_in.shape[0]
    assert depth == 1, "single-layer step"
    bp, seq, _ = x_prompt.shape
    db, dseq, _ = x_sample.shape
    l = 0
    w = w_in[l]
    o_qa, o_ka, o_va, o_qm, o_om_end = 0, W_A, 2 * W_A, 3 * W_A, 3 * W_A + 4 * W_M
    o_if = o_om_end
    o_ga = o_if + 2 * H_M
    weights = {
        "wz": jnp.concatenate([w[:, o_ga:], w[:, o_qa:o_ka], w[:, o_qm:o_om_end]], axis=1).astype(BF16),
        "wkv": w[:, o_ka:o_qm].astype(BF16),
        "wif": jnp.pad(w[:, o_if:o_ga], ((0, 0), (0, LANES - 2 * H_M))),
        "bif": jnp.pad(b_if[l], (0, LANES - 2 * H_M))[None, :],
        "wa": w_branch_a[l].astype(BF16),
        "wm": w_branch_m[l].astype(BF16),
        "wo": w_out[l].astype(BF16),
        "g1": ln1_g[l][None, :], "b1": ln1_b[l][None, :],
        "wr": jnp.pad(w_router[l], ((0, 0), (0, LANES - N_EXPERTS))),
        "br": jnp.pad(b_router[l], (0, LANES - N_EXPERTS), constant_values=NEG_INF)[None, :],
        "wg": w_gate[l].astype(BF16), "bg": b_gate[l][:, None, :],
        "wu": w_up[l].astype(BF16), "bu": b_up[l][:, None, :],
        "wd": w_down[l].astype(BF16), "bd": b_down[l][:, None, :],
        "g2": ln2_g[l][None, :], "b2": ln2_b[l][None, :],
    }
    assert bp == 1, "prompt batch of one sequence"
    zero_state = (jnp.zeros((bp, H_M, DH_M, DH_M), F32), jnp.zeros((bp, H_M, DH_M), F32),
                  jnp.zeros((bp, H_M), F32))
    yp, kp, vp, cp, np_, mp = _group(
        x_prompt.reshape(bp * seq, D_MODEL), weights, ps=PAGE_SIZE, tm=256,
        moe_tile=min(MOE_TILE, bp * seq),
        attn=lambda z, kb, vt, kmean, kpg, vpg: _moba_prompt(z, kb, vt, kmean),
        mlstm_state=zero_state, lc=math.gcd(seq, MLSTM_CHUNK))
    ck = jnp.swapaxes(cache_k.reshape(cache_k.shape[1:]), -1, -2)
    cv = jnp.swapaxes(cache_v.reshape(cache_v.shape[1:]), -1, -2)
    ys, ks, vs, cs, ns, ms = _group(
        x_sample.reshape(db * dseq, D_MODEL), weights, ps=dseq, tm=256,
        moe_tile=min(MOE_TILE, db * dseq),
        attn=lambda z, kb, vt, kmean, kpg, vpg: _moba_sample(z, kpg, vpg, ck, cv, page_table),
        mlstm_state=(state_C[l], state_n[l], state_m[l]), lc=math.gcd(dseq, MLSTM_CHUNK))
    n_pg = seq // PAGE_SIZE
    return (yp.reshape(bp, seq, D_MODEL), ys.reshape(db, dseq, D_MODEL),
            kp.reshape(1, bp, n_pg, H_A, PAGE_SIZE, DH_A), vp.reshape(1, bp, n_pg, H_A, PAGE_SIZE, DH_A),
            ks.reshape(1, db, H_A, dseq, DH_A), vs.reshape(1, db, H_A, dseq, DH_A),
            cp[None], np_[None], mp[None], cs[None], ns[None], ms[None])
```

```python
import functools
import math

import jax
import jax.numpy as jnp
from jax import lax
from jax.experimental import pallas as pl
from jax.experimental.pallas import tpu as pltpu

F32 = jnp.float32
BF16 = jnp.bfloat16
HIGHEST = lax.Precision.HIGHEST

D_MODEL = 1024
PAGE_SIZE = 128
H_A = 8
DH_A = 64
W_A = H_A * DH_A
MOBA_BLOCK = 256
MOBA_TOPK = 3
H_M = 4
DH_M = 128
W_M = H_M * DH_M
MLSTM_CHUNK = 64
N_EXPERTS = 32
TOP_K = 4
SWIGLU_LIMIT = 7.0
SWIGLU_ALPHA = 1.702
DN_ALPHA = 2.0 ** 0.25
LN_EPS = 1e-5
NEG_INF = -1e30

LANES = 128
Z_GA, Z_GM, Z_QA, Z_QM, Z_KM, Z_VM, Z_OM = 0, 1024, 2048, 2560, 3072, 3584, 4096
Z_WIDTH = 4608
VMEM_LIMIT = 56 * 1024 * 1024


def _nt_dot(a, b, precision=None):
    return lax.dot_general(a, b, (((1,), (1,)), ((), ())), precision=precision,
                           preferred_element_type=F32)


def _dot(a, b, precision=None):
    return jnp.dot(a, b, precision=precision, preferred_element_type=F32)


def _top_mask(scores, idx_f, k, axis=-1):
    n = scores.shape[axis]
    sel = jnp.zeros(scores.shape, jnp.bool_)
    work = scores
    first_max = None
    for _ in range(k):
        mx = jnp.max(work, axis=axis, keepdims=True)
        if first_max is None:
            first_max = mx
        cand = work == mx
        pick_idx = jnp.min(jnp.where(cand, idx_f, float(n)), axis=axis, keepdims=True)
        pick = idx_f == pick_idx
        sel = jnp.logical_or(sel, pick)
        work = jnp.where(pick, -jnp.inf, work)
    return sel, first_max


def _proj_kernel(x_ref, wz_ref, wkv_ref, wif_ref, bif_ref,
                 z_ref, kb_ref, vt_ref, kp_ref, vp_ref, gif_ref, kmean_ref, *, ps):
    x = x_ref[...]
    tm = x.shape[0]
    xb = x.astype(BF16)
    gif_ref[...] = _dot(x, wif_ref[...], precision=HIGHEST) + bif_ref[...]
    for c in range(Z_WIDTH // 512):
        z_ref[:, c * 512:(c + 1) * 512] = _dot(xb, wz_ref[:, c * 512:(c + 1) * 512])
    for c, pref in ((0, kp_ref), (1, vp_ref)):
        r = _dot(xb, wkv_ref[:, c * W_A:(c + 1) * W_A])
        if c == 0:
            kb_ref[...] = r.astype(BF16)
        else:
            vt_ref[0] = r.T.astype(BF16)
        for h in range(H_A):
            pref[:, h, :, :] = r[:, h * DH_A:(h + 1) * DH_A].reshape(tm // ps, ps, DH_A)
        if c == 0:
            kmean_ref[...] = jnp.mean(r.reshape(tm // MOBA_BLOCK, MOBA_BLOCK, W_A), axis=1, keepdims=True)


def _project(x2d, wz, wkv, wif, bif, *, ps, tm):
    t = x2d.shape[0]
    assert tm == MOBA_BLOCK, "one MoBA block of keys per row tile"
    grid = (t // tm,)
    const = lambda i: (0, 0)
    return pl.pallas_call(
        functools.partial(_proj_kernel, ps=ps),
        grid=grid,
        in_specs=[
            pl.BlockSpec((tm, D_MODEL), lambda i: (i, 0)),
            pl.BlockSpec(wz.shape, const),
            pl.BlockSpec(wkv.shape, const),
            pl.BlockSpec(wif.shape, const),
            pl.BlockSpec(bif.shape, const),
        ],
        out_specs=[
            pl.BlockSpec((tm, Z_WIDTH), lambda i: (i, 0)),
            pl.BlockSpec((tm, W_A), lambda i: (i, 0)),
            pl.BlockSpec((1, W_A, tm), lambda i: (i, 0, 0)),
            pl.BlockSpec((tm // ps, H_A, ps, DH_A), lambda i: (i, 0, 0, 0)),
            pl.BlockSpec((tm // ps, H_A, ps, DH_A), lambda i: (i, 0, 0, 0)),
            pl.BlockSpec((tm, LANES), lambda i: (i, 0)),
            pl.BlockSpec((tm // MOBA_BLOCK, 1, W_A), lambda i: (i, 0, 0)),
        ],
        out_shape=[
            jax.ShapeDtypeStruct((t, Z_WIDTH), F32),
            jax.ShapeDtypeStruct((t, W_A), BF16),
            jax.ShapeDtypeStruct((t // tm, W_A, tm), BF16),
            jax.ShapeDtypeStruct((t // ps, H_A, ps, DH_A), F32),
            jax.ShapeDtypeStruct((t // ps, H_A, ps, DH_A), F32),
            jax.ShapeDtypeStruct((t, LANES), F32),
            jax.ShapeDtypeStruct((t // MOBA_BLOCK, 1, W_A), F32),
        ],
        compiler_params=pltpu.CompilerParams(
            dimension_semantics=("arbitrary",), vmem_limit_bytes=VMEM_LIMIT),
        name="proj",
    )(x2d, wz, wkv, wif, bif)


PAST_UNROLL = 16


LOG2E = 1.4426950408889634


def _split3(x):
    hi = x.astype(BF16)
    r1 = x - hi.astype(F32)
    mid = r1.astype(BF16)
    lo = (r1 - mid.astype(F32)).astype(BF16)
    return hi, mid, lo


def _moba_prompt_kernel(q_ref, k_ref, vt_ref, km_ref, o_ref, m_sc, acc_sc, sel_sc):
    hp = pl.program_id(0)
    j = pl.program_id(1)
    bq = MOBA_BLOCK
    nb = km_ref.shape[0]
    q = q_ref[...]
    lane = lax.broadcasted_iota(jnp.int32, (1, LANES), 1)
    blk_f = lax.broadcasted_iota(jnp.int32, (nb, 1), 0).astype(F32)
    jf = j.astype(F32)
    rk = lax.broadcasted_iota(jnp.int32, (bq, bq), 0)
    rq = lax.broadcasted_iota(jnp.int32, (bq, bq), 1)
    causal = rq >= rk
    pos = lax.broadcasted_iota(jnp.int32, (bq, 1), 0).astype(F32)
    k_extra = jnp.where(lane < 3, pos, jnp.where(lane < 6, 1.0, 0.0)).astype(BF16)
    vrow = lax.broadcasted_iota(jnp.int32, (LANES, 1), 0)
    ones_row = (DH_A, 0)
    vrow_b = lax.broadcasted_iota(jnp.int32, (LANES, bq), 0).astype(F32).astype(BF16)
    is_ones_row = [vrow_b == r for r in ones_row]
    k_own = jnp.concatenate([k_ref[pl.ds(pl.multiple_of(j * bq, bq), bq), :], k_extra], axis=1)
    vt_own = vt_ref[j]
    q_augs = []
    coefs = []
    for hh in range(2):
        hmask = jnp.logical_and(lane >= hh * DH_A, lane < (hh + 1) * DH_A)
        qh = jnp.where(hmask, q, 0.0)
        head_f = (2 * hp + hh + 1).astype(F32)
        coef = jnp.exp2(jnp.zeros((1, 1), F32) - head_f) * LOG2E
        coefs.append(coef)
        scores = _nt_dot(km_ref[...], qh, precision=HIGHEST)
        scores = jnp.where(blk_f < jf, scores, NEG_INF)
        sel, _ = _top_mask(scores, blk_f, min(MOBA_TOPK, nb), axis=0)
        sel = jnp.logical_and(sel, blk_f < jf)
        sel_sc[hh] = jnp.where(sel, 1.0, 0.0)
        c3 = _split3(jnp.broadcast_to(coef, (bq, 1)))
        q3 = _split3(-(coef * pos))
        q_extra = jnp.zeros((bq, LANES), F32)
        for li, term in enumerate(c3 + q3):
            q_extra = jnp.where(lane == li, term.astype(F32), q_extra)
        qb = (qh * (DH_A ** -0.5 * LOG2E)).astype(BF16)
        q_aug = jnp.concatenate([qb, q_extra.astype(BF16)], axis=1)
        q_augs.append(q_aug)
        logits = jnp.where(causal, _nt_dot(k_own, q_aug), NEG_INF)
        m = jnp.max(logits, axis=0, keepdims=True)
        p = jnp.exp2(logits - m)
        m_sc[hh] = m
        vt_h = jnp.where(is_ones_row[hh], jnp.ones((), BF16), vt_own)
        acc_sc[hh] = _dot(vt_h, p.astype(BF16))

    def body(it, carry):
        ns, kns, vtns, offs, valids = [], [], [], [], []
        for u in range(PAST_UNROLL):
            n_raw = it * PAST_UNROLL + u
            n = jnp.minimum(n_raw, j - 1)
            ns.append(n)
            valids.append(n_raw < j)
            kns.append(jnp.concatenate([k_ref[pl.ds(pl.multiple_of(n * bq, bq), bq), :], k_extra], axis=1))
            vtns.append(vt_ref[n])
            offs.append((j - n).astype(F32) * float(bq))
        logits = [[], []]
        for hh in range(2):
            for u in range(PAST_UNROLL):
                picked = sel_sc[hh, pl.ds(ns[u], 1), :]
                keep = jnp.logical_and(picked > 0.5, valids[u])
                rb = jnp.where(keep, -(coefs[hh] * offs[u]), NEG_INF)
                logits[hh].append(_nt_dot(kns[u], q_augs[hh]) + rb)
        m_news, acc_news = [], []
        for hh in range(2):
            m_old = m_sc[hh]
            m_new = m_old
            for lg in logits[hh]:
                m_new = jnp.maximum(m_new, jnp.max(lg, axis=0, keepdims=True))
            m_news.append(m_new)
            acc_news.append(jnp.exp2(m_old - m_new) * acc_sc[hh])
        for hh in range(2):
            acc_new = acc_news[hh]
            for u in range(PAST_UNROLL):
                p = jnp.exp2(logits[hh][u] - m_news[hh])
                vt_h = jnp.where(is_ones_row[hh], jnp.ones((), BF16), vtns[u])
                acc_new = acc_new + _dot(vt_h, p.astype(BF16))
            acc_sc[hh] = acc_new
            m_sc[hh] = m_news[hh]
        return carry

    lax.fori_loop(0, lax.div(j + (PAST_UNROLL - 1), PAST_UNROLL), body, 0)
    acc0 = acc_sc[0]
    acc1 = acc_sc[1]
    o0 = acc0 / acc0[ones_row[0]:ones_row[0] + 1, :]
    o1 = acc1 / acc1[ones_row[1]:ones_row[1] + 1, :]
    o_ref[...] = jnp.where(vrow < DH_A, o0, o1).T


def _moba_prompt(z, kb, vt, kmean):
    t = z.shape[0]
    nb = t // MOBA_BLOCK
    bq = MOBA_BLOCK
    qcol = Z_QA // LANES
    return pl.pallas_call(
        _moba_prompt_kernel,
        grid=(H_A // 2, nb),
        in_specs=[
            pl.BlockSpec((bq, LANES), lambda hp, j: (j, qcol + hp)),
            pl.BlockSpec((t, LANES), lambda hp, j: (0, hp)),
            pl.BlockSpec((nb, LANES, bq), lambda hp, j: (0, hp, 0)),
            pl.BlockSpec((nb, LANES), lambda hp, j: (0, hp)),
        ],
        out_specs=pl.BlockSpec((bq, LANES), lambda hp, j: (j, hp)),
        out_shape=jax.ShapeDtypeStruct((t, W_A), F32),
        scratch_shapes=[
            pltpu.VMEM((2, 1, bq), F32),
            pltpu.VMEM((2, LANES, bq), F32),
            pltpu.VMEM((2, nb, bq), F32),
        ],
        compiler_params=pltpu.CompilerParams(
            dimension_semantics=("arbitrary", "arbitrary"), vmem_limit_bytes=VMEM_LIMIT),
        name="moba_prompt",
    )(z, kb, vt, kmean)


def _moba_sample_kernel(pt_ref, q_ref, kn_ref, vn_ref, *refs, ppg, past):
    k_refs = refs[:ppg]
    v_refs = refs[ppg:2 * ppg]
    o_ref = refs[2 * ppg]
    qs_sc, qf_sc, ksum_sc, m_sc, l_sc, o_sc = refs[2 * ppg + 1:]
    g = pl.program_id(1)
    ng = pl.num_programs(1)
    t = q_ref.shape[0]
    rows = H_A * t
    npb = o_sc.shape[0]
    ppb = MOBA_BLOCK // PAGE_SIZE
    row_i = lax.broadcasted_iota(jnp.int32, (rows, 1), 0)
    slope = jnp.exp2(-(lax.div(row_i, t) + 1).astype(F32))
    tok = lax.rem(row_i, t)
    blk_lane = lax.broadcasted_iota(jnp.int32, (1, LANES), 1)
    feat_head = lax.div(lax.broadcasted_iota(jnp.int32, (1, W_A), 1), DH_A)

    @pl.when(g == 0)
    def _():
        q = q_ref[...]
        qbd = jnp.concatenate([jnp.where(feat_head == h, q, 0.0) for h in range(H_A)], axis=0)
        qf_sc[...] = qbd
        qs_sc[...] = qbd * (DH_A ** -0.5)
        ksum_sc[...] = jnp.zeros(ksum_sc.shape, F32)
        m_sc[...] = jnp.zeros(m_sc.shape, F32)
        l_sc[...] = jnp.zeros(l_sc.shape, F32)

    qs = qs_sc[...]
    rk = lax.broadcasted_iota(jnp.int32, (1, MOBA_BLOCK), 1)
    ksum_all = ksum_sc[...]
    m_all = m_sc[...]
    l_all = l_sc[...]
    nblk = ppg // ppb
    s_blks = []
    for b in range(nblk):
        kt = jnp.concatenate([k_refs[b * ppb + p][...].reshape(W_A, PAGE_SIZE) for p in range(ppb)], axis=1)
        s_blks.append(_dot(qs, kt))
        ksum = jnp.sum(kt, axis=-1, keepdims=True)
        ksum_all = jnp.where(blk_lane == g * nblk + b, ksum, ksum_all)
    p_blks = []
    for b in range(nblk):
        n = g * nblk + b
        dist = ((past - n * MOBA_BLOCK) + tok - rk).astype(F32)
        logits = s_blks[b] - slope * dist
        m_b = jnp.max(logits, axis=-1, keepdims=True)
        p_ = jnp.exp(logits - m_b)
        p_blks.append(p_)
        m_all = jnp.where(blk_lane == n, m_b, m_all)
        l_all = jnp.where(blk_lane == n, jnp.sum(p_, axis=-1, keepdims=True), l_all)
    for b in range(nblk):
        vt = jnp.concatenate([v_refs[b * ppb + p][...].reshape(W_A, PAGE_SIZE) for p in range(ppb)], axis=1)
        o_sc[g * nblk + b] = _nt_dot(p_blks[b], vt)
    ksum_sc[...] = ksum_all
    m_sc[...] = m_all
    l_sc[...] = l_all

    @pl.when(g == ng - 1)
    def _():
        kmean = ksum_sc[...] * (1.0 / MOBA_BLOCK)
        scores = _dot(qf_sc[...], kmean, precision=HIGHEST)
        scores = jnp.where(blk_lane < npb, scores, -jnp.inf)
        sel, _ = _top_mask(scores, blk_lane.astype(F32), min(MOBA_TOPK, npb))
        kn = jnp.concatenate([kn_ref[h] for h in range(H_A)], axis=1)
        vn = jnp.concatenate([vn_ref[h] for h in range(H_A)], axis=1)
        s_own = _nt_dot(qs, kn)
        d_own = (tok - lax.broadcasted_iota(jnp.int32, (1, t), 1)).astype(F32)
        lg_own = jnp.where(d_own >= 0, s_own - slope * d_own, NEG_INF)
        m_all = m_sc[...]
        m_sel = jnp.max(jnp.where(sel, m_all, -jnp.inf), axis=-1, keepdims=True)
        m_fin = jnp.maximum(m_sel, jnp.max(lg_own, axis=-1, keepdims=True))
        w = jnp.where(sel, jnp.exp(m_all - m_fin), 0.0)
        p_own = jnp.exp(lg_own - m_fin)
        l_fin = jnp.sum(w * l_sc[...], axis=-1, keepdims=True) + jnp.sum(p_own, axis=-1, keepdims=True)
        o = _dot(p_own, vn)
        for n in range(npb):
            o = o + w[:, n:n + 1] * o_sc[n]
        o = o / l_fin
        out = jnp.zeros((t, W_A), F32)
        for h in range(H_A):
            out = out + jnp.where(feat_head == h, o[h * t:(h + 1) * t, :], 0.0)
        o_ref[...] = out


SAMPLE_PAGES_PER_STEP = 16


def _moba_sample(z, k_new, v_new, cache_k, cache_v, page_table):
    db, n_pages = page_table.shape
    t = z.shape[0] // db
    ppb = MOBA_BLOCK // PAGE_SIZE
    npb = n_pages // ppb
    assert n_pages == npb * ppb, "cached pages must fill whole MoBA blocks"
    ppg = SAMPLE_PAGES_PER_STEP if n_pages % SAMPLE_PAGES_PER_STEP == 0 else ppb
    ng = n_pages // ppg

    def page_map(b, g, pt, *, i):
        return (pt[b * n_pages + g * ppg + i], 0, 0, 0)

    page_specs = [pl.BlockSpec((None, H_A, DH_A, PAGE_SIZE), functools.partial(page_map, i=i))
                  for i in range(ppg)]
    grid_spec = pltpu.PrefetchScalarGridSpec(
        num_scalar_prefetch=1,
        grid=(db, ng),
        in_specs=[
            pl.BlockSpec((t, W_A), lambda b, g, pt: (b, Z_QA // W_A)),
            pl.BlockSpec((None, H_A, t, DH_A), lambda b, g, pt: (b, 0, 0, 0)),
            pl.BlockSpec((None, H_A, t, DH_A), lambda b, g, pt: (b, 0, 0, 0)),
        ] + page_specs + page_specs,
        out_specs=pl.BlockSpec((t, W_A), lambda b, g, pt: (b, 0)),
        scratch_shapes=[
            pltpu.VMEM((H_A * t, W_A), F32),
            pltpu.VMEM((H_A * t, W_A), F32),
            pltpu.VMEM((W_A, LANES), F32),
            pltpu.VMEM((H_A * t, LANES), F32),
            pltpu.VMEM((H_A * t, LANES), F32),
            pltpu.VMEM((npb, H_A * t, W_A), F32),
        ],
    )
    return pl.pallas_call(
        functools.partial(_moba_sample_kernel, ppg=ppg, past=n_pages * PAGE_SIZE),
        grid_spec=grid_spec,
        out_shape=jax.ShapeDtypeStruct((db * t, W_A), F32),
        compiler_params=pltpu.CompilerParams(
            dimension_semantics=("arbitrary", "arbitrary"), vmem_limit_bytes=VMEM_LIMIT),
        name="moba_sample",
    )(page_table.reshape(-1), z, k_new, v_new, *([cache_k] * ppg), *([cache_v] * ppg))


def _mlstm_kernel(q_ref, k_ref, v_ref, g_ref, c0_ref, n0_ref, m0_ref,
                  h_ref, c_ref, n_ref, m_ref, c_sc, n_sc, m_sc, *, lc, cps):
    step = pl.program_id(1)
    nsteps = pl.num_programs(1)
    lp = max(lc, MLSTM_CHUNK)

    @pl.when(step == 0)
    def _():
        c_sc[...] = c0_ref[0]
        n_sc[...] = n0_ref[0]
        m_sc[...] = jnp.broadcast_to(m0_ref[0], m_sc.shape)

    def padded(a):
        if a.shape[0] == lp:
            return a
        return jnp.concatenate([a, jnp.zeros((lp - a.shape[0],) + a.shape[1:], a.dtype)], axis=0)

    row = lax.broadcasted_iota(jnp.int32, (lp, lp), 0)
    col = lax.broadcasted_iota(jnp.int32, (lp, lp), 1)
    causal = row >= col
    tril = jnp.where(causal, 1.0, 0.0).astype(BF16)
    pick_8 = jnp.where(lax.broadcasted_iota(jnp.int32, (16, LANES), 0)
                       == lax.broadcasted_iota(jnp.int32, (16, LANES), 1), 1.0, 0.0).astype(BF16)
    gate_lane = lax.broadcasted_iota(jnp.int32, (1, LANES), 1)
    eye = jnp.where(lax.broadcasted_iota(jnp.int32, (DH_M, DH_M), 0)
                    == lax.broadcasted_iota(jnp.int32, (DH_M, DH_M), 1), 1.0, 0.0).astype(BF16)
    rvalid = lax.broadcasted_iota(jnp.int32, (lp, 1), 0) < lc
    cvalid = lax.broadcasted_iota(jnp.int32, (1, lp), 1) < lc
    m_all = m_sc[...]
    n_all = n_sc[...]
    c_all = [c_sc[h] for h in range(H_M)]
    pre = []
    for ci in range(cps):
        rs = slice(ci * lc, (ci + 1) * lc)
        q = padded(q_ref[rs, :])
        k = padded(k_ref[rs, :]) * (DH_M ** -0.5)
        v = padded(v_ref[rs, :])
        gts = padded(g_ref[rs, :])
        lf = jax.nn.log_sigmoid(gts)
        bcum = sum(_dot(tril, part) for part in _split3(lf))
        comb_t = sum(_nt_dot(pick_8, part) for part in _split3(jnp.where(gate_lane < H_M, gts, bcum)))
        heads = []
        for h in range(H_M):
            qh = q[:, h * DH_M:(h + 1) * DH_M]
            kh = k[:, h * DH_M:(h + 1) * DH_M]
            vh = v[:, h * DH_M:(h + 1) * DH_M]
            bc = bcum[:, H_M + h:H_M + h + 1]
            ic = gts[:, h:h + 1]
            br = comb_t[H_M + h:H_M + h + 1, :]
            ir = comb_t[h:h + 1, :]
            dmat = jnp.where(causal, bc - br + ir, -jnp.inf)
            qhb = qh.astype(BF16)
            wide = lambda col: jnp.broadcast_to(col, (lp, LANES))
            heads.append(dict(
                qh=qh, qhb=qhb, kh=kh, vhb=vh.astype(BF16), br=br, ir=ir, dmat=dmat,
                bc_w=wide(bc), ic_w=wide(ic),
                m_intra_w=wide(jnp.max(dmat, axis=-1, keepdims=True)),
                qk=_nt_dot(qhb, kh.astype(BF16)),
                kh_t=sum(_nt_dot(eye, part) for part in _split3(kh)),
            ))
        pre.append(heads)
    for ci in range(cps):
        rs = slice(ci * lc, (ci + 1) * lc)
        m_out, n_out, c_out = [], [], []
        for h in range(H_M):
            d = pre[ci][h]
            bc_w, ic_w, dmat = d["bc_w"], d["ic_w"], d["dmat"]
            m_prev = m_all[h:h + 1, :]
            c_prev = c_all[h]
            n_prev = n_all[h:h + 1, :]
            a_inter = bc_w + m_prev
            m_t = jnp.maximum(a_inter, d["m_intra_w"])
            w_inter = jnp.exp(a_inter - m_t)
            s = d["qk"] * jnp.exp(dmat - m_t[:, :lp])
            num = w_inter * _dot(d["qhb"], c_prev.astype(BF16)) + _dot(s.astype(BF16), d["vhb"])
            den = (w_inter[:, :1] * jnp.sum(d["qh"] * n_prev, axis=-1, keepdims=True)
                   + jnp.sum(s, axis=-1, keepdims=True))
            h_out = num / jnp.maximum(jnp.abs(den), jnp.exp(-m_t[:, :1]))
            h_ref[rs, h * DH_M:(h + 1) * DH_M] = h_out[:lc]
            m_new = m_t[lc - 1:lc, :]
            b_last = bc_w[lc - 1:lc, :]
            g_c = jnp.exp(b_last + m_prev - m_new)
            g_s = jnp.where(rvalid, jnp.exp(b_last - bc_w + ic_w - m_new), 0.0)
            g_s_row = jnp.where(cvalid, jnp.exp(b_last[:, :lp] - d["br"] + d["ir"] - m_new[:, :lp]), 0.0)
            kg_t = (d["kh_t"] * g_s_row).astype(BF16)
            c_out.append(g_c * c_prev + _dot(kg_t, d["vhb"]))
            n_out.append(g_c * n_prev + jnp.sum(g_s * d["kh"], axis=0, keepdims=True))
            m_out.append(m_new)
        m_all = jnp.concatenate(m_out, axis=0)
        n_all = jnp.concatenate(n_out, axis=0)
        c_all = c_out
    for h in range(H_M):
        c_sc[h] = c_all[h]
    n_sc[...] = n_all
    m_sc[...] = m_all

    @pl.when(step == nsteps - 1)
    def _():
        c_ref[0] = c_sc[...]
        n_ref[0] = n_sc[...]
        m_ref[0] = m_sc[:, 0:1]


def _mlstm(z, gif, c0, n0, m0, *, lc, cps):
    b = c0.shape[0]
    t = z.shape[0] // b
    rows = lc * cps
    nc = t // rows
    assert nc * rows == t
    row = lambda bi, ci: bi * nc + ci
    wcol = lambda off: off // W_M
    return pl.pallas_call(
        functools.partial(_mlstm_kernel, lc=lc, cps=cps),
        grid=(b, nc),
        in_specs=[
            pl.BlockSpec((rows, W_M), lambda bi, ci: (row(bi, ci), wcol(Z_QM))),
            pl.BlockSpec((rows, W_M), lambda bi, ci: (row(bi, ci), wcol(Z_KM))),
            pl.BlockSpec((rows, W_M), lambda bi, ci: (row(bi, ci), wcol(Z_VM))),
            pl.BlockSpec((rows, LANES), lambda bi, ci: (row(bi, ci), 0)),
            pl.BlockSpec((1, H_M, DH_M, DH_M), lambda bi, ci: (bi, 0, 0, 0)),
            pl.BlockSpec((1, H_M, DH_M), lambda bi, ci: (bi, 0, 0)),
            pl.BlockSpec((1, H_M, 1), lambda bi, ci: (bi, 0, 0)),
        ],
        out_specs=[
            pl.BlockSpec((rows, W_M), lambda bi, ci: (row(bi, ci), 0)),
            pl.BlockSpec((1, H_M, DH_M, DH_M), lambda bi, ci: (bi, 0, 0, 0)),
            pl.BlockSpec((1, H_M, DH_M), lambda bi, ci: (bi, 0, 0)),
            pl.BlockSpec((1, H_M, 1), lambda bi, ci: (bi, 0, 0)),
        ],
        out_shape=[
            jax.ShapeDtypeStruct((b * t, W_M), F32),
            jax.ShapeDtypeStruct((b, H_M, DH_M, DH_M), F32),
            jax.ShapeDtypeStruct((b, H_M, DH_M), F32),
            jax.ShapeDtypeStruct((b, H_M, 1), F32),
        ],
        scratch_shapes=[
            pltpu.VMEM((H_M, DH_M, DH_M), F32),
            pltpu.VMEM((H_M, DH_M), F32),
            pltpu.VMEM((H_M, LANES), F32),
        ],
        compiler_params=pltpu.CompilerParams(
            dimension_semantics=("arbitrary", "arbitrary"), vmem_limit_bytes=VMEM_LIMIT),
        name="mlstm",
    )(z, z, z, gif, c0, n0, m0[..., None])


def _layernorm(y, g, b):
    mu = jnp.mean(y, axis=-1, keepdims=True)
    var = jnp.mean(jnp.square(y - mu), axis=-1, keepdims=True)
    return (y - mu) * lax.rsqrt(var + LN_EPS) * g + b


def _finish_kernel(x_ref, a_ref, mh_ref, om_ref, ga_ref, gm_ref, wa_ref, wm_ref, wo_ref,
                   g1_ref, b1_ref, wr_ref, br_ref,
                   h_ref, hb_ref, gates_ref, gates_t_ref, cnt_ref):
    a = _dot(a_ref[...].astype(BF16), wa_ref[...])
    mo = mh_ref[...] * jax.nn.sigmoid(om_ref[...])
    m = _dot(mo.astype(BF16), wm_ref[...])
    merged = jax.nn.sigmoid(ga_ref[...]) * a + jax.nn.sigmoid(gm_ref[...]) * m
    y = DN_ALPHA * x_ref[...] + _dot(merged.astype(BF16), wo_ref[...])
    h = _layernorm(y, g1_ref[...], b1_ref[...])
    h_ref[...] = h
    hb_ref[...] = h.astype(BF16)
    logits = _dot(h, wr_ref[...], precision=HIGHEST) + br_ref[...]
    lane_f = lax.broadcasted_iota(jnp.int32, (1, LANES), 1).astype(F32)
    sel, top = _top_mask(logits, lane_f, TOP_K)
    e = jnp.where(sel, jnp.exp(logits - top), 0.0)
    gates = e / jnp.sum(e, axis=-1, keepdims=True)
    gates_ref[...] = gates
    gates_t_ref[...] = gates.T
    cnt_ref[0] = jnp.sum(jnp.where(gates > 0.0, 1.0, 0.0), axis=0, keepdims=True).astype(jnp.int32)


def _finish(x2d, a, mh, z, wa, wm, wo, g1, b1, wr, br, *, tm):
    t = x2d.shape[0]
    nt = t // tm
    const = lambda i: (0, 0)
    return pl.pallas_call(
        _finish_kernel,
        grid=(nt,),
        in_specs=[
            pl.BlockSpec((tm, D_MODEL), lambda i: (i, 0)),
            pl.BlockSpec((tm, W_A), lambda i: (i, 0)),
            pl.BlockSpec((tm, W_M), lambda i: (i, 0)),
            pl.BlockSpec((tm, W_M), lambda i: (i, Z_OM // W_M)),
            pl.BlockSpec((tm, D_MODEL), lambda i: (i, Z_GA // D_MODEL)),
            pl.BlockSpec((tm, D_MODEL), lambda i: (i, Z_GM // D_MODEL)),
            pl.BlockSpec(wa.shape, const),
            pl.BlockSpec(wm.shape, const),
            pl.BlockSpec(wo.shape, const),
            pl.BlockSpec(g1.shape, const),
            pl.BlockSpec(b1.shape, const),
            pl.BlockSpec(wr.shape, const),
            pl.BlockSpec(br.shape, const),
        ],
        out_specs=[
            pl.BlockSpec((tm, D_MODEL), lambda i: (i, 0)),
            pl.BlockSpec((tm, D_MODEL), lambda i: (i, 0)),
            pl.BlockSpec((tm, LANES), lambda i: (i, 0)),
            pl.BlockSpec((LANES, tm), lambda i: (0, i)),
            pl.BlockSpec((1, 1, LANES), lambda i: (i, 0, 0)),
        ],
        out_shape=[
            jax.ShapeDtypeStruct((t, D_MODEL), F32),
            jax.ShapeDtypeStruct((t, D_MODEL), BF16),
            jax.ShapeDtypeStruct((t, LANES), F32),
            jax.ShapeDtypeStruct((LANES, t), F32),
            jax.ShapeDtypeStruct((nt, 1, LANES), jnp.int32),
        ],
        compiler_params=pltpu.CompilerParams(
            dimension_semantics=("arbitrary",), vmem_limit_bytes=VMEM_LIMIT),
        name="finish",
    )(x2d, a, mh, z, z, z, wa, wm, wo, g1, b1, wr, br)


MLSTM_CHUNKS_PER_STEP = 4
MOE_TILE = 2048
MOE_SEG = 512
MOE_SEG_ROWS = 80
MOE_RANK_BLOCK = 256


def _moe_kernel(cnt_ref, hb_ref, g_ref, gt_ref, wg_ref, bg_ref, wu_ref, bu_ref, wd_ref, bd_ref,
                y_ref, posc_sc, selc_sc, posr_sc, *, cnt_per_seg, seg):
    i = pl.program_id(0)
    e = pl.program_id(1)
    tt = hb_ref.shape[0]
    nseg = tt // seg
    rseg = MOE_SEG_ROWS
    blk = MOE_RANK_BLOCK

    @pl.when(e == 0)
    def _():
        y_ref[...] = jnp.zeros(y_ref.shape, F32)

    cnt_max = 0
    for sg in range(nseg):
        first = (i * nseg + sg) * cnt_per_seg
        cnt_seg = cnt_ref[first, e]
        for r in range(1, cnt_per_seg):
            cnt_seg = cnt_seg + cnt_ref[first + r, e]
        cnt_max = jnp.maximum(cnt_max, cnt_seg)

    @pl.when(cnt_max > 0)
    def _():
        lane = lax.broadcasted_iota(jnp.int32, (1, LANES), 1)
        gcol = jnp.sum(jnp.where(lane == e, g_ref[...], 0.0), axis=-1, keepdims=True)
        selc = jnp.broadcast_to(jnp.where(gcol > 0.0, 1.0, 0.0), (tt, LANES))
        selc_sc[...] = selc
        rr = lax.broadcasted_iota(jnp.int32, (blk, blk), 0)
        cc = lax.broadcasted_iota(jnp.int32, (blk, blk), 1)
        strict_lower = jnp.where(rr > cc, 1.0, 0.0).astype(BF16)
        for bi in range(tt // blk):
            if (bi * blk) % seg == 0:
                offs = jnp.zeros((1, LANES), F32)
            sb = selc[bi * blk:(bi + 1) * blk]
            posc_sc[bi * blk:(bi + 1) * blk, :] = _dot(strict_lower, sb.astype(BF16)) + offs
            offs = offs + jnp.sum(sb, axis=0, keepdims=True)
        grow = gt_ref[pl.ds(lax.rem(e, 8), 1), :]
        selr = jnp.where(grow > 0.0, 1.0, 0.0)
        ru = lax.broadcasted_iota(jnp.int32, (LANES, LANES), 0)
        cu = lax.broadcasted_iota(jnp.int32, (LANES, LANES), 1)
        strict_upper = jnp.where(ru < cu, 1.0, 0.0).astype(BF16)
        for ci in range(tt // LANES):
            if (ci * LANES) % seg == 0:
                offr = jnp.zeros((16, 1), F32)
            sr = jnp.broadcast_to(selr[:, ci * LANES:(ci + 1) * LANES], (16, LANES))
            posr_sc[:, ci * LANES:(ci + 1) * LANES] = _dot(sr.astype(BF16), strict_upper) + offr
            offr = offr + jnp.sum(sr, axis=-1, keepdims=True)
        slot_sub = lax.broadcasted_iota(jnp.int32, (rseg, LANES), 0).astype(F32)
        slot_lane = lax.broadcasted_iota(jnp.int32, (1, LANES), 1).astype(F32)

        def one_pass(s, carry):
            base = (s * rseg).astype(F32)
            xs_parts, gs_parts = [], []
            for sg in range(nseg):
                pm_chunks = []
                for ci in range(sg * seg // LANES, (sg + 1) * seg // LANES):
                    pr = posr_sc[0:1, ci * LANES:(ci + 1) * LANES] - base
                    sr = selr[:, ci * LANES:(ci + 1) * LANES]
                    pm_chunks.append(jnp.logical_and(pr == slot_sub, sr > 0.0))
                pmask = jnp.concatenate(pm_chunks, axis=1)
                gs_parts.append(jnp.sum(jnp.where(pmask, grow[:, sg * seg:(sg + 1) * seg], 0.0),
                                        axis=-1, keepdims=True))
                xs_parts.append(_dot(jnp.where(pmask, 1.0, 0.0).astype(BF16),
                                     hb_ref[sg * seg:(sg + 1) * seg, :]).astype(BF16))
            xs = jnp.concatenate(xs_parts, axis=0)
            gs = jnp.concatenate(gs_parts, axis=0)
            gp = jnp.minimum(_dot(xs, wg_ref[0]) + bg_ref[0], SWIGLU_LIMIT)
            up = jnp.clip(_dot(xs, wu_ref[0]) + bu_ref[0], -SWIGLU_LIMIT, SWIGLU_LIMIT)
            hdn = (up + 1.0) * (gp * jax.nn.sigmoid(SWIGLU_ALPHA * gp))
            yb = ((_dot(hdn.astype(BF16), wd_ref[0]) + bd_ref[0]) * gs).astype(BF16)
            pad = jnp.zeros((LANES - rseg, yb.shape[1]), BF16)
            for sg in range(nseg):
                rows = slice(sg * seg, (sg + 1) * seg)
                hit = jnp.logical_and(posc_sc[rows, :] - base == slot_lane, slot_lane < float(rseg))
                pt = jnp.where(jnp.logical_and(hit, selc_sc[rows, :] > 0.0), 1.0, 0.0).astype(BF16)
                y_seg = jnp.concatenate([yb[sg * rseg:(sg + 1) * rseg], pad], axis=0)
                y_ref[rows, :] += _dot(pt, y_seg)
            return carry

        lax.fori_loop(0, lax.div(cnt_max + (rseg - 1), rseg), one_pass, 0)


def _moe(hb, gates, gates_t, cnt, wg, bg, wu, bu, wd, bd, *, tt):
    t = hb.shape[0]
    nt = t // tt
    seg = min(MOE_SEG, tt)
    cnt_per_seg = cnt.shape[0] * seg // t
    assert cnt_per_seg * t == cnt.shape[0] * seg and tt % seg == 0 and seg % MOE_RANK_BLOCK == 0
    wspec = pl.BlockSpec((1, D_MODEL, D_MODEL), lambda i, e, c: (e, 0, 0))
    bspec = pl.BlockSpec((1, 1, D_MODEL), lambda i, e, c: (e, 0, 0))
    grid_spec = pltpu.PrefetchScalarGridSpec(
        num_scalar_prefetch=1,
        grid=(nt, N_EXPERTS),
        in_specs=[
            pl.BlockSpec((tt, D_MODEL), lambda i, e, c: (i, 0)),
            pl.BlockSpec((tt, LANES), lambda i, e, c: (i, 0)),
            pl.BlockSpec((8, tt), lambda i, e, c: (e // 8, i)),
            wspec, bspec, wspec, bspec, wspec, bspec,
        ],
        out_specs=pl.BlockSpec((tt, D_MODEL), lambda i, e, c: (i, 0)),
        scratch_shapes=[
            pltpu.VMEM((tt, LANES), F32),
            pltpu.VMEM((tt, LANES), F32),
            pltpu.VMEM((16, tt), F32),
        ],
    )
    return pl.pallas_call(
        functools.partial(_moe_kernel, cnt_per_seg=cnt_per_seg, seg=seg),
        grid_spec=grid_spec,
        out_shape=jax.ShapeDtypeStruct((t, D_MODEL), F32),
        compiler_params=pltpu.CompilerParams(
            dimension_semantics=("arbitrary", "arbitrary"), vmem_limit_bytes=VMEM_LIMIT),
        name="moe",
    )(cnt, hb, gates, gates_t, wg, bg, wu, bu, wd, bd)


def _ln2_kernel(h_ref, y_ref, g_ref, b_ref, o_ref):
    o_ref[...] = _layernorm(DN_ALPHA * h_ref[...] + y_ref[...], g_ref[...], b_ref[...])


def _ln2(h, y, g, b, *, tm):
    t = h.shape[0]
    spec = pl.BlockSpec((tm, D_MODEL), lambda i: (i, 0))
    const = pl.BlockSpec((1, D_MODEL), lambda i: (0, 0))
    return pl.pallas_call(
        _ln2_kernel,
        grid=(t // tm,),
        in_specs=[spec, spec, const, const],
        out_specs=spec,
        out_shape=jax.ShapeDtypeStruct((t, D_MODEL), F32),
        compiler_params=pltpu.CompilerParams(dimension_semantics=("arbitrary",)),
        name="ln2",
    )(h, y, g, b)


def _group(x2d, weights, *, ps, tm, moe_tile, attn, mlstm_state, lc):
    z, kb, vt, kpages, vpages, gif, kmean = _project(
        x2d, weights["wz"], weights["wkv"], weights["wif"], weights["bif"], ps=ps, tm=tm)
    a = attn(z, kb, vt, kmean.reshape(kmean.shape[0], W_A), kpages, vpages)
    t_seq = x2d.shape[0] // mlstm_state[0].shape[0]
    cps = MLSTM_CHUNKS_PER_STEP if (t_seq // lc) % MLSTM_CHUNKS_PER_STEP == 0 else 1
    mh, c_t, n_t, m_t = _mlstm(z, gif, *mlstm_state, lc=lc, cps=cps)
    h, hb, gates, gates_t, cnt = _finish(
        x2d, a, mh, z, weights["wa"], weights["wm"], weights["wo"], weights["g1"], weights["b1"],
        weights["wr"], weights["br"], tm=tm)
    y = _moe(hb, gates, gates_t, cnt.reshape(cnt.shape[0], LANES),
             weights["wg"], weights["bg"], weights["wu"], weights["bu"], weights["wd"], weights["bd"],
             tt=moe_tile)
    out = _ln2(h, y, weights["g2"], weights["b2"], tm=tm)
    return out, kpages, vpages, c_t, n_t, m_t[..., 0]


def kernel(x_prompt, x_sample, cache_k, cache_v, page_table, state_C, state_n, state_m,
           w_in, b_if, w_branch_a, w_branch_m, w_out, ln1_g, ln1_b,
           w_router, b_router, w_gate, b_gate, w_up, b_up, w_down, b_down, ln2_g, ln2_b):
    depth = w_in.shape[0]
    assert depth == 1, "single-layer step"
    bp, seq, _ = x_prompt.shape
    db, dseq, _ = x_sample.shape
    l = 0
    w = w_in[l]
    o_qa, o_ka, o_va, o_qm, o_om_end = 0, W_A, 2 * W_A, 3 * W_A, 3 * W_A + 4 * W_M
    o_if = o_om_end
    o_ga = o_if + 2 * H_M
    weights = {
        "wz": jnp.concatenate([w[:, o_ga:], w[:, o_qa:o_ka], w[:, o_qm:o_om_end]], axis=1).astype(BF16),
        "wkv": w[:, o_ka:o_qm].astype(BF16),
        "wif": jnp.pad(w[:, o_if:o_ga], ((0, 0), (0, LANES - 2 * H_M))),
        "bif": jnp.pad(b_if[l], (0, LANES - 2 * H_M))[None, :],
        "wa": w_branch_a[l].astype(BF16),
        "wm": w_branch_m[l].astype(BF16),
        "wo": w_out[l].astype(BF16),
        "g1": ln1_g[l][None, :], "b1": ln1_b[l][None, :],
        "wr": jnp.pad(w_router[l], ((0, 0), (0, LANES - N_EXPERTS))),
        "br": jnp.pad(b_router[l], (0, LANES - N_EXPERTS), constant_values=NEG_INF)[None, :],
        "wg": w_gate[l].astype(BF16), "bg": b_gate[l][:, None, :],
        "wu": w_up[l].astype(BF16), "bu": b_up[l][:, None, :],
        "wd": w_down[l].astype(BF16), "bd": b_down[l][:, None, :],
        "g2": ln2_g[l][None, :], "b2": ln2_b[l][None, :],
    }
    assert bp == 1, "prompt batch of one sequence"
    zero_state = (jnp.zeros((bp, H_M, DH_M, DH_M), F32), jnp.zeros((bp, H_M, DH_M), F32),
                  jnp.zeros((bp, H_M), F32))
    yp, kp, vp, cp, np_, mp = _group(
        x_prompt.reshape(bp * seq, D_MODEL), weights, ps=PAGE_SIZE, tm=256,
        moe_tile=min(MOE_TILE, bp * seq),
        attn=lambda z, kb, vt, kmean, kpg, vpg: _moba_prompt(z, kb, vt, kmean),
        mlstm_state=zero_state, lc=math.gcd(seq, MLSTM_CHUNK))
    ck = jnp.swapaxes(cache_k.reshape(cache_k.shape[1:]), -1, -2)
    cv = jnp.swapaxes(cache_v.reshape(cache_v.shape[1:]), -1, -2)
    ys, ks, vs, cs, ns, ms = _group(
        x_sample.reshape(db * dseq, D_MODEL), weights, ps=dseq, tm=256,
        moe_tile=min(MOE_TILE, db * dseq),
        attn=lambda z, kb, vt, kmean, kpg, vpg: _moba_sample(z, kpg, vpg, ck, cv, page_table),
        mlstm_state=(state_C[l], state_n[l], state_m[l]), lc=math.gcd(dseq, MLSTM_CHUNK))
    n_pg = seq // PAGE_SIZE
    return (yp.reshape(bp, seq, D_MODEL), ys.reshape(db, dseq, D_MODEL),
            kp.reshape(1, bp, n_pg, H_A, PAGE_SIZE, DH_A), vp.reshape(1, bp, n_pg, H_A, PAGE_SIZE, DH_A),
            ks.reshape(1, db, H_A, dseq, DH_A), vs.reshape(1, db, H_A, dseq, DH_A),
            cp[None], np_[None], mp[None], cs[None], ns[None], ms[None])
```

```python
import functools
import math

import jax
import jax.numpy as jnp
from jax import lax
from jax.experimental import pallas as pl
from jax.experimental.pallas import tpu as pltpu

F32 = jnp.float32
BF16 = jnp.bfloat16
HIGHEST = lax.Precision.HIGHEST

D_MODEL = 1024
PAGE_SIZE = 128
H_A = 8
DH_A = 64
W_A = H_A * DH_A
MOBA_BLOCK = 256
MOBA_TOPK = 3
H_M = 4
DH_M = 128
W_M = H_M * DH_M
MLSTM_CHUNK = 64
N_EXPERTS = 32
TOP_K = 4
SWIGLU_LIMIT = 7.0
SWIGLU_ALPHA = 1.702
DN_ALPHA = 2.0 ** 0.25
LN_EPS = 1e-5
NEG_INF = -1e30

LANES = 128
Z_GA, Z_GM, Z_QA, Z_QM, Z_KM, Z_VM, Z_OM = 0, 1024, 2048, 2560, 3072, 3584, 4096
Z_WIDTH = 4608
VT_ROWS = 80
VMEM_LIMIT = 56 * 1024 * 1024


def _nt_dot(a, b, precision=None):
    return lax.dot_general(a, b, (((1,), (1,)), ((), ())), precision=precision,
                           preferred_element_type=F32)


def _dot(a, b, precision=None):
    return jnp.dot(a, b, precision=precision, preferred_element_type=F32)


def _top_mask(scores, idx_f, k, axis=-1):
    n = scores.shape[axis]
    sel = jnp.zeros(scores.shape, jnp.bool_)
    work = scores
    first_max = None
    for _ in range(k):
        mx = jnp.max(work, axis=axis, keepdims=True)
        if first_max is None:
            first_max = mx
        cand = work == mx
        pick_idx = jnp.min(jnp.where(cand, idx_f, float(n)), axis=axis, keepdims=True)
        pick = idx_f == pick_idx
        sel = jnp.logical_or(sel, pick)
        work = jnp.where(pick, -jnp.inf, work)
    return sel, first_max


def _proj_kernel(x_ref, wz_ref, wkv_ref, wif_ref, bif_ref,
                 z_ref, kb_ref, vt_ref, kp_ref, vp_ref, gif_ref, kmean_ref, *, ps):
    x = x_ref[...]
    tm = x.shape[0]
    xb = x.astype(BF16)
    x_lo = (x - xb.astype(F32)).astype(BF16)
    w_hi = wif_ref[:, :LANES]
    gif_ref[...] = (_dot(xb, w_hi) + _dot(xb, wif_ref[:, LANES:]) + _dot(x_lo, w_hi)) + bif_ref[...]
    for c in range(Z_WIDTH // 512):
        z_ref[:, c * 512:(c + 1) * 512] = _dot(xb, wz_ref[:, c * 512:(c + 1) * 512])
    for c, pref in ((0, kp_ref), (1, vp_ref)):
        r = _dot(xb, wkv_ref[:, c * W_A:(c + 1) * W_A])
        if c == 0:
            kb_ref[...] = r.astype(BF16)
        else:
            rt = r.T
            tail = (lax.broadcasted_iota(jnp.int32, (VT_ROWS - DH_A, tm), 0) == 0).astype(F32)
            for h in range(H_A):
                vt_ref[0, h] = jnp.concatenate([rt[h * DH_A:(h + 1) * DH_A], tail], axis=0).astype(BF16)
        for h in range(H_A):
            pref[:, h, :, :] = r[:, h * DH_A:(h + 1) * DH_A].reshape(tm // ps, ps, DH_A)
        if c == 0:
            kmean_ref[...] = jnp.mean(r.reshape(tm // MOBA_BLOCK, MOBA_BLOCK, W_A), axis=1, keepdims=True)


def _project(x2d, wz, wkv, wif, bif, *, ps, tm):
    t = x2d.shape[0]
    assert tm == MOBA_BLOCK, "one MoBA block of keys per row tile"
    grid = (t // tm,)
    const = lambda i: (0, 0)
    return pl.pallas_call(
        functools.partial(_proj_kernel, ps=ps),
        grid=grid,
        in_specs=[
            pl.BlockSpec((tm, D_MODEL), lambda i: (i, 0)),
            pl.BlockSpec(wz.shape, const),
            pl.BlockSpec(wkv.shape, const),
            pl.BlockSpec(wif.shape, const),
            pl.BlockSpec(bif.shape, const),
        ],
        out_specs=[
            pl.BlockSpec((tm, Z_WIDTH), lambda i: (i, 0)),
            pl.BlockSpec((tm, W_A), lambda i: (i, 0)),
            pl.BlockSpec((1, H_A, VT_ROWS, tm), lambda i: (i, 0, 0, 0)),
            pl.BlockSpec((tm // ps, H_A, ps, DH_A), lambda i: (i, 0, 0, 0)),
            pl.BlockSpec((tm // ps, H_A, ps, DH_A), lambda i: (i, 0, 0, 0)),
            pl.BlockSpec((tm, LANES), lambda i: (i, 0)),
            pl.BlockSpec((tm // MOBA_BLOCK, 1, W_A), lambda i: (i, 0, 0)),
        ],
        out_shape=[
            jax.ShapeDtypeStruct((t, Z_WIDTH), F32),
            jax.ShapeDtypeStruct((t, W_A), BF16),
            jax.ShapeDtypeStruct((t // tm, H_A, VT_ROWS, tm), BF16),
            jax.ShapeDtypeStruct((t // ps, H_A, ps, DH_A), F32),
            jax.ShapeDtypeStruct((t // ps, H_A, ps, DH_A), F32),
            jax.ShapeDtypeStruct((t, LANES), F32),
            jax.ShapeDtypeStruct((t // MOBA_BLOCK, 1, W_A), F32),
        ],
        compiler_params=pltpu.CompilerParams(
            dimension_semantics=("arbitrary",), vmem_limit_bytes=VMEM_LIMIT),
        name="proj",
    )(x2d, wz, wkv, wif, bif)


PAST_UNROLL = 16


LOG2E = 1.4426950408889634


def _split3(x):
    hi = x.astype(BF16)
    r1 = x - hi.astype(F32)
    mid = r1.astype(BF16)
    lo = (r1 - mid.astype(F32)).astype(BF16)
    return hi, mid, lo


def _moba_prompt_kernel(q_ref, k_ref, vt_ref, km_ref, o_ref, m_sc, acc_sc, sel_sc):
    hp = pl.program_id(0)
    j = pl.program_id(1)
    bq = MOBA_BLOCK
    nb = km_ref.shape[0]
    q = q_ref[...]
    lane = lax.broadcasted_iota(jnp.int32, (1, LANES), 1)
    blk_f = lax.broadcasted_iota(jnp.int32, (nb, 1), 0).astype(F32)
    jf = j.astype(F32)
    rk = lax.broadcasted_iota(jnp.int32, (bq, bq), 0)
    rq = lax.broadcasted_iota(jnp.int32, (bq, bq), 1)
    causal = rq >= rk
    pos = lax.broadcasted_iota(jnp.int32, (bq, 1), 0).astype(F32)
    k_extra = jnp.where(lane < 3, pos, jnp.where(lane < 6, 1.0, 0.0)).astype(BF16)
    k_own = jnp.concatenate([k_ref[pl.ds(pl.multiple_of(j * bq, bq), bq), :], k_extra], axis=1)
    vt_own = vt_ref[j]
    q_augs = []
    coefs = []
    for hh in range(2):
        hmask = jnp.logical_and(lane >= hh * DH_A, lane < (hh + 1) * DH_A)
        qh = jnp.where(hmask, q, 0.0)
        head_f = (2 * hp + hh + 1).astype(F32)
        coef = jnp.exp2(jnp.zeros((1, 1), F32) - head_f) * LOG2E
        coefs.append(coef)
        scores = _nt_dot(km_ref[...], qh, precision=HIGHEST)
        scores = jnp.where(blk_f < jf, scores, NEG_INF)
        sel, _ = _top_mask(scores, blk_f, min(MOBA_TOPK, nb), axis=0)
        sel = jnp.logical_and(sel, blk_f < jf)
        sel_sc[hh] = jnp.where(sel, 1.0, 0.0)
        c3 = _split3(jnp.broadcast_to(coef, (bq, 1)))
        q3 = _split3(-(coef * pos))
        q_extra = jnp.zeros((bq, LANES), F32)
        for li, term in enumerate(c3 + q3):
            q_extra = jnp.where(lane == li, term.astype(F32), q_extra)
        qb = (qh * (DH_A ** -0.5 * LOG2E)).astype(BF16)
        q_aug = jnp.concatenate([qb, q_extra.astype(BF16)], axis=1)
        q_augs.append(q_aug)
        logits = jnp.where(causal, _nt_dot(k_own, q_aug), NEG_INF)
        m = jnp.max(logits, axis=0, keepdims=True)
        p = jnp.exp2(logits - m)
        m_sc[hh] = m
        acc_sc[hh] = _dot(vt_own[hh], p.astype(BF16))

    def body(it, carry):
        ns, kns, vtns, offs, valids = [], [], [], [], []
        for u in range(PAST_UNROLL):
            n_raw = it * PAST_UNROLL + u
            n = jnp.minimum(n_raw, j - 1)
            ns.append(n)
            valids.append(n_raw < j)
            kns.append(jnp.concatenate([k_ref[pl.ds(pl.multiple_of(n * bq, bq), bq), :], k_extra], axis=1))
            vtns.append(vt_ref[n])
            offs.append((j - n).astype(F32) * float(bq))
        logits = [[], []]
        for hh in range(2):
            for u in range(PAST_UNROLL):
                picked = sel_sc[hh, pl.ds(ns[u], 1), :]
                keep = jnp.logical_and(picked > 0.5, valids[u])
                rb = jnp.where(keep, -(coefs[hh] * offs[u]), NEG_INF)
                logits[hh].append(_nt_dot(kns[u], q_augs[hh]) + rb)
        m_news, acc_news = [], []
        for hh in range(2):
            m_old = m_sc[hh]
            m_new = m_old
            for lg in logits[hh]:
                m_new = jnp.maximum(m_new, jnp.max(lg, axis=0, keepdims=True))
            m_news.append(m_new)
            acc_news.append(jnp.exp2(m_old - m_new) * acc_sc[hh])
        for hh in range(2):
            acc_new = acc_news[hh]
            for u in range(PAST_UNROLL):
                p = jnp.exp2(logits[hh][u] - m_news[hh])
                acc_new = acc_new + _dot(vtns[u][hh], p.astype(BF16))
            acc_sc[hh] = acc_new
            m_sc[hh] = m_news[hh]
        return carry

    lax.fori_loop(0, lax.div(j + (PAST_UNROLL - 1), PAST_UNROLL), body, 0)
    acc0 = acc_sc[0]
    acc1 = acc_sc[1]
    o0 = acc0[:DH_A] / acc0[DH_A:DH_A + 1, :]
    o1 = acc1[:DH_A] / acc1[DH_A:DH_A + 1, :]
    o_ref[...] = jnp.concatenate([o0, o1], axis=0).T


def _moba_prompt(z, kb, vt, kmean):
    t = z.shape[0]
    nb = t // MOBA_BLOCK
    bq = MOBA_BLOCK
    qcol = Z_QA // LANES
    return pl.pallas_call(
        _moba_prompt_kernel,
        grid=(H_A // 2, nb),
        in_specs=[
            pl.BlockSpec((bq, LANES), lambda hp, j: (j, qcol + hp)),
            pl.BlockSpec((t, LANES), lambda hp, j: (0, hp)),
            pl.BlockSpec((nb, 2, VT_ROWS, bq), lambda hp, j: (0, hp, 0, 0)),
            pl.BlockSpec((nb, LANES), lambda hp, j: (0, hp)),
        ],
        out_specs=pl.BlockSpec((bq, LANES), lambda hp, j: (j, hp)),
        out_shape=jax.ShapeDtypeStruct((t, W_A), F32),
        scratch_shapes=[
            pltpu.VMEM((2, 1, bq), F32),
            pltpu.VMEM((2, VT_ROWS, bq), F32),
            pltpu.VMEM((2, nb, bq), F32),
        ],
        compiler_params=pltpu.CompilerParams(
            dimension_semantics=("arbitrary", "arbitrary"), vmem_limit_bytes=VMEM_LIMIT),
        name="moba_prompt",
    )(z, kb, vt, kmean)


def _moba_sample_kernel(pt_ref, q_ref, kn_ref, vn_ref, *refs, ppg, past):
    k_refs = refs[:ppg]
    v_refs = refs[ppg:2 * ppg]
    o_ref = refs[2 * ppg]
    qs_sc, qf_sc, ksum_sc, m_sc, l_sc, o_sc = refs[2 * ppg + 1:]
    g = pl.program_id(1)
    ng = pl.num_programs(1)
    t = q_ref.shape[0]
    rows = H_A * t
    npb = o_sc.shape[0]
    ppb = MOBA_BLOCK // PAGE_SIZE
    row_i = lax.broadcasted_iota(jnp.int32, (rows, 1), 0)
    slope = jnp.exp2(-(lax.div(row_i, t) + 1).astype(F32))
    tok = lax.rem(row_i, t)
    blk_lane = lax.broadcasted_iota(jnp.int32, (1, LANES), 1)
    feat_head = lax.div(lax.broadcasted_iota(jnp.int32, (1, W_A), 1), DH_A)

    @pl.when(g == 0)
    def _():
        q = q_ref[...]
        qbd = jnp.concatenate([jnp.where(feat_head == h, q, 0.0) for h in range(H_A)], axis=0)
        qf_sc[...] = qbd
        qs_sc[...] = qbd * (DH_A ** -0.5)
        ksum_sc[...] = jnp.zeros(ksum_sc.shape, F32)
        m_sc[...] = jnp.zeros(m_sc.shape, F32)
        l_sc[...] = jnp.zeros(l_sc.shape, F32)

    qs = qs_sc[...]
    rk = lax.broadcasted_iota(jnp.int32, (1, MOBA_BLOCK), 1)
    ksum_all = ksum_sc[...]
    m_all = m_sc[...]
    l_all = l_sc[...]
    nblk = ppg // ppb
    s_blks = []
    for b in range(nblk):
        kt = jnp.concatenate([k_refs[b * ppb + p][...].reshape(W_A, PAGE_SIZE) for p in range(ppb)], axis=1)
        s_blks.append(_dot(qs, kt))
        ksum = jnp.sum(kt, axis=-1, keepdims=True)
        ksum_all = jnp.where(blk_lane == g * nblk + b, ksum, ksum_all)
    p_blks = []
    for b in range(nblk):
        n = g * nblk + b
        dist = ((past - n * MOBA_BLOCK) + tok - rk).astype(F32)
        logits = s_blks[b] - slope * dist
        m_b = jnp.max(logits, axis=-1, keepdims=True)
        p_ = jnp.exp(logits - m_b)
        p_blks.append(p_)
        m_all = jnp.where(blk_lane == n, m_b, m_all)
        l_all = jnp.where(blk_lane == n, jnp.sum(p_, axis=-1, keepdims=True), l_all)
    for b in range(nblk):
        vt = jnp.concatenate([v_refs[b * ppb + p][...].reshape(W_A, PAGE_SIZE) for p in range(ppb)], axis=1)
        o_sc[g * nblk + b] = _nt_dot(p_blks[b], vt)
    ksum_sc[...] = ksum_all
    m_sc[...] = m_all
    l_sc[...] = l_all

    @pl.when(g == ng - 1)
    def _():
        kmean = ksum_sc[...] * (1.0 / MOBA_BLOCK)
        scores = _dot(qf_sc[...], kmean, precision=HIGHEST)
        scores = jnp.where(blk_lane < npb, scores, -jnp.inf)
        sel, _ = _top_mask(scores, blk_lane.astype(F32), min(MOBA_TOPK, npb))
        kn = jnp.concatenate([kn_ref[h] for h in range(H_A)], axis=1)
        vn = jnp.concatenate([vn_ref[h] for h in range(H_A)], axis=1)
        s_own = _nt_dot(qs, kn)
        d_own = (tok - lax.broadcasted_iota(jnp.int32, (1, t), 1)).astype(F32)
        lg_own = jnp.where(d_own >= 0, s_own - slope * d_own, NEG_INF)
        m_all = m_sc[...]
        m_sel = jnp.max(jnp.where(sel, m_all, -jnp.inf), axis=-1, keepdims=True)
        m_fin = jnp.maximum(m_sel, jnp.max(lg_own, axis=-1, keepdims=True))
        w = jnp.where(sel, jnp.exp(m_all - m_fin), 0.0)
        p_own = jnp.exp(lg_own - m_fin)
        l_fin = jnp.sum(w * l_sc[...], axis=-1, keepdims=True) + jnp.sum(p_own, axis=-1, keepdims=True)
        o = _dot(p_own, vn)
        for n in range(npb):
            o = o + w[:, n:n + 1] * o_sc[n]
        o = o / l_fin
        out = jnp.zeros((t, W_A), F32)
        for h in range(H_A):
            out = out + jnp.where(feat_head == h, o[h * t:(h + 1) * t, :], 0.0)
        o_ref[...] = out


SAMPLE_PAGES_PER_STEP = 16


def _moba_sample(z, k_new, v_new, cache_k, cache_v, page_table):
    db, n_pages = page_table.shape
    t = z.shape[0] // db
    ppb = MOBA_BLOCK // PAGE_SIZE
    npb = n_pages // ppb
    assert n_pages == npb * ppb, "cached pages must fill whole MoBA blocks"
    ppg = SAMPLE_PAGES_PER_STEP if n_pages % SAMPLE_PAGES_PER_STEP == 0 else ppb
    ng = n_pages // ppg

    def page_map(b, g, pt, *, i):
        return (pt[b * n_pages + g * ppg + i], 0, 0, 0)

    page_specs = [pl.BlockSpec((None, H_A, DH_A, PAGE_SIZE), functools.partial(page_map, i=i))
                  for i in range(ppg)]
    grid_spec = pltpu.PrefetchScalarGridSpec(
        num_scalar_prefetch=1,
        grid=(db, ng),
        in_specs=[
            pl.BlockSpec((t, W_A), lambda b, g, pt: (b, Z_QA // W_A)),
            pl.BlockSpec((None, H_A, t, DH_A), lambda b, g, pt: (b, 0, 0, 0)),
            pl.BlockSpec((None, H_A, t, DH_A), lambda b, g, pt: (b, 0, 0, 0)),
        ] + page_specs + page_specs,
        out_specs=pl.BlockSpec((t, W_A), lambda b, g, pt: (b, 0)),
        scratch_shapes=[
            pltpu.VMEM((H_A * t, W_A), F32),
            pltpu.VMEM((H_A * t, W_A), F32),
            pltpu.VMEM((W_A, LANES), F32),
            pltpu.VMEM((H_A * t, LANES), F32),
            pltpu.VMEM((H_A * t, LANES), F32),
            pltpu.VMEM((npb, H_A * t, W_A), F32),
        ],
    )
    return pl.pallas_call(
        functools.partial(_moba_sample_kernel, ppg=ppg, past=n_pages * PAGE_SIZE),
        grid_spec=grid_spec,
        out_shape=jax.ShapeDtypeStruct((db * t, W_A), F32),
        compiler_params=pltpu.CompilerParams(
            dimension_semantics=("arbitrary", "arbitrary"), vmem_limit_bytes=VMEM_LIMIT),
        name="moba_sample",
    )(page_table.reshape(-1), z, k_new, v_new, *([cache_k] * ppg), *([cache_v] * ppg))


def _mlstm_kernel(q_ref, k_ref, v_ref, g_ref, c0_ref, n0_ref, m0_ref,
                  h_ref, c_ref, n_ref, m_ref, c_sc, n_sc, m_sc, *, lc, cps):
    step = pl.program_id(1)
    nsteps = pl.num_programs(1)
    lp = max(lc, MLSTM_CHUNK)

    @pl.when(step == 0)
    def _():
        c_sc[...] = c0_ref[0]
        n_sc[...] = n0_ref[0]
        m_sc[...] = jnp.broadcast_to(m0_ref[0], m_sc.shape)

    def padded(a):
        if a.shape[0] == lp:
            return a
        return jnp.concatenate([a, jnp.zeros((lp - a.shape[0],) + a.shape[1:], a.dtype)], axis=0)

    row = lax.broadcasted_iota(jnp.int32, (lp, lp), 0)
    col = lax.broadcasted_iota(jnp.int32, (lp, lp), 1)
    causal = row >= col
    tril = jnp.where(causal, 1.0, 0.0).astype(BF16)
    pick_8 = jnp.where(lax.broadcasted_iota(jnp.int32, (16, LANES), 0)
                       == lax.broadcasted_iota(jnp.int32, (16, LANES), 1), 1.0, 0.0).astype(BF16)
    gate_lane = lax.broadcasted_iota(jnp.int32, (1, LANES), 1)
    eye = jnp.where(lax.broadcasted_iota(jnp.int32, (DH_M, DH_M), 0)
                    == lax.broadcasted_iota(jnp.int32, (DH_M, DH_M), 1), 1.0, 0.0).astype(BF16)
    rvalid = lax.broadcasted_iota(jnp.int32, (lp, 1), 0) < lc
    cvalid = lax.broadcasted_iota(jnp.int32, (1, lp), 1) < lc
    m_all = m_sc[...]
    n_all = n_sc[...]
    c_all = [c_sc[h] for h in range(H_M)]
    pre = []
    for ci in range(cps):
        rs = slice(ci * lc, (ci + 1) * lc)
        q = padded(q_ref[rs, :])
        k = padded(k_ref[rs, :]) * (DH_M ** -0.5)
        v = padded(v_ref[rs, :])
        gts = padded(g_ref[rs, :])
        lf = jax.nn.log_sigmoid(gts)
        bcum = sum(_dot(tril, part) for part in _split3(lf))
        comb_t = sum(_nt_dot(pick_8, part) for part in _split3(jnp.where(gate_lane < H_M, gts, bcum)))
        heads = []
        for h in range(H_M):
            qh = q[:, h * DH_M:(h + 1) * DH_M]
            kh = k[:, h * DH_M:(h + 1) * DH_M]
            vh = v[:, h * DH_M:(h + 1) * DH_M]
            bc = bcum[:, H_M + h:H_M + h + 1]
            ic = gts[:, h:h + 1]
            br = comb_t[H_M + h:H_M + h + 1, :]
            ir = comb_t[h:h + 1, :]
            dmat = jnp.where(causal, bc - br + ir, -jnp.inf)
            qhb = qh.astype(BF16)
            wide = lambda col: jnp.broadcast_to(col, (lp, LANES))
            heads.append(dict(
                qh=qh, qhb=qhb, kh=kh, vhb=vh.astype(BF16), br=br, ir=ir, dmat=dmat,
                bc_w=wide(bc), ic_w=wide(ic),
                m_intra_w=wide(jnp.max(dmat, axis=-1, keepdims=True)),
                qk=_nt_dot(qhb, kh.astype(BF16)),
                kh_t=sum(_nt_dot(eye, part) for part in _split3(kh)),
            ))
        pre.append(heads)
    for ci in range(cps):
        rs = slice(ci * lc, (ci + 1) * lc)
        m_out, n_out, c_out = [], [], []
        for h in range(H_M):
            d = pre[ci][h]
            bc_w, ic_w, dmat = d["bc_w"], d["ic_w"], d["dmat"]
            m_prev = m_all[h:h + 1, :]
            c_prev = c_all[h]
            n_prev = n_all[h:h + 1, :]
            a_inter = bc_w + m_prev
            m_t = jnp.maximum(a_inter, d["m_intra_w"])
            w_inter = jnp.exp(a_inter - m_t)
            s = d["qk"] * jnp.exp(dmat - m_t[:, :lp])
            num = w_inter * _dot(d["qhb"], c_prev.astype(BF16)) + _dot(s.astype(BF16), d["vhb"])
            den = (w_inter[:, :1] * jnp.sum(d["qh"] * n_prev, axis=-1, keepdims=True)
                   + jnp.sum(s, axis=-1, keepdims=True))
            h_out = num / jnp.maximum(jnp.abs(den), jnp.exp(-m_t[:, :1]))
            h_ref[rs, h * DH_M:(h + 1) * DH_M] = h_out[:lc]
            m_new = m_t[lc - 1:lc, :]
            b_last = bc_w[lc - 1:lc, :]
            g_c = jnp.exp(b_last + m_prev - m_new)
            g_s = jnp.where(rvalid, jnp.exp(b_last - bc_w + ic_w - m_new), 0.0)
            g_s_row = jnp.where(cvalid, jnp.exp(b_last[:, :lp] - d["br"] + d["ir"] - m_new[:, :lp]), 0.0)
            kg_t = (d["kh_t"] * g_s_row).astype(BF16)
            c_out.append(g_c * c_prev + _dot(kg_t, d["vhb"]))
            n_out.append(g_c * n_prev + jnp.sum(g_s * d["kh"], axis=0, keepdims=True))
            m_out.append(m_new)
        m_all = jnp.concatenate(m_out, axis=0)
        n_all = jnp.concatenate(n_out, axis=0)
        c_all = c_out
    for h in range(H_M):
        c_sc[h] = c_all[h]
    n_sc[...] = n_all
    m_sc[...] = m_all

    @pl.when(step == nsteps - 1)
    def _():
        c_ref[0] = c_sc[...]
        n_ref[0] = n_sc[...]
        m_ref[0] = m_sc[:, 0:1]


def _mlstm(z, gif, c0, n0, m0, *, lc, cps):
    b = c0.shape[0]
    t = z.shape[0] // b
    rows = lc * cps
    nc = t // rows
    assert nc * rows == t
    row = lambda bi, ci: bi * nc + ci
    wcol = lambda off: off // W_M
    return pl.pallas_call(
        functools.partial(_mlstm_kernel, lc=lc, cps=cps),
        grid=(b, nc),
        in_specs=[
            pl.BlockSpec((rows, W_M), lambda bi, ci: (row(bi, ci), wcol(Z_QM))),
            pl.BlockSpec((rows, W_M), lambda bi, ci: (row(bi, ci), wcol(Z_KM))),
            pl.BlockSpec((rows, W_M), lambda bi, ci: (row(bi, ci), wcol(Z_VM))),
            pl.BlockSpec((rows, LANES), lambda bi, ci: (row(bi, ci), 0)),
            pl.BlockSpec((1, H_M, DH_M, DH_M), lambda bi, ci: (bi, 0, 0, 0)),
            pl.BlockSpec((1, H_M, DH_M), lambda bi, ci: (bi, 0, 0)),
            pl.BlockSpec((1, H_M, 1), lambda bi, ci: (bi, 0, 0)),
        ],
        out_specs=[
            pl.BlockSpec((rows, W_M), lambda bi, ci: (row(bi, ci), 0)),
            pl.BlockSpec((1, H_M, DH_M, DH_M), lambda bi, ci: (bi, 0, 0, 0)),
            pl.BlockSpec((1, H_M, DH_M), lambda bi, ci: (bi, 0, 0)),
            pl.BlockSpec((1, H_M, 1), lambda bi, ci: (bi, 0, 0)),
        ],
        out_shape=[
            jax.ShapeDtypeStruct((b * t, W_M), F32),
            jax.ShapeDtypeStruct((b, H_M, DH_M, DH_M), F32),
            jax.ShapeDtypeStruct((b, H_M, DH_M), F32),
            jax.ShapeDtypeStruct((b, H_M, 1), F32),
        ],
        scratch_shapes=[
            pltpu.VMEM((H_M, DH_M, DH_M), F32),
            pltpu.VMEM((H_M, DH_M), F32),
            pltpu.VMEM((H_M, LANES), F32),
        ],
        compiler_params=pltpu.CompilerParams(
            dimension_semantics=("arbitrary", "arbitrary"), vmem_limit_bytes=VMEM_LIMIT),
        name="mlstm",
    )(z, z, z, gif, c0, n0, m0[..., None])


def _layernorm(y, g, b):
    mu = jnp.mean(y, axis=-1, keepdims=True)
    var = jnp.mean(jnp.square(y - mu), axis=-1, keepdims=True)
    return (y - mu) * lax.rsqrt(var + LN_EPS) * g + b


ROUTER_ROWS = 256
FINISH_ROWS = 512


def _finish_kernel(x_ref, a_ref, mh_ref, om_ref, ga_ref, gm_ref, wa_ref, wm_ref, wo_ref,
                   g1_ref, b1_ref, wr_ref, br_ref,
                   h_ref, hb_ref, gates_ref, gates_t_ref, cnt_ref):
    sub = ROUTER_ROWS
    parts = [slice(s * sub, (s + 1) * sub) for s in range(x_ref.shape[0] // sub)]
    a = [_dot(a_ref[p, :].astype(BF16), wa_ref[...]) for p in parts]
    m = [_dot((mh_ref[p, :] * jax.nn.sigmoid(om_ref[p, :])).astype(BF16), wm_ref[...]) for p in parts]
    merged = [jax.nn.sigmoid(ga_ref[p, :]) * a_ + jax.nn.sigmoid(gm_ref[p, :]) * m_
              for p, a_, m_ in zip(parts, a, m)]
    y = [DN_ALPHA * x_ref[p, :] + _dot(mg.astype(BF16), wo_ref[...]) for p, mg in zip(parts, merged)]
    h = [_layernorm(y_, g1_ref[...], b1_ref[...]) for y_ in y]
    for p, h_ in zip(parts, h):
        h_ref[p, :] = h_
        hb_ref[p, :] = h_.astype(BF16)
    logits = [_dot(h_, wr_ref[...], precision=HIGHEST) + br_ref[...] for h_ in h]
    lane_f = lax.broadcasted_iota(jnp.int32, (1, LANES), 1).astype(F32)
    for s, (p, lg) in enumerate(zip(parts, logits)):
        sel, top = _top_mask(lg, lane_f, TOP_K)
        e = jnp.where(sel, jnp.exp(lg - top), 0.0)
        gates = e / jnp.sum(e, axis=-1, keepdims=True)
        gates_ref[p, :] = gates
        gates_t_ref[:, p] = gates.T
        cnt_ref[s] = jnp.sum(jnp.where(gates > 0.0, 1.0, 0.0), axis=0, keepdims=True).astype(jnp.int32)


def _finish(x2d, a, mh, z, wa, wm, wo, g1, b1, wr, br, *, tm):
    t = x2d.shape[0]
    nt = t // tm
    const = lambda i: (0, 0)
    return pl.pallas_call(
        _finish_kernel,
        grid=(nt,),
        in_specs=[
            pl.BlockSpec((tm, D_MODEL), lambda i: (i, 0)),
            pl.BlockSpec((tm, W_A), lambda i: (i, 0)),
            pl.BlockSpec((tm, W_M), lambda i: (i, 0)),
            pl.BlockSpec((tm, W_M), lambda i: (i, Z_OM // W_M)),
            pl.BlockSpec((tm, D_MODEL), lambda i: (i, Z_GA // D_MODEL)),
            pl.BlockSpec((tm, D_MODEL), lambda i: (i, Z_GM // D_MODEL)),
            pl.BlockSpec(wa.shape, const),
            pl.BlockSpec(wm.shape, const),
            pl.BlockSpec(wo.shape, const),
            pl.BlockSpec(g1.shape, const),
            pl.BlockSpec(b1.shape, const),
            pl.BlockSpec(wr.shape, const),
            pl.BlockSpec(br.shape, const),
        ],
        out_specs=[
            pl.BlockSpec((tm, D_MODEL), lambda i: (i, 0)),
            pl.BlockSpec((tm, D_MODEL), lambda i: (i, 0)),
            pl.BlockSpec((tm, LANES), lambda i: (i, 0)),
            pl.BlockSpec((LANES, tm), lambda i: (0, i)),
            pl.BlockSpec((tm // ROUTER_ROWS, 1, LANES), lambda i: (i, 0, 0)),
        ],
        out_shape=[
            jax.ShapeDtypeStruct((t, D_MODEL), F32),
            jax.ShapeDtypeStruct((t, D_MODEL), BF16),
            jax.ShapeDtypeStruct((t, LANES), F32),
            jax.ShapeDtypeStruct((LANES, t), F32),
            jax.ShapeDtypeStruct((t // ROUTER_ROWS, 1, LANES), jnp.int32),
        ],
        compiler_params=pltpu.CompilerParams(
            dimension_semantics=("arbitrary",), vmem_limit_bytes=VMEM_LIMIT),
        name="finish",
    )(x2d, a, mh, z, z, z, wa, wm, wo, g1, b1, wr, br)


MLSTM_CHUNKS_PER_STEP = 4
MOE_TILE = 2048
MOE_SEG = 512
MOE_SEG_ROWS = 80
MOE_RANK_BLOCK = ROUTER_ROWS


def _moe_kernel(cnt_ref, hb_ref, g_ref, gt_ref, wg_ref, bg_ref, wu_ref, bu_ref, wd_ref, bd_ref,
                y_ref, posc_sc, posr_sc, *, cnt_per_seg, seg):
    i = pl.program_id(0)
    e = pl.program_id(1)
    tt = hb_ref.shape[0]
    nseg = tt // seg
    rseg = MOE_SEG_ROWS
    blk = MOE_RANK_BLOCK

    @pl.when(e == 0)
    def _():
        y_ref[...] = jnp.zeros(y_ref.shape, F32)

    cnt_max = 0
    for sg in range(nseg):
        first = (i * nseg + sg) * cnt_per_seg
        cnt_seg = cnt_ref[first, e]
        for r in range(1, cnt_per_seg):
            cnt_seg = cnt_seg + cnt_ref[first + r, e]
        cnt_max = jnp.maximum(cnt_max, cnt_seg)

    @pl.when(cnt_max > 0)
    def _():
        lane = lax.broadcasted_iota(jnp.int32, (1, LANES), 1)
        gcol = jnp.sum(jnp.where(lane == e, g_ref[...], 0.0), axis=-1, keepdims=True)
        selc = jnp.broadcast_to(jnp.where(gcol > 0.0, 1.0, 0.0), (tt, LANES))
        rr = lax.broadcasted_iota(jnp.int32, (blk, blk), 0)
        cc = lax.broadcasted_iota(jnp.int32, (blk, blk), 1)
        strict_lower = jnp.where(rr > cc, 1.0, 0.0).astype(BF16)
        for bi in range(tt // blk):
            if (bi * blk) % seg == 0:
                offs = jnp.zeros((1, LANES), F32)
            sb = selc[bi * blk:(bi + 1) * blk]
            rank = _dot(strict_lower, sb.astype(BF16)) + offs
            posc_sc[bi * blk:(bi + 1) * blk, :] = jnp.where(sb > 0.0, rank, -1.0)
            offs = offs + jnp.sum(sb, axis=0, keepdims=True)
        grow = gt_ref[pl.ds(lax.rem(e, 8), 1), :]
        selr = jnp.where(grow > 0.0, 1.0, 0.0)
        ru = lax.broadcasted_iota(jnp.int32, (LANES, LANES), 0)
        cu = lax.broadcasted_iota(jnp.int32, (LANES, LANES), 1)
        strict_upper = jnp.where(ru < cu, 1.0, 0.0).astype(BF16)
        for ci in range(tt // LANES):
            if (ci * LANES) % seg == 0:
                offr = jnp.zeros((16, 1), F32)
            sr = jnp.broadcast_to(selr[:, ci * LANES:(ci + 1) * LANES], (16, LANES))
            rank = _dot(sr.astype(BF16), strict_upper) + offr
            posr_sc[:, ci * LANES:(ci + 1) * LANES] = jnp.where(sr > 0.0, rank, -1.0)
            offr = offr + jnp.sum(sr, axis=-1, keepdims=True)
        slot_sub = lax.broadcasted_iota(jnp.int32, (rseg, LANES), 0).astype(F32)
        slot_lane = lax.broadcasted_iota(jnp.int32, (1, LANES), 1).astype(F32)
        slot_lane = jnp.where(slot_lane < float(rseg), slot_lane, -1e9)

        def one_pass(s, carry):
            base = (s * rseg).astype(F32)
            xs_parts, gs_parts = [], []
            for sg in range(nseg):
                pm_chunks = []
                for ci in range(sg * seg // LANES, (sg + 1) * seg // LANES):
                    pr = posr_sc[0:1, ci * LANES:(ci + 1) * LANES] - base
                    pm_chunks.append(pr == slot_sub)
                pmask = jnp.concatenate(pm_chunks, axis=1)
                gs_parts.append(jnp.sum(jnp.where(pmask, grow[:, sg * seg:(sg + 1) * seg], 0.0),
                                        axis=-1, keepdims=True))
                xs_parts.append(_dot(jnp.where(pmask, 1.0, 0.0).astype(BF16),
                                     hb_ref[sg * seg:(sg + 1) * seg, :]).astype(BF16))
            xs = jnp.concatenate(xs_parts, axis=0)
            gs = jnp.concatenate(gs_parts, axis=0)
            gp = jnp.minimum(_dot(xs, wg_ref[0]) + bg_ref[0], SWIGLU_LIMIT)
            up = jnp.clip(_dot(xs, wu_ref[0]) + bu_ref[0], -SWIGLU_LIMIT, SWIGLU_LIMIT)
            hdn = (up + 1.0) * (gp * jax.nn.sigmoid(SWIGLU_ALPHA * gp))
            yb = ((_dot(hdn.astype(BF16), wd_ref[0]) + bd_ref[0]) * gs).astype(BF16)
            pad = jnp.zeros((LANES - rseg, yb.shape[1]), BF16)
            for sg in range(nseg):
                rows = slice(sg * seg, (sg + 1) * seg)
                pt = jnp.where(posc_sc[rows, :] - base == slot_lane, 1.0, 0.0).astype(BF16)
                y_seg = jnp.concatenate([yb[sg * rseg:(sg + 1) * rseg], pad], axis=0)
                y_ref[rows, :] += _dot(pt, y_seg)
            return carry

        lax.fori_loop(0, lax.div(cnt_max + (rseg - 1), rseg), one_pass, 0)


def _moe(hb, gates, gates_t, cnt, wg, bg, wu, bu, wd, bd, *, tt):
    t = hb.shape[0]
    nt = t // tt
    seg = min(MOE_SEG, tt)
    cnt_per_seg = cnt.shape[0] * seg // t
    assert cnt_per_seg * t == cnt.shape[0] * seg and tt % seg == 0 and seg % MOE_RANK_BLOCK == 0
    wspec = pl.BlockSpec((1, D_MODEL, D_MODEL), lambda i, e, c: (e, 0, 0))
    bspec = pl.BlockSpec((1, 1, D_MODEL), lambda i, e, c: (e, 0, 0))
    grid_spec = pltpu.PrefetchScalarGridSpec(
        num_scalar_prefetch=1,
        grid=(nt, N_EXPERTS),
        in_specs=[
            pl.BlockSpec((tt, D_MODEL), lambda i, e, c: (i, 0)),
            pl.BlockSpec((tt, LANES), lambda i, e, c: (i, 0)),
            pl.BlockSpec((8, tt), lambda i, e, c: (e // 8, i)),
            wspec, bspec, wspec, bspec, wspec, bspec,
        ],
        out_specs=pl.BlockSpec((tt, D_MODEL), lambda i, e, c: (i, 0)),
        scratch_shapes=[
            pltpu.VMEM((tt, LANES), F32),
            pltpu.VMEM((16, tt), F32),
        ],
    )
    return pl.pallas_call(
        functools.partial(_moe_kernel, cnt_per_seg=cnt_per_seg, seg=seg),
        grid_spec=grid_spec,
        out_shape=jax.ShapeDtypeStruct((t, D_MODEL), F32),
        compiler_params=pltpu.CompilerParams(
            dimension_semantics=("arbitrary", "arbitrary"), vmem_limit_bytes=VMEM_LIMIT),
        name="moe",
    )(cnt, hb, gates, gates_t, wg, bg, wu, bu, wd, bd)


def _ln2_kernel(h_ref, y_ref, g_ref, b_ref, o_ref):
    o_ref[...] = _layernorm(DN_ALPHA * h_ref[...] + y_ref[...], g_ref[...], b_ref[...])


def _ln2(h, y, g, b, *, tm):
    t = h.shape[0]
    spec = pl.BlockSpec((tm, D_MODEL), lambda i: (i, 0))
    const = pl.BlockSpec((1, D_MODEL), lambda i: (0, 0))
    return pl.pallas_call(
        _ln2_kernel,
        grid=(t // tm,),
        in_specs=[spec, spec, const, const],
        out_specs=spec,
        out_shape=jax.ShapeDtypeStruct((t, D_MODEL), F32),
        compiler_params=pltpu.CompilerParams(dimension_semantics=("arbitrary",)),
        name="ln2",
    )(h, y, g, b)


def _group(x2d, weights, *, ps, tm, moe_tile, attn, mlstm_state, lc):
    z, kb, vt, kpages, vpages, gif, kmean = _project(
        x2d, weights["wz"], weights["wkv"], weights["wif"], weights["bif"], ps=ps, tm=tm)
    a = attn(z, kb, vt, kmean.reshape(kmean.shape[0], W_A), kpages, vpages)
    t_seq = x2d.shape[0] // mlstm_state[0].shape[0]
    cps = MLSTM_CHUNKS_PER_STEP if (t_seq // lc) % MLSTM_CHUNKS_PER_STEP == 0 else 1
    mh, c_t, n_t, m_t = _mlstm(z, gif, *mlstm_state, lc=lc, cps=cps)
    h, hb, gates, gates_t, cnt = _finish(
        x2d, a, mh, z, weights["wa"], weights["wm"], weights["wo"], weights["g1"], weights["b1"],
        weights["wr"], weights["br"],
        tm=FINISH_ROWS if x2d.shape[0] % FINISH_ROWS == 0 else ROUTER_ROWS)
    y = _moe(hb, gates, gates_t, cnt.reshape(cnt.shape[0], LANES),
             weights["wg"], weights["bg"], weights["wu"], weights["bu"], weights["wd"], weights["bd"],
             tt=moe_tile)
    out = _ln2(h, y, weights["g2"], weights["b2"], tm=tm)
    return out, kpages, vpages, c_t, n_t, m_t[..., 0]


def _hi_lo(w):
    hi = w.astype(BF16)
    return jnp.concatenate([hi, (w - hi.astype(F32)).astype(BF16)], axis=1)


def kernel(x_prompt, x_sample, cache_k, cache_v, page_table, state_C, state_n, state_m,
           w_in, b_if, w_branch_a, w_branch_m, w_out, ln1_g, ln1_b,
           w_router, b_router, w_gate, b_gate, w_up, b_up, w_down, b_down, ln2_g, ln2_b):
    depth = w_in.shape[0]
    assert depth == 1, "single-layer step"
    bp, seq, _ = x_prompt.shape
    db, dseq, _ = x_sample.shape
    l = 0
    w = w_in[l]
    o_qa, o_ka, o_va, o_qm, o_om_end = 0, W_A, 2 * W_A, 3 * W_A, 3 * W_A + 4 * W_M
    o_if = o_om_end
    o_ga = o_if + 2 * H_M
    weights = {
        "wz": jnp.concatenate([w[:, o_ga:], w[:, o_qa:o_ka], w[:, o_qm:o_om_end]], axis=1).astype(BF16),
        "wkv": w[:, o_ka:o_qm].astype(BF16),
        "wif": _hi_lo(jnp.pad(w[:, o_if:o_ga], ((0, 0), (0, LANES - 2 * H_M)))),
        "bif": jnp.pad(b_if[l], (0, LANES - 2 * H_M))[None, :],
        "wa": w_branch_a[l].astype(BF16),
        "wm": w_branch_m[l].astype(BF16),
        "wo": w_out[l].astype(BF16),
        "g1": ln1_g[l][None, :], "b1": ln1_b[l][None, :],
        "wr": jnp.pad(w_router[l], ((0, 0), (0, LANES - N_EXPERTS))),
        "br": jnp.pad(b_router[l], (0, LANES - N_EXPERTS), constant_values=NEG_INF)[None, :],
        "wg": w_gate[l].astype(BF16), "bg": b_gate[l][:, None, :],
        "wu": w_up[l].astype(BF16), "bu": b_up[l][:, None, :],
        "wd": w_down[l].astype(BF16), "bd": b_down[l][:, None, :],
        "g2": ln2_g[l][None, :], "b2": ln2_b[l][None, :],
    }
    assert bp == 1, "prompt batch of one sequence"
    zero_state = (jnp.zeros((bp, H_M, DH_M, DH_M), F32), jnp.zeros((bp, H_M, DH_M), F32),
                  jnp.zeros((bp, H_M), F32))
    yp, kp, vp, cp, np_, mp = _group(
        x_prompt.reshape(bp * seq, D_MODEL), weights, ps=PAGE_SIZE, tm=256,
        moe_tile=min(MOE_TILE, bp * seq),
        attn=lambda z, kb, vt, kmean, kpg, vpg: _moba_prompt(z, kb, vt, kmean),
        mlstm_state=zero_state, lc=math.gcd(seq, MLSTM_CHUNK))
    ck = jnp.swapaxes(cache_k.reshape(cache_k.shape[1:]), -1, -2)
    cv = jnp.swapaxes(cache_v.reshape(cache_v.shape[1:]), -1, -2)
    ys, ks, vs, cs, ns, ms = _group(
        x_sample.reshape(db * dseq, D_MODEL), weights, ps=dseq, tm=256,
        moe_tile=min(MOE_TILE, db * dseq),
        attn=lambda z, kb, vt, kmean, kpg, vpg: _moba_sample(z, kpg, vpg, ck, cv, page_table),
        mlstm_state=(state_C[l], state_n[l], state_m[l]), lc=math.gcd(dseq, MLSTM_CHUNK))
    n_pg = seq // PAGE_SIZE
    return (yp.reshape(bp, seq, D_MODEL), ys.reshape(db, dseq, D_MODEL),
            kp.reshape(1, bp, n_pg, H_A, PAGE_SIZE, DH_A), vp.reshape(1, bp, n_pg, H_A, PAGE_SIZE, DH_A),
            ks.reshape(1, db, H_A, dseq, DH_A), vs.reshape(1, db, H_A, dseq, DH_A),
            cp[None], np_[None], mp[None], cs[None], ns[None], ms[None])
```

```python
import functools
import math

import jax
import jax.numpy as jnp
from jax import lax
from jax.experimental import pallas as pl
from jax.experimental.pallas import tpu as pltpu

F32 = jnp.float32
BF16 = jnp.bfloat16
HIGHEST = lax.Precision.HIGHEST

D_MODEL = 1024
PAGE_SIZE = 128
H_A = 8
DH_A = 64
W_A = H_A * DH_A
MOBA_BLOCK = 256
MOBA_TOPK = 3
H_M = 4
DH_M = 128
W_M = H_M * DH_M
MLSTM_CHUNK = 64
N_EXPERTS = 32
TOP_K = 4
SWIGLU_LIMIT = 7.0
SWIGLU_ALPHA = 1.702
DN_ALPHA = 2.0 ** 0.25
LN_EPS = 1e-5
NEG_INF = -1e30

LANES = 128
Z_GA, Z_GM, Z_QA, Z_QM, Z_KM, Z_VM, Z_OM = 0, 1024, 2048, 2560, 3072, 3584, 4096
Z_WIDTH = 4608
VT_ROWS = 80
VMEM_LIMIT = 56 * 1024 * 1024


def _nt_dot(a, b, precision=None):
    return lax.dot_general(a, b, (((1,), (1,)), ((), ())), precision=precision,
                           preferred_element_type=F32)


def _dot(a, b, precision=None):
    return jnp.dot(a, b, precision=precision, preferred_element_type=F32)


def _top_mask(scores, idx_f, k, axis=-1):
    n = scores.shape[axis]
    sel = jnp.zeros(scores.shape, jnp.bool_)
    work = scores
    first_max = None
    for _ in range(k):
        mx = jnp.max(work, axis=axis, keepdims=True)
        if first_max is None:
            first_max = mx
        cand = work == mx
        pick_idx = jnp.min(jnp.where(cand, idx_f, float(n)), axis=axis, keepdims=True)
        pick = idx_f == pick_idx
        sel = jnp.logical_or(sel, pick)
        work = jnp.where(pick, -jnp.inf, work)
    return sel, first_max


def _proj_kernel(x_ref, wz_ref, wkv_ref, wif_ref, bif_ref,
                 z_ref, kb_ref, vt_ref, kp_ref, vp_ref, gif_ref, kmean_ref, *, ps):
    x = x_ref[...]
    tm = x.shape[0]
    xb = x.astype(BF16)
    x_lo = (x - xb.astype(F32)).astype(BF16)
    w_hi = wif_ref[:, :LANES]
    gif_ref[...] = (_dot(xb, w_hi) + _dot(xb, wif_ref[:, LANES:]) + _dot(x_lo, w_hi)) + bif_ref[...]
    for c in range(Z_WIDTH // 512):
        z_ref[:, c * 512:(c + 1) * 512] = _dot(xb, wz_ref[:, c * 512:(c + 1) * 512])
    for c, pref in ((0, kp_ref), (1, vp_ref)):
        r = _dot(xb, wkv_ref[:, c * W_A:(c + 1) * W_A])
        if c == 0:
            kb_ref[...] = r.astype(BF16)
        else:
            rt = r.T
            tail = (lax.broadcasted_iota(jnp.int32, (VT_ROWS - DH_A, tm), 0) == 0).astype(F32)
            for h in range(H_A):
                vt_ref[0, h] = jnp.concatenate([rt[h * DH_A:(h + 1) * DH_A], tail], axis=0).astype(BF16)
        for h in range(H_A):
            pref[:, h, :, :] = r[:, h * DH_A:(h + 1) * DH_A].reshape(tm // ps, ps, DH_A)
        if c == 0:
            kmean_ref[...] = jnp.mean(r.reshape(tm // MOBA_BLOCK, MOBA_BLOCK, W_A), axis=1, keepdims=True)


def _project(x2d, wz, wkv, wif, bif, *, ps, tm):
    t = x2d.shape[0]
    assert tm == MOBA_BLOCK, "one MoBA block of keys per row tile"
    grid = (t // tm,)
    const = lambda i: (0, 0)
    return pl.pallas_call(
        functools.partial(_proj_kernel, ps=ps),
        grid=grid,
        in_specs=[
            pl.BlockSpec((tm, D_MODEL), lambda i: (i, 0)),
            pl.BlockSpec(wz.shape, const),
            pl.BlockSpec(wkv.shape, const),
            pl.BlockSpec(wif.shape, const),
            pl.BlockSpec(bif.shape, const),
        ],
        out_specs=[
            pl.BlockSpec((tm, Z_WIDTH), lambda i: (i, 0)),
            pl.BlockSpec((tm, W_A), lambda i: (i, 0)),
            pl.BlockSpec((1, H_A, VT_ROWS, tm), lambda i: (i, 0, 0, 0)),
            pl.BlockSpec((tm // ps, H_A, ps, DH_A), lambda i: (i, 0, 0, 0)),
            pl.BlockSpec((tm // ps, H_A, ps, DH_A), lambda i: (i, 0, 0, 0)),
            pl.BlockSpec((tm, LANES), lambda i: (i, 0)),
            pl.BlockSpec((tm // MOBA_BLOCK, 1, W_A), lambda i: (i, 0, 0)),
        ],
        out_shape=[
            jax.ShapeDtypeStruct((t, Z_WIDTH), F32),
            jax.ShapeDtypeStruct((t, W_A), BF16),
            jax.ShapeDtypeStruct((t // tm, H_A, VT_ROWS, tm), BF16),
            jax.ShapeDtypeStruct((t // ps, H_A, ps, DH_A), F32),
            jax.ShapeDtypeStruct((t // ps, H_A, ps, DH_A), F32),
            jax.ShapeDtypeStruct((t, LANES), F32),
            jax.ShapeDtypeStruct((t // MOBA_BLOCK, 1, W_A), F32),
        ],
        compiler_params=pltpu.CompilerParams(
            dimension_semantics=("arbitrary",), vmem_limit_bytes=VMEM_LIMIT),
        name="proj",
    )(x2d, wz, wkv, wif, bif)


PAST_UNROLL = 16


LOG2E = 1.4426950408889634


def _split3(x):
    hi = x.astype(BF16)
    r1 = x - hi.astype(F32)
    mid = r1.astype(BF16)
    lo = (r1 - mid.astype(F32)).astype(BF16)
    return hi, mid, lo


def _moba_prompt_kernel(q_ref, k_ref, vt_ref, km_ref, o_ref, m_sc, acc_sc, sel_sc):
    hp = pl.program_id(0)
    j = pl.program_id(1)
    bq = MOBA_BLOCK
    nb = km_ref.shape[0]
    q = q_ref[...]
    lane = lax.broadcasted_iota(jnp.int32, (1, LANES), 1)
    blk_f = lax.broadcasted_iota(jnp.int32, (nb, 1), 0).astype(F32)
    jf = j.astype(F32)
    rk = lax.broadcasted_iota(jnp.int32, (bq, bq), 0)
    rq = lax.broadcasted_iota(jnp.int32, (bq, bq), 1)
    causal = rq >= rk
    pos = lax.broadcasted_iota(jnp.int32, (bq, 1), 0).astype(F32)
    k_extra = jnp.where(lane < 3, pos, jnp.where(lane < 6, 1.0, 0.0)).astype(BF16)
    k_own = jnp.concatenate([k_ref[pl.ds(pl.multiple_of(j * bq, bq), bq), :], k_extra], axis=1)
    vt_own = vt_ref[j]
    q_augs = []
    coefs = []
    for hh in range(2):
        hmask = jnp.logical_and(lane >= hh * DH_A, lane < (hh + 1) * DH_A)
        qh = jnp.where(hmask, q, 0.0)
        head_f = (2 * hp + hh + 1).astype(F32)
        coef = jnp.exp2(jnp.zeros((1, 1), F32) - head_f) * LOG2E
        coefs.append(coef)
        scores = _nt_dot(km_ref[...], qh, precision=HIGHEST)
        scores = jnp.where(blk_f < jf, scores, NEG_INF)
        sel, _ = _top_mask(scores, blk_f, min(MOBA_TOPK, nb), axis=0)
        sel = jnp.logical_and(sel, blk_f < jf)
        sel_sc[hh] = jnp.where(sel, 1.0, 0.0)
        c3 = _split3(jnp.broadcast_to(coef, (bq, 1)))
        q3 = _split3(-(coef * pos))
        q_extra = jnp.zeros((bq, LANES), F32)
        for li, term in enumerate(c3 + q3):
            q_extra = jnp.where(lane == li, term.astype(F32), q_extra)
        qb = (qh * (DH_A ** -0.5 * LOG2E)).astype(BF16)
        q_aug = jnp.concatenate([qb, q_extra.astype(BF16)], axis=1)
        q_augs.append(q_aug)
        logits = jnp.where(causal, _nt_dot(k_own, q_aug), NEG_INF)
        m = jnp.max(logits, axis=0, keepdims=True)
        p = jnp.exp2(logits - m)
        m_sc[hh] = m
        acc_sc[hh] = _dot(vt_own[hh], p.astype(BF16))

    def body(it, carry):
        ns, kns, vtns, offs, valids = [], [], [], [], []
        for u in range(PAST_UNROLL):
            n_raw = it * PAST_UNROLL + u
            n = jnp.minimum(n_raw, j - 1)
            ns.append(n)
            valids.append(n_raw < j)
            kns.append(jnp.concatenate([k_ref[pl.ds(pl.multiple_of(n * bq, bq), bq), :], k_extra], axis=1))
            vtns.append(vt_ref[n])
            offs.append((j - n).astype(F32) * float(bq))
        logits = [[], []]
        for hh in range(2):
            for u in range(PAST_UNROLL):
                picked = sel_sc[hh, pl.ds(ns[u], 1), :]
                keep = jnp.logical_and(picked > 0.5, valids[u])
                rb = jnp.where(keep, -(coefs[hh] * offs[u]), NEG_INF)
                logits[hh].append(_nt_dot(kns[u], q_augs[hh]) + rb)
        m_news, acc_news = [], []
        for hh in range(2):
            m_old = m_sc[hh]
            m_new = m_old
            for lg in logits[hh]:
                m_new = jnp.maximum(m_new, jnp.max(lg, axis=0, keepdims=True))
            m_news.append(m_new)
            acc_news.append(jnp.exp2(m_old - m_new) * acc_sc[hh])
        for hh in range(2):
            acc_new = acc_news[hh]
            for u in range(PAST_UNROLL):
                p = jnp.exp2(logits[hh][u] - m_news[hh])
                acc_new = acc_new + _dot(vtns[u][hh], p.astype(BF16))
            acc_sc[hh] = acc_new
            m_sc[hh] = m_news[hh]
        return carry

    lax.fori_loop(0, lax.div(j + (PAST_UNROLL - 1), PAST_UNROLL), body, 0)
    acc0 = acc_sc[0]
    acc1 = acc_sc[1]
    o0 = acc0[:DH_A] / acc0[DH_A:DH_A + 1, :]
    o1 = acc1[:DH_A] / acc1[DH_A:DH_A + 1, :]
    o_ref[...] = jnp.concatenate([o0, o1], axis=0).T


def _moba_prompt(z, kb, vt, kmean):
    t = z.shape[0]
    nb = t // MOBA_BLOCK
    bq = MOBA_BLOCK
    qcol = Z_QA // LANES
    return pl.pallas_call(
        _moba_prompt_kernel,
        grid=(H_A // 2, nb),
        in_specs=[
            pl.BlockSpec((bq, LANES), lambda hp, j: (j, qcol + hp)),
            pl.BlockSpec((t, LANES), lambda hp, j: (0, hp)),
            pl.BlockSpec((nb, 2, VT_ROWS, bq), lambda hp, j: (0, hp, 0, 0)),
            pl.BlockSpec((nb, LANES), lambda hp, j: (0, hp)),
        ],
        out_specs=pl.BlockSpec((bq, LANES), lambda hp, j: (j, hp)),
        out_shape=jax.ShapeDtypeStruct((t, W_A), F32),
        scratch_shapes=[
            pltpu.VMEM((2, 1, bq), F32),
            pltpu.VMEM((2, VT_ROWS, bq), F32),
            pltpu.VMEM((2, nb, bq), F32),
        ],
        compiler_params=pltpu.CompilerParams(
            dimension_semantics=("arbitrary", "arbitrary"), vmem_limit_bytes=VMEM_LIMIT),
        name="moba_prompt",
    )(z, kb, vt, kmean)


def _moba_sample_kernel(pt_ref, q_ref, kn_ref, vn_ref, *refs, ppg, past):
    k_refs = refs[:ppg]
    v_refs = refs[ppg:2 * ppg]
    o_ref = refs[2 * ppg]
    qs_sc, qf_sc, ksum_sc, m_sc, l_sc, o_sc = refs[2 * ppg + 1:]
    g = pl.program_id(1)
    ng = pl.num_programs(1)
    t = q_ref.shape[0]
    rows = H_A * t
    npb = o_sc.shape[0]
    ppb = MOBA_BLOCK // PAGE_SIZE
    row_i = lax.broadcasted_iota(jnp.int32, (rows, 1), 0)
    slope = jnp.exp2(-(lax.div(row_i, t) + 1).astype(F32))
    tok = lax.rem(row_i, t)
    blk_lane = lax.broadcasted_iota(jnp.int32, (1, LANES), 1)
    feat_head = lax.div(lax.broadcasted_iota(jnp.int32, (1, W_A), 1), DH_A)

    @pl.when(g == 0)
    def _():
        q = q_ref[...]
        qbd = jnp.concatenate([jnp.where(feat_head == h, q, 0.0) for h in range(H_A)], axis=0)
        qf_sc[...] = qbd
        qs_sc[...] = qbd * (DH_A ** -0.5)
        ksum_sc[...] = jnp.zeros(ksum_sc.shape, F32)
        m_sc[...] = jnp.zeros(m_sc.shape, F32)
        l_sc[...] = jnp.zeros(l_sc.shape, F32)

    qs = qs_sc[...]
    rk = lax.broadcasted_iota(jnp.int32, (1, MOBA_BLOCK), 1)
    ksum_all = ksum_sc[...]
    m_all = m_sc[...]
    l_all = l_sc[...]
    nblk = ppg // ppb
    s_blks = []
    for b in range(nblk):
        kt = jnp.concatenate([k_refs[b * ppb + p][...].reshape(W_A, PAGE_SIZE) for p in range(ppb)], axis=1)
        s_blks.append(_dot(qs, kt))
        ksum = jnp.sum(kt, axis=-1, keepdims=True)
        ksum_all = jnp.where(blk_lane == g * nblk + b, ksum, ksum_all)
    p_blks = []
    for b in range(nblk):
        n = g * nblk + b
        dist = ((past - n * MOBA_BLOCK) + tok - rk).astype(F32)
        logits = s_blks[b] - slope * dist
        m_b = jnp.max(logits, axis=-1, keepdims=True)
        p_ = jnp.exp(logits - m_b)
        p_blks.append(p_)
        m_all = jnp.where(blk_lane == n, m_b, m_all)
        l_all = jnp.where(blk_lane == n, jnp.sum(p_, axis=-1, keepdims=True), l_all)
    for b in range(nblk):
        vt = jnp.concatenate([v_refs[b * ppb + p][...].reshape(W_A, PAGE_SIZE) for p in range(ppb)], axis=1)
        o_sc[g * nblk + b] = _nt_dot(p_blks[b], vt)
    ksum_sc[...] = ksum_all
    m_sc[...] = m_all
    l_sc[...] = l_all

    @pl.when(g == ng - 1)
    def _():
        kmean = ksum_sc[...] * (1.0 / MOBA_BLOCK)
        scores = _dot(qf_sc[...], kmean, precision=HIGHEST)
        scores = jnp.where(blk_lane < npb, scores, -jnp.inf)
        sel, _ = _top_mask(scores, blk_lane.astype(F32), min(MOBA_TOPK, npb))
        kn = jnp.concatenate([kn_ref[h] for h in range(H_A)], axis=1)
        vn = jnp.concatenate([vn_ref[h] for h in range(H_A)], axis=1)
        s_own = _nt_dot(qs, kn)
        d_own = (tok - lax.broadcasted_iota(jnp.int32, (1, t), 1)).astype(F32)
        lg_own = jnp.where(d_own >= 0, s_own - slope * d_own, NEG_INF)
        m_all = m_sc[...]
        m_sel = jnp.max(jnp.where(sel, m_all, -jnp.inf), axis=-1, keepdims=True)
        m_fin = jnp.maximum(m_sel, jnp.max(lg_own, axis=-1, keepdims=True))
        w = jnp.where(sel, jnp.exp(m_all - m_fin), 0.0)
        p_own = jnp.exp(lg_own - m_fin)
        l_fin = jnp.sum(w * l_sc[...], axis=-1, keepdims=True) + jnp.sum(p_own, axis=-1, keepdims=True)
        o = _dot(p_own, vn)
        for n in range(npb):
            o = o + w[:, n:n + 1] * o_sc[n]
        o = o / l_fin
        out = jnp.zeros((t, W_A), F32)
        for h in range(H_A):
            out = out + jnp.where(feat_head == h, o[h * t:(h + 1) * t, :], 0.0)
        o_ref[...] = out


SAMPLE_PAGES_PER_STEP = 32


def _moba_sample(z, k_new, v_new, cache_k, cache_v, page_table):
    db, n_pages = page_table.shape
    t = z.shape[0] // db
    ppb = MOBA_BLOCK // PAGE_SIZE
    npb = n_pages // ppb
    assert n_pages == npb * ppb, "cached pages must fill whole MoBA blocks"
    ppg = SAMPLE_PAGES_PER_STEP if n_pages % SAMPLE_PAGES_PER_STEP == 0 else ppb
    ng = n_pages // ppg

    def page_map(b, g, pt, *, i):
        return (pt[b * n_pages + g * ppg + i], 0, 0, 0)

    page_specs = [pl.BlockSpec((None, H_A, DH_A, PAGE_SIZE), functools.partial(page_map, i=i))
                  for i in range(ppg)]
    grid_spec = pltpu.PrefetchScalarGridSpec(
        num_scalar_prefetch=1,
        grid=(db, ng),
        in_specs=[
            pl.BlockSpec((t, W_A), lambda b, g, pt: (b, Z_QA // W_A)),
            pl.BlockSpec((None, H_A, t, DH_A), lambda b, g, pt: (b, 0, 0, 0)),
            pl.BlockSpec((None, H_A, t, DH_A), lambda b, g, pt: (b, 0, 0, 0)),
        ] + page_specs + page_specs,
        out_specs=pl.BlockSpec((t, W_A), lambda b, g, pt: (b, 0)),
        scratch_shapes=[
            pltpu.VMEM((H_A * t, W_A), F32),
            pltpu.VMEM((H_A * t, W_A), F32),
            pltpu.VMEM((W_A, LANES), F32),
            pltpu.VMEM((H_A * t, LANES), F32),
            pltpu.VMEM((H_A * t, LANES), F32),
            pltpu.VMEM((npb, H_A * t, W_A), F32),
        ],
    )
    return pl.pallas_call(
        functools.partial(_moba_sample_kernel, ppg=ppg, past=n_pages * PAGE_SIZE),
        grid_spec=grid_spec,
        out_shape=jax.ShapeDtypeStruct((db * t, W_A), F32),
        compiler_params=pltpu.CompilerParams(
            dimension_semantics=("arbitrary", "arbitrary"), vmem_limit_bytes=VMEM_LIMIT),
        name="moba_sample",
    )(page_table.reshape(-1), z, k_new, v_new, *([cache_k] * ppg), *([cache_v] * ppg))


def _mlstm_kernel(q_ref, k_ref, v_ref, g_ref, c0_ref, n0_ref, m0_ref,
                  h_ref, c_ref, n_ref, m_ref, c_sc, n_sc, m_sc, *, lc, cps):
    step = pl.program_id(1)
    nsteps = pl.num_programs(1)
    lp = max(lc, MLSTM_CHUNK)

    @pl.when(step == 0)
    def _():
        c_sc[...] = c0_ref[0]
        n_sc[...] = n0_ref[0]
        m_sc[...] = jnp.broadcast_to(m0_ref[0], m_sc.shape)

    def padded(a):
        if a.shape[0] == lp:
            return a
        return jnp.concatenate([a, jnp.zeros((lp - a.shape[0],) + a.shape[1:], a.dtype)], axis=0)

    row = lax.broadcasted_iota(jnp.int32, (lp, lp), 0)
    col = lax.broadcasted_iota(jnp.int32, (lp, lp), 1)
    causal = row >= col
    tril = jnp.where(causal, 1.0, 0.0).astype(BF16)
    pick_8 = jnp.where(lax.broadcasted_iota(jnp.int32, (16, LANES), 0)
                       == lax.broadcasted_iota(jnp.int32, (16, LANES), 1), 1.0, 0.0).astype(BF16)
    gate_lane = lax.broadcasted_iota(jnp.int32, (1, LANES), 1)
    eye = jnp.where(lax.broadcasted_iota(jnp.int32, (DH_M, DH_M), 0)
                    == lax.broadcasted_iota(jnp.int32, (DH_M, DH_M), 1), 1.0, 0.0).astype(BF16)
    rvalid = lax.broadcasted_iota(jnp.int32, (lp, 1), 0) < lc
    cvalid = lax.broadcasted_iota(jnp.int32, (1, lp), 1) < lc
    m_all = m_sc[...]
    n_all = n_sc[...]
    c_all = [c_sc[h] for h in range(H_M)]
    pre = []
    for ci in range(cps):
        rs = slice(ci * lc, (ci + 1) * lc)
        q = padded(q_ref[rs, :])
        k = padded(k_ref[rs, :]) * (DH_M ** -0.5)
        v = padded(v_ref[rs, :])
        gts = padded(g_ref[rs, :])
        lf = jax.nn.log_sigmoid(gts)
        bcum = sum(_dot(tril, part) for part in _split3(lf))
        comb_t = sum(_nt_dot(pick_8, part) for part in _split3(jnp.where(gate_lane < H_M, gts, bcum)))
        heads = []
        for h in range(H_M):
            qh = q[:, h * DH_M:(h + 1) * DH_M]
            kh = k[:, h * DH_M:(h + 1) * DH_M]
            vh = v[:, h * DH_M:(h + 1) * DH_M]
            bc = bcum[:, H_M + h:H_M + h + 1]
            ic = gts[:, h:h + 1]
            br = comb_t[H_M + h:H_M + h + 1, :]
            ir = comb_t[h:h + 1, :]
            dmat = jnp.where(causal, bc - br + ir, -jnp.inf)
            qhb = qh.astype(BF16)
            wide = lambda col: jnp.broadcast_to(col, (lp, LANES))
            heads.append(dict(
                qh=qh, qhb=qhb, kh=kh, vhb=vh.astype(BF16), br=br, ir=ir, dmat=dmat,
                bc_w=wide(bc), ic_w=wide(ic),
                m_intra_w=wide(jnp.max(dmat, axis=-1, keepdims=True)),
                qk=_nt_dot(qhb, kh.astype(BF16)),
                kh_t=sum(_nt_dot(eye, part) for part in _split3(kh)),
            ))
        pre.append(heads)
    for ci in range(cps):
        rs = slice(ci * lc, (ci + 1) * lc)
        m_out, n_out, c_out = [], [], []
        for h in range(H_M):
            d = pre[ci][h]
            bc_w, ic_w, dmat = d["bc_w"], d["ic_w"], d["dmat"]
            m_prev = m_all[h:h + 1, :]
            c_prev = c_all[h]
            n_prev = n_all[h:h + 1, :]
            a_inter = bc_w + m_prev
            m_t = jnp.maximum(a_inter, d["m_intra_w"])
            w_inter = jnp.exp(a_inter - m_t)
            s = d["qk"] * jnp.exp(dmat - m_t[:, :lp])
            num = w_inter * _dot(d["qhb"], c_prev.astype(BF16)) + _dot(s.astype(BF16), d["vhb"])
            den = (w_inter[:, :1] * jnp.sum(d["qh"] * n_prev, axis=-1, keepdims=True)
                   + jnp.sum(s, axis=-1, keepdims=True))
            h_out = num / jnp.maximum(jnp.abs(den), jnp.exp(-m_t[:, :1]))
            h_ref[rs, h * DH_M:(h + 1) * DH_M] = h_out[:lc]
            m_new = m_t[lc - 1:lc, :]
            b_last = bc_w[lc - 1:lc, :]
            g_c = jnp.exp(b_last + m_prev - m_new)
            g_s = jnp.where(rvalid, jnp.exp(b_last - bc_w + ic_w - m_new), 0.0)
            g_s_row = jnp.where(cvalid, jnp.exp(b_last[:, :lp] - d["br"] + d["ir"] - m_new[:, :lp]), 0.0)
            kg_t = (d["kh_t"] * g_s_row).astype(BF16)
            c_out.append(g_c * c_prev + _dot(kg_t, d["vhb"]))
            n_out.append(g_c * n_prev + jnp.sum(g_s * d["kh"], axis=0, keepdims=True))
            m_out.append(m_new)
        m_all = jnp.concatenate(m_out, axis=0)
        n_all = jnp.concatenate(n_out, axis=0)
        c_all = c_out
    for h in range(H_M):
        c_sc[h] = c_all[h]
    n_sc[...] = n_all
    m_sc[...] = m_all

    @pl.when(step == nsteps - 1)
    def _():
        c_ref[0] = c_sc[...]
        n_ref[0] = n_sc[...]
        m_ref[0] = m_sc[:, 0:1]


def _mlstm(z, gif, c0, n0, m0, *, lc, cps):
    b = c0.shape[0]
    t = z.shape[0] // b
    rows = lc * cps
    nc = t // rows
    assert nc * rows == t
    row = lambda bi, ci: bi * nc + ci
    wcol = lambda off: off // W_M
    return pl.pallas_call(
        functools.partial(_mlstm_kernel, lc=lc, cps=cps),
        grid=(b, nc),
        in_specs=[
            pl.BlockSpec((rows, W_M), lambda bi, ci: (row(bi, ci), wcol(Z_QM))),
            pl.BlockSpec((rows, W_M), lambda bi, ci: (row(bi, ci), wcol(Z_KM))),
            pl.BlockSpec((rows, W_M), lambda bi, ci: (row(bi, ci), wcol(Z_VM))),
            pl.BlockSpec((rows, LANES), lambda bi, ci: (row(bi, ci), 0)),
            pl.BlockSpec((1, H_M, DH_M, DH_M), lambda bi, ci: (bi, 0, 0, 0)),
            pl.BlockSpec((1, H_M, DH_M), lambda bi, ci: (bi, 0, 0)),
            pl.BlockSpec((1, H_M, 1), lambda bi, ci: (bi, 0, 0)),
        ],
        out_specs=[
            pl.BlockSpec((rows, W_M), lambda bi, ci: (row(bi, ci), 0)),
            pl.BlockSpec((1, H_M, DH_M, DH_M), lambda bi, ci: (bi, 0, 0, 0)),
            pl.BlockSpec((1, H_M, DH_M), lambda bi, ci: (bi, 0, 0)),
            pl.BlockSpec((1, H_M, 1), lambda bi, ci: (bi, 0, 0)),
        ],
        out_shape=[
            jax.ShapeDtypeStruct((b * t, W_M), F32),
            jax.ShapeDtypeStruct((b, H_M, DH_M, DH_M), F32),
            jax.ShapeDtypeStruct((b, H_M, DH_M), F32),
            jax.ShapeDtypeStruct((b, H_M, 1), F32),
        ],
        scratch_shapes=[
            pltpu.VMEM((H_M, DH_M, DH_M), F32),
            pltpu.VMEM((H_M, DH_M), F32),
            pltpu.VMEM((H_M, LANES), F32),
        ],
        compiler_params=pltpu.CompilerParams(
            dimension_semantics=("arbitrary", "arbitrary"), vmem_limit_bytes=VMEM_LIMIT),
        name="mlstm",
    )(z, z, z, gif, c0, n0, m0[..., None])


def _layernorm(y, g, b):
    mu = jnp.mean(y, axis=-1, keepdims=True)
    var = jnp.mean(jnp.square(y - mu), axis=-1, keepdims=True)
    return (y - mu) * lax.rsqrt(var + LN_EPS) * g + b


ROUTER_ROWS = 256
FINISH_ROWS = 512


def _finish_kernel(x_ref, a_ref, mh_ref, om_ref, ga_ref, gm_ref, wa_ref, wm_ref, wo_ref,
                   g1_ref, b1_ref, wr_ref, br_ref,
                   h_ref, hb_ref, gates_ref, gates_t_ref, cnt_ref):
    sub = ROUTER_ROWS
    parts = [slice(s * sub, (s + 1) * sub) for s in range(x_ref.shape[0] // sub)]
    a = [_dot(a_ref[p, :].astype(BF16), wa_ref[...]) for p in parts]
    m = [_dot((mh_ref[p, :] * jax.nn.sigmoid(om_ref[p, :])).astype(BF16), wm_ref[...]) for p in parts]
    merged = [jax.nn.sigmoid(ga_ref[p, :]) * a_ + jax.nn.sigmoid(gm_ref[p, :]) * m_
              for p, a_, m_ in zip(parts, a, m)]
    y = [DN_ALPHA * x_ref[p, :] + _dot(mg.astype(BF16), wo_ref[...]) for p, mg in zip(parts, merged)]
    h = [_layernorm(y_, g1_ref[...], b1_ref[...]) for y_ in y]
    for p, h_ in zip(parts, h):
        h_ref[p, :] = h_
        hb_ref[p, :] = h_.astype(BF16)
    logits = [_dot(h_, wr_ref[...], precision=HIGHEST) + br_ref[...] for h_ in h]
    lane_f = lax.broadcasted_iota(jnp.int32, (1, LANES), 1).astype(F32)
    for s, (p, lg) in enumerate(zip(parts, logits)):
        sel, top = _top_mask(lg, lane_f, TOP_K)
        e = jnp.where(sel, jnp.exp(lg - top), 0.0)
        gates = e / jnp.sum(e, axis=-1, keepdims=True)
        gates_ref[p, :] = gates
        gates_t_ref[:, p] = gates.T
        cnt_ref[s] = jnp.sum(jnp.where(gates > 0.0, 1.0, 0.0), axis=0, keepdims=True).astype(jnp.int32)


def _finish(x2d, a, mh, z, wa, wm, wo, g1, b1, wr, br, *, tm):
    t = x2d.shape[0]
    nt = t // tm
    const = lambda i: (0, 0)
    return pl.pallas_call(
        _finish_kernel,
        grid=(nt,),
        in_specs=[
            pl.BlockSpec((tm, D_MODEL), lambda i: (i, 0)),
            pl.BlockSpec((tm, W_A), lambda i: (i, 0)),
            pl.BlockSpec((tm, W_M), lambda i: (i, 0)),
            pl.BlockSpec((tm, W_M), lambda i: (i, Z_OM // W_M)),
            pl.BlockSpec((tm, D_MODEL), lambda i: (i, Z_GA // D_MODEL)),
            pl.BlockSpec((tm, D_MODEL), lambda i: (i, Z_GM // D_MODEL)),
            pl.BlockSpec(wa.shape, const),
            pl.BlockSpec(wm.shape, const),
            pl.BlockSpec(wo.shape, const),
            pl.BlockSpec(g1.shape, const),
            pl.BlockSpec(b1.shape, const),
            pl.BlockSpec(wr.shape, const),
            pl.BlockSpec(br.shape, const),
        ],
        out_specs=[
            pl.BlockSpec((tm, D_MODEL), lambda i: (i, 0)),
            pl.BlockSpec((tm, D_MODEL), lambda i: (i, 0)),
            pl.BlockSpec((tm, LANES), lambda i: (i, 0)),
            pl.BlockSpec((LANES, tm), lambda i: (0, i)),
            pl.BlockSpec((tm // ROUTER_ROWS, 1, LANES), lambda i: (i, 0, 0)),
        ],
        out_shape=[
            jax.ShapeDtypeStruct((t, D_MODEL), F32),
            jax.ShapeDtypeStruct((t, D_MODEL), BF16),
            jax.ShapeDtypeStruct((t, LANES), F32),
            jax.ShapeDtypeStruct((LANES, t), F32),
            jax.ShapeDtypeStruct((t // ROUTER_ROWS, 1, LANES), jnp.int32),
        ],
        compiler_params=pltpu.CompilerParams(
            dimension_semantics=("arbitrary",), vmem_limit_bytes=VMEM_LIMIT),
        name="finish",
    )(x2d, a, mh, z, z, z, wa, wm, wo, g1, b1, wr, br)


MLSTM_CHUNKS_PER_STEP = 4
MOE_TILE = 2048
MOE_SEG = 512
MOE_SEG_ROWS = 80
MOE_RANK_BLOCK = ROUTER_ROWS


def _moe_kernel(cnt_ref, hb_ref, g_ref, gt_ref, wg_ref, bg_ref, wu_ref, bu_ref, wd_ref, bd_ref,
                y_ref, posc_sc, posr_sc, *, cnt_per_seg, seg):
    i = pl.program_id(0)
    e = pl.program_id(1)
    tt = hb_ref.shape[0]
    nseg = tt // seg
    rseg = MOE_SEG_ROWS
    blk = MOE_RANK_BLOCK

    @pl.when(e == 0)
    def _():
        y_ref[...] = jnp.zeros(y_ref.shape, F32)

    cnt_max = 0
    for sg in range(nseg):
        first = (i * nseg + sg) * cnt_per_seg
        cnt_seg = cnt_ref[first, e]
        for r in range(1, cnt_per_seg):
            cnt_seg = cnt_seg + cnt_ref[first + r, e]
        cnt_max = jnp.maximum(cnt_max, cnt_seg)

    @pl.when(cnt_max > 0)
    def _():
        lane = lax.broadcasted_iota(jnp.int32, (1, LANES), 1)
        gcol = jnp.sum(jnp.where(lane == e, g_ref[...], 0.0), axis=-1, keepdims=True)
        selc = jnp.broadcast_to(jnp.where(gcol > 0.0, 1.0, 0.0), (tt, LANES))
        rr = lax.broadcasted_iota(jnp.int32, (blk, blk), 0)
        cc = lax.broadcasted_iota(jnp.int32, (blk, blk), 1)
        strict_lower = jnp.where(rr > cc, 1.0, 0.0).astype(BF16)
        for bi in range(tt // blk):
            if (bi * blk) % seg == 0:
                offs = jnp.zeros((1, LANES), F32)
            sb = selc[bi * blk:(bi + 1) * blk]
            rank = _dot(strict_lower, sb.astype(BF16)) + offs
            posc_sc[bi * blk:(bi + 1) * blk, :] = jnp.where(sb > 0.0, rank, -1.0)
            offs = offs + jnp.sum(sb, axis=0, keepdims=True)
        grow = gt_ref[pl.ds(lax.rem(e, 8), 1), :]
        selr = jnp.where(grow > 0.0, 1.0, 0.0)
        ru = lax.broadcasted_iota(jnp.int32, (LANES, LANES), 0)
        cu = lax.broadcasted_iota(jnp.int32, (LANES, LANES), 1)
        strict_upper = jnp.where(ru < cu, 1.0, 0.0).astype(BF16)
        for ci in range(tt // LANES):
            if (ci * LANES) % seg == 0:
                offr = jnp.zeros((16, 1), F32)
            sr = jnp.broadcast_to(selr[:, ci * LANES:(ci + 1) * LANES], (16, LANES))
            rank = _dot(sr.astype(BF16), strict_upper) + offr
            posr_sc[:, ci * LANES:(ci + 1) * LANES] = jnp.where(sr > 0.0, rank, -1.0)
            offr = offr + jnp.sum(sr, axis=-1, keepdims=True)
        slot_sub = lax.broadcasted_iota(jnp.int32, (rseg, LANES), 0).astype(F32)
        slot_lane = lax.broadcasted_iota(jnp.int32, (1, LANES), 1).astype(F32)
        slot_lane = jnp.where(slot_lane < float(rseg), slot_lane, -1e9)

        def one_pass(s, carry):
            base = (s * rseg).astype(F32)
            xs_parts, gs_parts = [], []
            for sg in range(nseg):
                pm_chunks = []
                for ci in range(sg * seg // LANES, (sg + 1) * seg // LANES):
                    pr = posr_sc[0:1, ci * LANES:(ci + 1) * LANES] - base
                    pm_chunks.append(pr == slot_sub)
                pmask = jnp.concatenate(pm_chunks, axis=1)
                gs_parts.append(jnp.sum(jnp.where(pmask, grow[:, sg * seg:(sg + 1) * seg], 0.0),
                                        axis=-1, keepdims=True))
                xs_parts.append(_dot(jnp.where(pmask, 1.0, 0.0).astype(BF16),
                                     hb_ref[sg * seg:(sg + 1) * seg, :]).astype(BF16))
            xs = jnp.concatenate(xs_parts, axis=0)
            gs = jnp.concatenate(gs_parts, axis=0)
            gp = jnp.minimum(_dot(xs, wg_ref[0]) + bg_ref[0], SWIGLU_LIMIT)
            up = jnp.clip(_dot(xs, wu_ref[0]) + bu_ref[0], -SWIGLU_LIMIT, SWIGLU_LIMIT)
            hdn = (up + 1.0) * (gp * jax.nn.sigmoid(SWIGLU_ALPHA * gp))
            yb = ((_dot(hdn.astype(BF16), wd_ref[0]) + bd_ref[0]) * gs).astype(BF16)
            pad = jnp.zeros((LANES - rseg, yb.shape[1]), BF16)
            for sg in range(nseg):
                rows = slice(sg * seg, (sg + 1) * seg)
                pt = jnp.where(posc_sc[rows, :] - base == slot_lane, 1.0, 0.0).astype(BF16)
                y_seg = jnp.concatenate([yb[sg * rseg:(sg + 1) * rseg], pad], axis=0)
                y_ref[rows, :] += _dot(pt, y_seg)
            return carry

        lax.fori_loop(0, lax.div(cnt_max + (rseg - 1), rseg), one_pass, 0)


def _moe(hb, gates, gates_t, cnt, wg, bg, wu, bu, wd, bd, *, tt):
    t = hb.shape[0]
    nt = t // tt
    seg = min(MOE_SEG, tt)
    cnt_per_seg = cnt.shape[0] * seg // t
    assert cnt_per_seg * t == cnt.shape[0] * seg and tt % seg == 0 and seg % MOE_RANK_BLOCK == 0
    wspec = pl.BlockSpec((1, D_MODEL, D_MODEL), lambda i, e, c: (e, 0, 0))
    bspec = pl.BlockSpec((1, 1, D_MODEL), lambda i, e, c: (e, 0, 0))
    grid_spec = pltpu.PrefetchScalarGridSpec(
        num_scalar_prefetch=1,
        grid=(nt, N_EXPERTS),
        in_specs=[
            pl.BlockSpec((tt, D_MODEL), lambda i, e, c: (i, 0)),
            pl.BlockSpec((tt, LANES), lambda i, e, c: (i, 0)),
            pl.BlockSpec((8, tt), lambda i, e, c: (e // 8, i)),
            wspec, bspec, wspec, bspec, wspec, bspec,
        ],
        out_specs=pl.BlockSpec((tt, D_MODEL), lambda i, e, c: (i, 0)),
        scratch_shapes=[
            pltpu.VMEM((tt, LANES), F32),
            pltpu.VMEM((16, tt), F32),
        ],
    )
    return pl.pallas_call(
        functools.partial(_moe_kernel, cnt_per_seg=cnt_per_seg, seg=seg),
        grid_spec=grid_spec,
        out_shape=jax.ShapeDtypeStruct((t, D_MODEL), F32),
        compiler_params=pltpu.CompilerParams(
            dimension_semantics=("arbitrary", "arbitrary"), vmem_limit_bytes=VMEM_LIMIT),
        name="moe",
    )(cnt, hb, gates, gates_t, wg, bg, wu, bu, wd, bd)


def _ln2_kernel(h_ref, y_ref, g_ref, b_ref, o_ref):
    o_ref[...] = _layernorm(DN_ALPHA * h_ref[...] + y_ref[...], g_ref[...], b_ref[...])


def _ln2(h, y, g, b, *, tm):
    t = h.shape[0]
    spec = pl.BlockSpec((tm, D_MODEL), lambda i: (i, 0))
    const = pl.BlockSpec((1, D_MODEL), lambda i: (0, 0))
    return pl.pallas_call(
        _ln2_kernel,
        grid=(t // tm,),
        in_specs=[spec, spec, const, const],
        out_specs=spec,
        out_shape=jax.ShapeDtypeStruct((t, D_MODEL), F32),
        compiler_params=pltpu.CompilerParams(dimension_semantics=("arbitrary",)),
        name="ln2",
    )(h, y, g, b)


def _group(x2d, weights, *, ps, tm, moe_tile, attn, mlstm_state, lc):
    z, kb, vt, kpages, vpages, gif, kmean = _project(
        x2d, weights["wz"], weights["wkv"], weights["wif"], weights["bif"], ps=ps, tm=tm)
    a = attn(z, kb, vt, kmean.reshape(kmean.shape[0], W_A), kpages, vpages)
    t_seq = x2d.shape[0] // mlstm_state[0].shape[0]
    cps = MLSTM_CHUNKS_PER_STEP if (t_seq // lc) % MLSTM_CHUNKS_PER_STEP == 0 else 1
    mh, c_t, n_t, m_t = _mlstm(z, gif, *mlstm_state, lc=lc, cps=cps)
    h, hb, gates, gates_t, cnt = _finish(
        x2d, a, mh, z, weights["wa"], weights["wm"], weights["wo"], weights["g1"], weights["b1"],
        weights["wr"], weights["br"],
        tm=FINISH_ROWS if x2d.shape[0] % FINISH_ROWS == 0 else ROUTER_ROWS)
    y = _moe(hb, gates, gates_t, cnt.reshape(cnt.shape[0], LANES),
             weights["wg"], weights["bg"], weights["wu"], weights["bu"], weights["wd"], weights["bd"],
             tt=moe_tile)
    out = _ln2(h, y, weights["g2"], weights["b2"], tm=tm)
    return out, kpages, vpages, c_t, n_t, m_t[..., 0]


def _hi_lo(w):
    hi = w.astype(BF16)
    return jnp.concatenate([hi, (w - hi.astype(F32)).astype(BF16)], axis=1)


def kernel(x_prompt, x_sample, cache_k, cache_v, page_table, state_C, state_n, state_m,
           w_in, b_if, w_branch_a, w_branch_m, w_out, ln1_g, ln1_b,
           w_router, b_router, w_gate, b_gate, w_up, b_up, w_down, b_down, ln2_g, ln2_b):
    depth = w_in.shape[0]
    assert depth == 1, "single-layer step"
    bp, seq, _ = x_prompt.shape
    db, dseq, _ = x_sample.shape
    l = 0
    w = w_in[l]
    o_qa, o_ka, o_va, o_qm, o_om_end = 0, W_A, 2 * W_A, 3 * W_A, 3 * W_A + 4 * W_M
    o_if = o_om_end
    o_ga = o_if + 2 * H_M
    weights = {
        "wz": jnp.concatenate([w[:, o_ga:], w[:, o_qa:o_ka], w[:, o_qm:o_om_end]], axis=1).astype(BF16),
        "wkv": w[:, o_ka:o_qm].astype(BF16),
        "wif": _hi_lo(jnp.pad(w[:, o_if:o_ga], ((0, 0), (0, LANES - 2 * H_M)))),
        "bif": jnp.pad(b_if[l], (0, LANES - 2 * H_M))[None, :],
        "wa": w_branch_a[l].astype(BF16),
        "wm": w_branch_m[l].astype(BF16),
        "wo": w_out[l].astype(BF16),
        "g1": ln1_g[l][None, :], "b1": ln1_b[l][None, :],
        "wr": jnp.pad(w_router[l], ((0, 0), (0, LANES - N_EXPERTS))),
        "br": jnp.pad(b_router[l], (0, LANES - N_EXPERTS), constant_values=NEG_INF)[None, :],
        "wg": w_gate[l].astype(BF16), "bg": b_gate[l][:, None, :],
        "wu": w_up[l].astype(BF16), "bu": b_up[l][:, None, :],
        "wd": w_down[l].astype(BF16), "bd": b_down[l][:, None, :],
        "g2": ln2_g[l][None, :], "b2": ln2_b[l][None, :],
    }
    assert bp == 1, "prompt batch of one sequence"
    zero_state = (jnp.zeros((bp, H_M, DH_M, DH_M), F32), jnp.zeros((bp, H_M, DH_M), F32),
                  jnp.zeros((bp, H_M), F32))
    yp, kp, vp, cp, np_, mp = _group(
        x_prompt.reshape(bp * seq, D_MODEL), weights, ps=PAGE_SIZE, tm=256,
        moe_tile=min(MOE_TILE, bp * seq),
        attn=lambda z, kb, vt, kmean, kpg, vpg: _moba_prompt(z, kb, vt, kmean),
        mlstm_state=zero_state, lc=math.gcd(seq, MLSTM_CHUNK))
    ck = jnp.swapaxes(cache_k.reshape(cache_k.shape[1:]), -1, -2)
    cv = jnp.swapaxes(cache_v.reshape(cache_v.shape[1:]), -1, -2)
    ys, ks, vs, cs, ns, ms = _group(
        x_sample.reshape(db * dseq, D_MODEL), weights, ps=dseq, tm=256,
        moe_tile=min(MOE_TILE, db * dseq),
        attn=lambda z, kb, vt, kmean, kpg, vpg: _moba_sample(z, kpg, vpg, ck, cv, page_table),
        mlstm_state=(state_C[l], state_n[l], state_m[l]), lc=math.gcd(dseq, MLSTM_CHUNK))
    n_pg = seq // PAGE_SIZE
    return (yp.reshape(bp, seq, D_MODEL), ys.reshape(db, dseq, D_MODEL),
            kp.reshape(1, bp, n_pg, H_A, PAGE_SIZE, DH_A), vp.reshape(1, bp, n_pg, H_A, PAGE_SIZE, DH_A),
            ks.reshape(1, db, H_A, dseq, DH_A), vs.reshape(1, db, H_A, dseq, DH_A),
            cp[None], np_[None], mp[None], cs[None], ns[None], ms[None])
```

```python
import functools
import math

import jax
import jax.numpy as jnp
from jax import lax
from jax.experimental import pallas as pl
from jax.experimental.pallas import tpu as pltpu

F32 = jnp.float32
BF16 = jnp.bfloat16
HIGHEST = lax.Precision.HIGHEST

D_MODEL = 1024
PAGE_SIZE = 128
H_A = 8
DH_A = 64
W_A = H_A * DH_A
MOBA_BLOCK = 256
MOBA_TOPK = 3
H_M = 4
DH_M = 128
W_M = H_M * DH_M
MLSTM_CHUNK = 64
N_EXPERTS = 32
TOP_K = 4
SWIGLU_LIMIT = 7.0
SWIGLU_ALPHA = 1.702
DN_ALPHA = 2.0 ** 0.25
LN_EPS = 1e-5
NEG_INF = -1e30

LANES = 128
Z_GA, Z_GM, Z_QA, Z_QM, Z_KM, Z_VM, Z_OM = 0, 1024, 2048, 2560, 3072, 3584, 4096
Z_WIDTH = 4608
VT_ROWS = 80
VMEM_LIMIT = 56 * 1024 * 1024


def _nt_dot(a, b, precision=None):
    return lax.dot_general(a, b, (((1,), (1,)), ((), ())), precision=precision,
                           preferred_element_type=F32)


def _dot(a, b, precision=None):
    return jnp.dot(a, b, precision=precision, preferred_element_type=F32)


def _top_mask(scores, idx_f, k, axis=-1):
    n = scores.shape[axis]
    sel = jnp.zeros(scores.shape, jnp.bool_)
    work = scores
    first_max = None
    for _ in range(k):
        mx = jnp.max(work, axis=axis, keepdims=True)
        if first_max is None:
            first_max = mx
        cand = work == mx
        pick_idx = jnp.min(jnp.where(cand, idx_f, float(n)), axis=axis, keepdims=True)
        pick = idx_f == pick_idx
        sel = jnp.logical_or(sel, pick)
        work = jnp.where(pick, -jnp.inf, work)
    return sel, first_max


def _proj_kernel(x_ref, wz_ref, wkv_ref, wif_ref, bif_ref,
                 z_ref, kb_ref, vt_ref, kp_ref, vp_ref, gif_ref, kmean_ref, *, ps):
    x = x_ref[...]
    tm = x.shape[0]
    xb = x.astype(BF16)
    x_lo = (x - xb.astype(F32)).astype(BF16)
    w_hi = wif_ref[:, :LANES]
    gif_ref[...] = (_dot(xb, w_hi) + _dot(xb, wif_ref[:, LANES:]) + _dot(x_lo, w_hi)) + bif_ref[...]
    for c in range(Z_WIDTH // 512):
        z_ref[:, c * 512:(c + 1) * 512] = _dot(xb, wz_ref[:, c * 512:(c + 1) * 512])
    for c, pref in ((0, kp_ref), (1, vp_ref)):
        r = _dot(xb, wkv_ref[:, c * W_A:(c + 1) * W_A])
        if c == 0:
            kb_ref[...] = r.astype(BF16)
        else:
            rt = r.T
            tail = (lax.broadcasted_iota(jnp.int32, (VT_ROWS - DH_A, tm), 0) == 0).astype(F32)
            for h in range(H_A):
                vt_ref[0, h] = jnp.concatenate([rt[h * DH_A:(h + 1) * DH_A], tail], axis=0).astype(BF16)
        for h in range(H_A):
            pref[:, h, :, :] = r[:, h * DH_A:(h + 1) * DH_A].reshape(tm // ps, ps, DH_A)
        if c == 0:
            kmean_ref[...] = jnp.mean(r.reshape(tm // MOBA_BLOCK, MOBA_BLOCK, W_A), axis=1, keepdims=True)


def _project(x2d, wz, wkv, wif, bif, *, ps, tm):
    t = x2d.shape[0]
    assert tm == MOBA_BLOCK, "one MoBA block of keys per row tile"
    grid = (t // tm,)
    const = lambda i: (0, 0)
    return pl.pallas_call(
        functools.partial(_proj_kernel, ps=ps),
        grid=grid,
        in_specs=[
            pl.BlockSpec((tm, D_MODEL), lambda i: (i, 0)),
            pl.BlockSpec(wz.shape, const),
            pl.BlockSpec(wkv.shape, const),
            pl.BlockSpec(wif.shape, const),
            pl.BlockSpec(bif.shape, const),
        ],
        out_specs=[
            pl.BlockSpec((tm, Z_WIDTH), lambda i: (i, 0)),
            pl.BlockSpec((tm, W_A), lambda i: (i, 0)),
            pl.BlockSpec((1, H_A, VT_ROWS, tm), lambda i: (i, 0, 0, 0)),
            pl.BlockSpec((tm // ps, H_A, ps, DH_A), lambda i: (i, 0, 0, 0)),
            pl.BlockSpec((tm // ps, H_A, ps, DH_A), lambda i: (i, 0, 0, 0)),
            pl.BlockSpec((tm, LANES), lambda i: (i, 0)),
            pl.BlockSpec((tm // MOBA_BLOCK, 1, W_A), lambda i: (i, 0, 0)),
        ],
        out_shape=[
            jax.ShapeDtypeStruct((t, Z_WIDTH), F32),
            jax.ShapeDtypeStruct((t, W_A), BF16),
            jax.ShapeDtypeStruct((t // tm, H_A, VT_ROWS, tm), BF16),
            jax.ShapeDtypeStruct((t // ps, H_A, ps, DH_A), F32),
            jax.ShapeDtypeStruct((t // ps, H_A, ps, DH_A), F32),
            jax.ShapeDtypeStruct((t, LANES), F32),
            jax.ShapeDtypeStruct((t // MOBA_BLOCK, 1, W_A), F32),
        ],
        compiler_params=pltpu.CompilerParams(
            dimension_semantics=("arbitrary",), vmem_limit_bytes=VMEM_LIMIT),
        name="proj",
    )(x2d, wz, wkv, wif, bif)


PAST_UNROLL = 16


LOG2E = 1.4426950408889634


def _split3(x):
    hi = x.astype(BF16)
    r1 = x - hi.astype(F32)
    mid = r1.astype(BF16)
    lo = (r1 - mid.astype(F32)).astype(BF16)
    return hi, mid, lo


def _moba_prompt_kernel(q_ref, k_ref, vt_ref, km_ref, o_ref, m_sc, acc_sc, sel_sc):
    hp = pl.program_id(0)
    j = pl.program_id(1)
    bq = MOBA_BLOCK
    nb = km_ref.shape[0]
    q = q_ref[...]
    lane = lax.broadcasted_iota(jnp.int32, (1, LANES), 1)
    blk_f = lax.broadcasted_iota(jnp.int32, (nb, 1), 0).astype(F32)
    jf = j.astype(F32)
    rk = lax.broadcasted_iota(jnp.int32, (bq, bq), 0)
    rq = lax.broadcasted_iota(jnp.int32, (bq, bq), 1)
    causal = rq >= rk
    pos = lax.broadcasted_iota(jnp.int32, (bq, 1), 0).astype(F32)
    k_extra = jnp.where(lane < 3, pos, jnp.where(lane < 6, 1.0, 0.0)).astype(BF16)
    k_own = jnp.concatenate([k_ref[pl.ds(pl.multiple_of(j * bq, bq), bq), :], k_extra], axis=1)
    vt_own = vt_ref[j]
    q_augs = []
    coefs = []
    for hh in range(2):
        hmask = jnp.logical_and(lane >= hh * DH_A, lane < (hh + 1) * DH_A)
        qh = jnp.where(hmask, q, 0.0)
        head_f = (2 * hp + hh + 1).astype(F32)
        coef = jnp.exp2(jnp.zeros((1, 1), F32) - head_f) * LOG2E
        coefs.append(coef)
        scores = _nt_dot(km_ref[...], qh, precision=HIGHEST)
        scores = jnp.where(blk_f < jf, scores, NEG_INF)
        sel, _ = _top_mask(scores, blk_f, min(MOBA_TOPK, nb), axis=0)
        sel = jnp.logical_and(sel, blk_f < jf)
        sel_sc[hh] = jnp.where(sel, 1.0, 0.0)
        c3 = _split3(jnp.broadcast_to(coef, (bq, 1)))
        q3 = _split3(-(coef * pos))
        q_extra = jnp.zeros((bq, LANES), F32)
        for li, term in enumerate(c3 + q3):
            q_extra = jnp.where(lane == li, term.astype(F32), q_extra)
        qb = (qh * (DH_A ** -0.5 * LOG2E)).astype(BF16)
        q_aug = jnp.concatenate([qb, q_extra.astype(BF16)], axis=1)
        q_augs.append(q_aug)
        logits = jnp.where(causal, _nt_dot(k_own, q_aug), NEG_INF)
        m = jnp.max(logits, axis=0, keepdims=True)
        p = jnp.exp2(logits - m)
        m_sc[hh] = m
        acc_sc[hh] = _dot(vt_own[hh], p.astype(BF16))

    def body(it, carry):
        ns, kns, vtns, offs, valids = [], [], [], [], []
        for u in range(PAST_UNROLL):
            n_raw = it * PAST_UNROLL + u
            n = jnp.minimum(n_raw, j - 1)
            ns.append(n)
            valids.append(n_raw < j)
            kns.append(jnp.concatenate([k_ref[pl.ds(pl.multiple_of(n * bq, bq), bq), :], k_extra], axis=1))
            vtns.append(vt_ref[n])
            offs.append((j - n).astype(F32) * float(bq))
        logits = [[], []]
        for hh in range(2):
            for u in range(PAST_UNROLL):
                picked = sel_sc[hh, pl.ds(ns[u], 1), :]
                keep = jnp.logical_and(picked > 0.5, valids[u])
                rb = jnp.where(keep, -(coefs[hh] * offs[u]), NEG_INF)
                logits[hh].append(_nt_dot(kns[u], q_augs[hh]) + rb)
        m_news, acc_news = [], []
        for hh in range(2):
            m_old = m_sc[hh]
            m_new = m_old
            for lg in logits[hh]:
                m_new = jnp.maximum(m_new, jnp.max(lg, axis=0, keepdims=True))
            m_news.append(m_new)
            acc_news.append(jnp.exp2(m_old - m_new) * acc_sc[hh])
        for hh in range(2):
            acc_new = acc_news[hh]
            for u in range(PAST_UNROLL):
                p = jnp.exp2(logits[hh][u] - m_news[hh])
                acc_new = acc_new + _dot(vtns[u][hh], p.astype(BF16))
            acc_sc[hh] = acc_new
            m_sc[hh] = m_news[hh]
        return carry

    lax.fori_loop(0, lax.div(j + (PAST_UNROLL - 1), PAST_UNROLL), body, 0)
    acc0 = acc_sc[0]
    acc1 = acc_sc[1]
    o0 = acc0[:DH_A] / acc0[DH_A:DH_A + 1, :]
    o1 = acc1[:DH_A] / acc1[DH_A:DH_A + 1, :]
    o_ref[...] = jnp.concatenate([o0, o1], axis=0).T


def _moba_prompt(z, kb, vt, kmean):
    t = z.shape[0]
    nb = t // MOBA_BLOCK
    bq = MOBA_BLOCK
    qcol = Z_QA // LANES
    return pl.pallas_call(
        _moba_prompt_kernel,
        grid=(H_A // 2, nb),
        in_specs=[
            pl.BlockSpec((bq, LANES), lambda hp, j: (j, qcol + hp)),
            pl.BlockSpec((t, LANES), lambda hp, j: (0, hp)),
            pl.BlockSpec((nb, 2, VT_ROWS, bq), lambda hp, j: (0, hp, 0, 0)),
            pl.BlockSpec((nb, LANES), lambda hp, j: (0, hp)),
        ],
        out_specs=pl.BlockSpec((bq, LANES), lambda hp, j: (j, hp)),
        out_shape=jax.ShapeDtypeStruct((t, W_A), F32),
        scratch_shapes=[
            pltpu.VMEM((2, 1, bq), F32),
            pltpu.VMEM((2, VT_ROWS, bq), F32),
            pltpu.VMEM((2, nb, bq), F32),
        ],
        compiler_params=pltpu.CompilerParams(
            dimension_semantics=("arbitrary", "arbitrary"), vmem_limit_bytes=VMEM_LIMIT),
        name="moba_prompt",
    )(z, kb, vt, kmean)


def _moba_sample_kernel(pt_ref, q_ref, kn_ref, vn_ref, *refs, ppg, past):
    k_refs = refs[:ppg]
    v_refs = refs[ppg:2 * ppg]
    o_ref = refs[2 * ppg]
    qs_sc, qf_sc, ksum_sc, m_sc, l_sc, o_sc = refs[2 * ppg + 1:]
    g = pl.program_id(1)
    ng = pl.num_programs(1)
    t = q_ref.shape[0]
    rows = H_A * t
    npb = o_sc.shape[0]
    ppb = MOBA_BLOCK // PAGE_SIZE
    row_i = lax.broadcasted_iota(jnp.int32, (rows, 1), 0)
    slope = jnp.exp2(-(lax.div(row_i, t) + 1).astype(F32))
    tok = lax.rem(row_i, t)
    blk_lane = lax.broadcasted_iota(jnp.int32, (1, LANES), 1)
    feat_head = lax.div(lax.broadcasted_iota(jnp.int32, (1, W_A), 1), DH_A)

    @pl.when(g == 0)
    def _():
        q = q_ref[...]
        qbd = jnp.concatenate([jnp.where(feat_head == h, q, 0.0) for h in range(H_A)], axis=0)
        qf_sc[...] = qbd
        qs_sc[...] = qbd * (DH_A ** -0.5)
        ksum_sc[...] = jnp.zeros(ksum_sc.shape, F32)
        m_sc[...] = jnp.zeros(m_sc.shape, F32)
        l_sc[...] = jnp.zeros(l_sc.shape, F32)

    qs = qs_sc[...]
    rk = lax.broadcasted_iota(jnp.int32, (1, MOBA_BLOCK), 1)
    ksum_all = ksum_sc[...]
    m_all = m_sc[...]
    l_all = l_sc[...]
    nblk = ppg // ppb
    s_blks = []
    for b in range(nblk):
        kt = jnp.concatenate([k_refs[b * ppb + p][...].reshape(W_A, PAGE_SIZE) for p in range(ppb)], axis=1)
        s_blks.append(_dot(qs, kt))
        ksum = jnp.sum(kt, axis=-1, keepdims=True)
        ksum_all = jnp.where(blk_lane == g * nblk + b, ksum, ksum_all)
    p_blks = []
    for b in range(nblk):
        n = g * nblk + b
        dist = ((past - n * MOBA_BLOCK) + tok - rk).astype(F32)
        logits = s_blks[b] - slope * dist
        m_b = jnp.max(logits, axis=-1, keepdims=True)
        p_ = jnp.exp(logits - m_b)
        p_blks.append(p_)
        m_all = jnp.where(blk_lane == n, m_b, m_all)
        l_all = jnp.where(blk_lane == n, jnp.sum(p_, axis=-1, keepdims=True), l_all)
    for b in range(nblk):
        vt = jnp.concatenate([v_refs[b * ppb + p][...].reshape(W_A, PAGE_SIZE) for p in range(ppb)], axis=1)
        o_sc[g * nblk + b] = _nt_dot(p_blks[b], vt)
    ksum_sc[...] = ksum_all
    m_sc[...] = m_all
    l_sc[...] = l_all

    @pl.when(g == ng - 1)
    def _():
        kmean = ksum_sc[...] * (1.0 / MOBA_BLOCK)
        scores = _dot(qf_sc[...], kmean, precision=HIGHEST)
        scores = jnp.where(blk_lane < npb, scores, -jnp.inf)
        sel, _ = _top_mask(scores, blk_lane.astype(F32), min(MOBA_TOPK, npb))
        kn = jnp.concatenate([kn_ref[h] for h in range(H_A)], axis=1)
        vn = jnp.concatenate([vn_ref[h] for h in range(H_A)], axis=1)
        s_own = _nt_dot(qs, kn)
        d_own = (tok - lax.broadcasted_iota(jnp.int32, (1, t), 1)).astype(F32)
        lg_own = jnp.where(d_own >= 0, s_own - slope * d_own, NEG_INF)
        m_all = m_sc[...]
        m_sel = jnp.max(jnp.where(sel, m_all, -jnp.inf), axis=-1, keepdims=True)
        m_fin = jnp.maximum(m_sel, jnp.max(lg_own, axis=-1, keepdims=True))
        w = jnp.where(sel, jnp.exp(m_all - m_fin), 0.0)
        p_own = jnp.exp(lg_own - m_fin)
        l_fin = jnp.sum(w * l_sc[...], axis=-1, keepdims=True) + jnp.sum(p_own, axis=-1, keepdims=True)
        o = _dot(p_own, vn)
        for n in range(npb):
            o = o + w[:, n:n + 1] * o_sc[n]
        o = o / l_fin
        out = jnp.zeros((t, W_A), F32)
        for h in range(H_A):
            out = out + jnp.where(feat_head == h, o[h * t:(h + 1) * t, :], 0.0)
        o_ref[...] = out


SAMPLE_PAGES_PER_STEP = 32


def _moba_sample(z, k_new, v_new, cache_k, cache_v, page_table):
    db, n_pages = page_table.shape
    t = z.shape[0] // db
    ppb = MOBA_BLOCK // PAGE_SIZE
    npb = n_pages // ppb
    assert n_pages == npb * ppb, "cached pages must fill whole MoBA blocks"
    ppg = SAMPLE_PAGES_PER_STEP if n_pages % SAMPLE_PAGES_PER_STEP == 0 else ppb
    ng = n_pages // ppg

    def page_map(b, g, pt, *, i):
        return (pt[b * n_pages + g * ppg + i], 0, 0, 0)

    page_specs = [pl.BlockSpec((None, H_A, DH_A, PAGE_SIZE), functools.partial(page_map, i=i))
                  for i in range(ppg)]
    grid_spec = pltpu.PrefetchScalarGridSpec(
        num_scalar_prefetch=1,
        grid=(db, ng),
        in_specs=[
            pl.BlockSpec((t, W_A), lambda b, g, pt: (b, Z_QA // W_A)),
            pl.BlockSpec((None, H_A, t, DH_A), lambda b, g, pt: (b, 0, 0, 0)),
            pl.BlockSpec((None, H_A, t, DH_A), lambda b, g, pt: (b, 0, 0, 0)),
        ] + page_specs + page_specs,
        out_specs=pl.BlockSpec((t, W_A), lambda b, g, pt: (b, 0)),
        scratch_shapes=[
            pltpu.VMEM((H_A * t, W_A), F32),
            pltpu.VMEM((H_A * t, W_A), F32),
            pltpu.VMEM((W_A, LANES), F32),
            pltpu.VMEM((H_A * t, LANES), F32),
            pltpu.VMEM((H_A * t, LANES), F32),
            pltpu.VMEM((npb, H_A * t, W_A), F32),
        ],
    )
    return pl.pallas_call(
        functools.partial(_moba_sample_kernel, ppg=ppg, past=n_pages * PAGE_SIZE),
        grid_spec=grid_spec,
        out_shape=jax.ShapeDtypeStruct((db * t, W_A), F32),
        compiler_params=pltpu.CompilerParams(
            dimension_semantics=("arbitrary", "arbitrary"), vmem_limit_bytes=VMEM_LIMIT),
        name="moba_sample",
    )(page_table.reshape(-1), z, k_new, v_new, *([cache_k] * ppg), *([cache_v] * ppg))


def _mlstm_kernel(q_ref, k_ref, v_ref, g_ref, c0_ref, n0_ref, m0_ref,
                  h_ref, c_ref, n_ref, m_ref, c_sc, n_sc, m_sc, *, lc, cps):
    step = pl.program_id(1)
    nsteps = pl.num_programs(1)
    lp = max(lc, MLSTM_CHUNK)

    @pl.when(step == 0)
    def _():
        c_sc[...] = c0_ref[0]
        n_sc[...] = n0_ref[0]
        m_sc[...] = jnp.broadcast_to(m0_ref[0], m_sc.shape)

    def padded(a):
        if a.shape[0] == lp:
            return a
        return jnp.concatenate([a, jnp.zeros((lp - a.shape[0],) + a.shape[1:], a.dtype)], axis=0)

    row = lax.broadcasted_iota(jnp.int32, (lp, lp), 0)
    col = lax.broadcasted_iota(jnp.int32, (lp, lp), 1)
    causal = row >= col
    tril = jnp.where(causal, 1.0, 0.0).astype(BF16)
    pick_8 = jnp.where(lax.broadcasted_iota(jnp.int32, (16, LANES), 0)
                       == lax.broadcasted_iota(jnp.int32, (16, LANES), 1), 1.0, 0.0).astype(BF16)
    gate_lane = lax.broadcasted_iota(jnp.int32, (1, LANES), 1)
    eye = jnp.where(lax.broadcasted_iota(jnp.int32, (DH_M, DH_M), 0)
                    == lax.broadcasted_iota(jnp.int32, (DH_M, DH_M), 1), 1.0, 0.0).astype(BF16)
    rvalid = lax.broadcasted_iota(jnp.int32, (lp, 1), 0) < lc
    cvalid = lax.broadcasted_iota(jnp.int32, (1, lp), 1) < lc
    m_all = m_sc[...]
    n_all = n_sc[...]
    c_all = [c_sc[h] for h in range(H_M)]
    pre = []
    for ci in range(cps):
        rs = slice(ci * lc, (ci + 1) * lc)
        q = padded(q_ref[rs, :])
        k = padded(k_ref[rs, :]) * (DH_M ** -0.5)
        v = padded(v_ref[rs, :])
        gts = padded(g_ref[rs, :])
        lf = jax.nn.log_sigmoid(gts)
        bcum = sum(_dot(tril, part) for part in _split3(lf))
        comb_t = sum(_nt_dot(pick_8, part) for part in _split3(jnp.where(gate_lane < H_M, gts, bcum)))
        heads = []
        for h in range(H_M):
            qh = q[:, h * DH_M:(h + 1) * DH_M]
            kh = k[:, h * DH_M:(h + 1) * DH_M]
            vh = v[:, h * DH_M:(h + 1) * DH_M]
            bc = bcum[:, H_M + h:H_M + h + 1]
            ic = gts[:, h:h + 1]
            br = comb_t[H_M + h:H_M + h + 1, :]
            ir = comb_t[h:h + 1, :]
            dmat = jnp.where(causal, bc - br + ir, -jnp.inf)
            qhb = qh.astype(BF16)
            wide = lambda col: jnp.broadcast_to(col, (lp, LANES))
            heads.append(dict(
                qh=qh, qhb=qhb, kh=kh, vhb=vh.astype(BF16), br=br, ir=ir, dmat=dmat,
                bc_w=wide(bc), ic_w=wide(ic),
                m_intra_w=wide(jnp.max(dmat, axis=-1, keepdims=True)),
                qk=_nt_dot(qhb, kh.astype(BF16)),
                kh_t=sum(_nt_dot(eye, part) for part in _split3(kh)),
            ))
        pre.append(heads)
    for ci in range(cps):
        rs = slice(ci * lc, (ci + 1) * lc)
        m_out, n_out, c_out = [], [], []
        for h in range(H_M):
            d = pre[ci][h]
            bc_w, ic_w, dmat = d["bc_w"], d["ic_w"], d["dmat"]
            m_prev = m_all[h:h + 1, :]
            c_prev = c_all[h]
            n_prev = n_all[h:h + 1, :]
            a_inter = bc_w + m_prev
            m_t = jnp.maximum(a_inter, d["m_intra_w"])
            w_inter = jnp.exp(a_inter - m_t)
            s = d["qk"] * jnp.exp(dmat - m_t[:, :lp])
            num = w_inter * _dot(d["qhb"], c_prev.astype(BF16)) + _dot(s.astype(BF16), d["vhb"])
            den = (w_inter[:, :1] * jnp.sum(d["qh"] * n_prev, axis=-1, keepdims=True)
                   + jnp.sum(s, axis=-1, keepdims=True))
            h_out = num / jnp.maximum(jnp.abs(den), jnp.exp(-m_t[:, :1]))
            h_ref[rs, h * DH_M:(h + 1) * DH_M] = h_out[:lc]
            m_new = m_t[lc - 1:lc, :]
            b_last = bc_w[lc - 1:lc, :]
            g_c = jnp.exp(b_last + m_prev - m_new)
            g_s = jnp.where(rvalid, jnp.exp(b_last - bc_w + ic_w - m_new), 0.0)
            g_s_row = jnp.where(cvalid, jnp.exp(b_last[:, :lp] - d["br"] + d["ir"] - m_new[:, :lp]), 0.0)
            kg_t = (d["kh_t"] * g_s_row).astype(BF16)
            c_out.append(g_c * c_prev + _dot(kg_t, d["vhb"]))
            n_out.append(g_c * n_prev + jnp.sum(g_s * d["kh"], axis=0, keepdims=True))
            m_out.append(m_new)
        m_all = jnp.concatenate(m_out, axis=0)
        n_all = jnp.concatenate(n_out, axis=0)
        c_all = c_out
    for h in range(H_M):
        c_sc[h] = c_all[h]
    n_sc[...] = n_all
    m_sc[...] = m_all

    @pl.when(step == nsteps - 1)
    def _():
        c_ref[0] = c_sc[...]
        n_ref[0] = n_sc[...]
        m_ref[0] = m_sc[:, 0:1]


def _mlstm(z, gif, c0, n0, m0, *, lc, cps):
    b = c0.shape[0]
    t = z.shape[0] // b
    rows = lc * cps
    nc = t // rows
    assert nc * rows == t
    row = lambda bi, ci: bi * nc + ci
    wcol = lambda off: off // W_M
    return pl.pallas_call(
        functools.partial(_mlstm_kernel, lc=lc, cps=cps),
        grid=(b, nc),
        in_specs=[
            pl.BlockSpec((rows, W_M), lambda bi, ci: (row(bi, ci), wcol(Z_QM))),
            pl.BlockSpec((rows, W_M), lambda bi, ci: (row(bi, ci), wcol(Z_KM))),
            pl.BlockSpec((rows, W_M), lambda bi, ci: (row(bi, ci), wcol(Z_VM))),
            pl.BlockSpec((rows, LANES), lambda bi, ci: (row(bi, ci), 0)),
            pl.BlockSpec((1, H_M, DH_M, DH_M), lambda bi, ci: (bi, 0, 0, 0)),
            pl.BlockSpec((1, H_M, DH_M), lambda bi, ci: (bi, 0, 0)),
            pl.BlockSpec((1, H_M, 1), lambda bi, ci: (bi, 0, 0)),
        ],
        out_specs=[
            pl.BlockSpec((rows, W_M), lambda bi, ci: (row(bi, ci), 0)),
            pl.BlockSpec((1, H_M, DH_M, DH_M), lambda bi, ci: (bi, 0, 0, 0)),
            pl.BlockSpec((1, H_M, DH_M), lambda bi, ci: (bi, 0, 0)),
            pl.BlockSpec((1, H_M, 1), lambda bi, ci: (bi, 0, 0)),
        ],
        out_shape=[
            jax.ShapeDtypeStruct((b * t, W_M), F32),
            jax.ShapeDtypeStruct((b, H_M, DH_M, DH_M), F32),
            jax.ShapeDtypeStruct((b, H_M, DH_M), F32),
            jax.ShapeDtypeStruct((b, H_M, 1), F32),
        ],
        scratch_shapes=[
            pltpu.VMEM((H_M, DH_M, DH_M), F32),
            pltpu.VMEM((H_M, DH_M), F32),
            pltpu.VMEM((H_M, LANES), F32),
        ],
        compiler_params=pltpu.CompilerParams(
            dimension_semantics=("arbitrary", "arbitrary"), vmem_limit_bytes=VMEM_LIMIT),
        name="mlstm",
    )(z, z, z, gif, c0, n0, m0[..., None])


def _layernorm(y, g, b):
    mu = jnp.mean(y, axis=-1, keepdims=True)
    var = jnp.mean(jnp.square(y - mu), axis=-1, keepdims=True)
    return (y - mu) * lax.rsqrt(var + LN_EPS) * g + b


ROUTER_ROWS = 256
FINISH_ROWS = 512


def _finish_kernel(x_ref, a_ref, mh_ref, om_ref, ga_ref, gm_ref, wa_ref, wm_ref, wo_ref,
                   g1_ref, b1_ref, wr_ref, br_ref,
                   h_ref, hb_ref, gates_t_ref, cnt_ref):
    sub = ROUTER_ROWS
    parts = [slice(s * sub, (s + 1) * sub) for s in range(x_ref.shape[0] // sub)]
    a = [_dot(a_ref[p, :].astype(BF16), wa_ref[...]) for p in parts]
    m = [_dot((mh_ref[p, :] * jax.nn.sigmoid(om_ref[p, :])).astype(BF16), wm_ref[...]) for p in parts]
    merged = [jax.nn.sigmoid(ga_ref[p, :]) * a_ + jax.nn.sigmoid(gm_ref[p, :]) * m_
              for p, a_, m_ in zip(parts, a, m)]
    y = [DN_ALPHA * x_ref[p, :] + _dot(mg.astype(BF16), wo_ref[...]) for p, mg in zip(parts, merged)]
    h = [_layernorm(y_, g1_ref[...], b1_ref[...]) for y_ in y]
    for p, h_ in zip(parts, h):
        h_ref[p, :] = h_
        hb_ref[p, :] = h_.astype(BF16)
    logits = [_dot(h_, wr_ref[...], precision=HIGHEST) + br_ref[...] for h_ in h]
    lane_f = lax.broadcasted_iota(jnp.int32, (1, LANES), 1).astype(F32)
    for s, (p, lg) in enumerate(zip(parts, logits)):
        sel, top = _top_mask(lg, lane_f, TOP_K)
        e = jnp.where(sel, jnp.exp(lg - top), 0.0)
        gates = e / jnp.sum(e, axis=-1, keepdims=True)
        gates_t_ref[:, p] = gates.T
        cnt_ref[s] = jnp.sum(jnp.where(gates > 0.0, 1.0, 0.0), axis=0, keepdims=True).astype(jnp.int32)


def _finish(x2d, a, mh, z, wa, wm, wo, g1, b1, wr, br, *, tm):
    t = x2d.shape[0]
    nt = t // tm
    const = lambda i: (0, 0)
    return pl.pallas_call(
        _finish_kernel,
        grid=(nt,),
        in_specs=[
            pl.BlockSpec((tm, D_MODEL), lambda i: (i, 0)),
            pl.BlockSpec((tm, W_A), lambda i: (i, 0)),
            pl.BlockSpec((tm, W_M), lambda i: (i, 0)),
            pl.BlockSpec((tm, W_M), lambda i: (i, Z_OM // W_M)),
            pl.BlockSpec((tm, D_MODEL), lambda i: (i, Z_GA // D_MODEL)),
            pl.BlockSpec((tm, D_MODEL), lambda i: (i, Z_GM // D_MODEL)),
            pl.BlockSpec(wa.shape, const),
            pl.BlockSpec(wm.shape, const),
            pl.BlockSpec(wo.shape, const),
            pl.BlockSpec(g1.shape, const),
            pl.BlockSpec(b1.shape, const),
            pl.BlockSpec(wr.shape, const),
            pl.BlockSpec(br.shape, const),
        ],
        out_specs=[
            pl.BlockSpec((tm, D_MODEL), lambda i: (i, 0)),
            pl.BlockSpec((tm, D_MODEL), lambda i: (i, 0)),
            pl.BlockSpec((LANES, tm), lambda i: (0, i)),
            pl.BlockSpec((tm // ROUTER_ROWS, 1, LANES), lambda i: (i, 0, 0)),
        ],
        out_shape=[
            jax.ShapeDtypeStruct((t, D_MODEL), F32),
            jax.ShapeDtypeStruct((t, D_MODEL), BF16),
            jax.ShapeDtypeStruct((LANES, t), F32),
            jax.ShapeDtypeStruct((t // ROUTER_ROWS, 1, LANES), jnp.int32),
        ],
        compiler_params=pltpu.CompilerParams(
            dimension_semantics=("arbitrary",), vmem_limit_bytes=VMEM_LIMIT),
        name="finish",
    )(x2d, a, mh, z, z, z, wa, wm, wo, g1, b1, wr, br)


MLSTM_CHUNKS_PER_STEP = 4
MOE_TILE = 2048
MOE_SEG = 512
MOE_SEG_ROWS = 80
MOE_RANK_BLOCK = ROUTER_ROWS


def _moe_kernel(cnt_ref, hb_ref, gt_ref, wg_ref, bg_ref, wu_ref, bu_ref, wd_ref, bd_ref,
                y_ref, posr_sc, *, cnt_per_seg, seg):
    i = pl.program_id(0)
    e = pl.program_id(1)
    tt = hb_ref.shape[0]
    nseg = tt // seg
    rseg = MOE_SEG_ROWS

    @pl.when(e == 0)
    def _():
        y_ref[...] = jnp.zeros(y_ref.shape, F32)

    cnt_max = 0
    for sg in range(nseg):
        first = (i * nseg + sg) * cnt_per_seg
        cnt_seg = cnt_ref[first, e]
        for r in range(1, cnt_per_seg):
            cnt_seg = cnt_seg + cnt_ref[first + r, e]
        cnt_max = jnp.maximum(cnt_max, cnt_seg)

    @pl.when(cnt_max > 0)
    def _():
        grow = gt_ref[pl.ds(lax.rem(e, 8), 1), :]
        selr = jnp.where(grow > 0.0, 1.0, 0.0)
        ru = lax.broadcasted_iota(jnp.int32, (LANES, LANES), 0)
        cu = lax.broadcasted_iota(jnp.int32, (LANES, LANES), 1)
        strict_upper = jnp.where(ru < cu, 1.0, 0.0).astype(BF16)
        for ci in range(tt // LANES):
            if (ci * LANES) % seg == 0:
                offr = jnp.zeros((16, 1), F32)
            sr = jnp.broadcast_to(selr[:, ci * LANES:(ci + 1) * LANES], (16, LANES))
            rank = _dot(sr.astype(BF16), strict_upper) + offr
            posr_sc[:, ci * LANES:(ci + 1) * LANES] = jnp.where(sr > 0.0, rank, -1.0)
            offr = offr + jnp.sum(sr, axis=-1, keepdims=True)
        slot_sub = lax.broadcasted_iota(jnp.int32, (rseg, LANES), 0).astype(F32)
        no_slot = jnp.zeros((LANES - rseg, seg), F32)

        def one_pass(s, carry):
            base = (s * rseg).astype(F32)
            xs_parts, gs_parts, scatter = [], [], []
            for sg in range(nseg):
                pm_chunks = []
                for ci in range(sg * seg // LANES, (sg + 1) * seg // LANES):
                    pr = posr_sc[0:1, ci * LANES:(ci + 1) * LANES] - base
                    pm_chunks.append(pr == slot_sub)
                pmask = jnp.concatenate(pm_chunks, axis=1)
                gs_parts.append(jnp.sum(jnp.where(pmask, grow[:, sg * seg:(sg + 1) * seg], 0.0),
                                        axis=-1, keepdims=True))
                onehot = jnp.where(pmask, 1.0, 0.0)
                xs_parts.append(_dot(onehot.astype(BF16), hb_ref[sg * seg:(sg + 1) * seg, :]).astype(BF16))
                scatter.append(jnp.concatenate([onehot, no_slot], axis=0).T.astype(BF16))
            xs = jnp.concatenate(xs_parts, axis=0)
            gs = jnp.concatenate(gs_parts, axis=0)
            gp = jnp.minimum(_dot(xs, wg_ref[0]) + bg_ref[0], SWIGLU_LIMIT)
            up = jnp.clip(_dot(xs, wu_ref[0]) + bu_ref[0], -SWIGLU_LIMIT, SWIGLU_LIMIT)
            hdn = (up + 1.0) * (gp * jax.nn.sigmoid(SWIGLU_ALPHA * gp))
            yb = ((_dot(hdn.astype(BF16), wd_ref[0]) + bd_ref[0]) * gs).astype(BF16)
            pad = jnp.zeros((LANES - rseg, yb.shape[1]), BF16)
            for sg in range(nseg):
                rows = slice(sg * seg, (sg + 1) * seg)
                y_seg = jnp.concatenate([yb[sg * rseg:(sg + 1) * rseg], pad], axis=0)
                y_ref[rows, :] += _dot(scatter[sg], y_seg)
            return carry

        lax.fori_loop(0, lax.div(cnt_max + (rseg - 1), rseg), one_pass, 0)


def _moe(hb, gates_t, cnt, wg, bg, wu, bu, wd, bd, *, tt):
    t = hb.shape[0]
    nt = t // tt
    seg = min(MOE_SEG, tt)
    cnt_per_seg = cnt.shape[0] * seg // t
    assert cnt_per_seg * t == cnt.shape[0] * seg and tt % seg == 0 and seg % MOE_RANK_BLOCK == 0
    wspec = pl.BlockSpec((1, D_MODEL, D_MODEL), lambda i, e, c: (e, 0, 0))
    bspec = pl.BlockSpec((1, 1, D_MODEL), lambda i, e, c: (e, 0, 0))
    grid_spec = pltpu.PrefetchScalarGridSpec(
        num_scalar_prefetch=1,
        grid=(nt, N_EXPERTS),
        in_specs=[
            pl.BlockSpec((tt, D_MODEL), lambda i, e, c: (i, 0)),
            pl.BlockSpec((8, tt), lambda i, e, c: (e // 8, i)),
            wspec, bspec, wspec, bspec, wspec, bspec,
        ],
        out_specs=pl.BlockSpec((tt, D_MODEL), lambda i, e, c: (i, 0)),
        scratch_shapes=[
            pltpu.VMEM((16, tt), F32),
        ],
    )
    return pl.pallas_call(
        functools.partial(_moe_kernel, cnt_per_seg=cnt_per_seg, seg=seg),
        grid_spec=grid_spec,
        out_shape=jax.ShapeDtypeStruct((t, D_MODEL), F32),
        compiler_params=pltpu.CompilerParams(
            dimension_semantics=("arbitrary", "arbitrary"), vmem_limit_bytes=VMEM_LIMIT),
        name="moe",
    )(cnt, hb, gates_t, wg, bg, wu, bu, wd, bd)


def _ln2_kernel(h_ref, y_ref, g_ref, b_ref, o_ref):
    o_ref[...] = _layernorm(DN_ALPHA * h_ref[...] + y_ref[...], g_ref[...], b_ref[...])


def _ln2(h, y, g, b, *, tm):
    t = h.shape[0]
    spec = pl.BlockSpec((tm, D_MODEL), lambda i: (i, 0))
    const = pl.BlockSpec((1, D_MODEL), lambda i: (0, 0))
    return pl.pallas_call(
        _ln2_kernel,
        grid=(t // tm,),
        in_specs=[spec, spec, const, const],
        out_specs=spec,
        out_shape=jax.ShapeDtypeStruct((t, D_MODEL), F32),
        compiler_params=pltpu.CompilerParams(dimension_semantics=("arbitrary",)),
        name="ln2",
    )(h, y, g, b)


def _group(x2d, weights, *, ps, tm, moe_tile, attn, mlstm_state, lc):
    z, kb, vt, kpages, vpages, gif, kmean = _project(
        x2d, weights["wz"], weights["wkv"], weights["wif"], weights["bif"], ps=ps, tm=tm)
    a = attn(z, kb, vt, kmean.reshape(kmean.shape[0], W_A), kpages, vpages)
    t_seq = x2d.shape[0] // mlstm_state[0].shape[0]
    cps = MLSTM_CHUNKS_PER_STEP if (t_seq // lc) % MLSTM_CHUNKS_PER_STEP == 0 else 1
    mh, c_t, n_t, m_t = _mlstm(z, gif, *mlstm_state, lc=lc, cps=cps)
    h, hb, gates_t, cnt = _finish(
        x2d, a, mh, z, weights["wa"], weights["wm"], weights["wo"], weights["g1"], weights["b1"],
        weights["wr"], weights["br"],
        tm=FINISH_ROWS if x2d.shape[0] % FINISH_ROWS == 0 else ROUTER_ROWS)
    y = _moe(hb, gates_t, cnt.reshape(cnt.shape[0], LANES),
             weights["wg"], weights["bg"], weights["wu"], weights["bu"], weights["wd"], weights["bd"],
             tt=moe_tile)
    out = _ln2(h, y, weights["g2"], weights["b2"], tm=tm)
    return out, kpages, vpages, c_t, n_t, m_t[..., 0]


def _hi_lo(w):
    hi = w.astype(BF16)
    return jnp.concatenate([hi, (w - hi.astype(F32)).astype(BF16)], axis=1)


def kernel(x_prompt, x_sample, cache_k, cache_v, page_table, state_C, state_n, state_m,
           w_in, b_if, w_branch_a, w_branch_m, w_out, ln1_g, ln1_b,
           w_router, b_router, w_gate, b_gate, w_up, b_up, w_down, b_down, ln2_g, ln2_b):
    depth = w_in.shape[0]
    assert depth == 1, "single-layer step"
    bp, seq, _ = x_prompt.shape
    db, dseq, _ = x_sample.shape
    l = 0
    w = w_in[l]
    o_qa, o_ka, o_va, o_qm, o_om_end = 0, W_A, 2 * W_A, 3 * W_A, 3 * W_A + 4 * W_M
    o_if = o_om_end
    o_ga = o_if + 2 * H_M
    weights = {
        "wz": jnp.concatenate([w[:, o_ga:], w[:, o_qa:o_ka], w[:, o_qm:o_om_end]], axis=1).astype(BF16),
        "wkv": w[:, o_ka:o_qm].astype(BF16),
        "wif": _hi_lo(jnp.pad(w[:, o_if:o_ga], ((0, 0), (0, LANES - 2 * H_M)))),
        "bif": jnp.pad(b_if[l], (0, LANES - 2 * H_M))[None, :],
        "wa": w_branch_a[l].astype(BF16),
        "wm": w_branch_m[l].astype(BF16),
        "wo": w_out[l].astype(BF16),
        "g1": ln1_g[l][None, :], "b1": ln1_b[l][None, :],
        "wr": jnp.pad(w_router[l], ((0, 0), (0, LANES - N_EXPERTS))),
        "br": jnp.pad(b_router[l], (0, LANES - N_EXPERTS), constant_values=NEG_INF)[None, :],
        "wg": w_gate[l].astype(BF16), "bg": b_gate[l][:, None, :],
        "wu": w_up[l].astype(BF16), "bu": b_up[l][:, None, :],
        "wd": w_down[l].astype(BF16), "bd": b_down[l][:, None, :],
        "g2": ln2_g[l][None, :], "b2": ln2_b[l][None, :],
    }
    assert bp == 1, "prompt batch of one sequence"
    zero_state = (jnp.zeros((bp, H_M, DH_M, DH_M), F32), jnp.zeros((bp, H_M, DH_M), F32),
                  jnp.zeros((bp, H_M), F32))
    yp, kp, vp, cp, np_, mp = _group(
        x_prompt.reshape(bp * seq, D_MODEL), weights, ps=PAGE_SIZE, tm=256,
        moe_tile=min(MOE_TILE, bp * seq),
        attn=lambda z, kb, vt, kmean, kpg, vpg: _moba_prompt(z, kb, vt, kmean),
        mlstm_state=zero_state, lc=math.gcd(seq, MLSTM_CHUNK))
    ck = jnp.swapaxes(cache_k.reshape(cache_k.shape[1:]), -1, -2)
    cv = jnp.swapaxes(cache_v.reshape(cache_v.shape[1:]), -1, -2)
    ys, ks, vs, cs, ns, ms = _group(
        x_sample.reshape(db * dseq, D_MODEL), weights, ps=dseq, tm=256,
        moe_tile=min(MOE_TILE, db * dseq),
        attn=lambda z, kb, vt, kmean, kpg, vpg: _moba_sample(z, kpg, vpg, ck, cv, page_table),
        mlstm_state=(state_C[l], state_n[l], state_m[l]), lc=math.gcd(dseq, MLSTM_CHUNK))
    n_pg = seq // PAGE_SIZE
    return (yp.reshape(bp, seq, D_MODEL), ys.reshape(db, dseq, D_MODEL),
            kp.reshape(1, bp, n_pg, H_A, PAGE_SIZE, DH_A), vp.reshape(1, bp, n_pg, H_A, PAGE_SIZE, DH_A),
            ks.reshape(1, db, H_A, dseq, DH_A), vs.reshape(1, db, H_A, dseq, DH_A),
            cp[None], np_[None], mp[None], cs[None], ns[None], ms[None])
```

```python
import functools
import math

import jax
import jax.numpy as jnp
from jax import lax
from jax.experimental import pallas as pl
from jax.experimental.pallas import tpu as pltpu

F32 = jnp.float32
BF16 = jnp.bfloat16
HIGHEST = lax.Precision.HIGHEST

D_MODEL = 1024
PAGE_SIZE = 128
H_A = 8
DH_A = 64
W_A = H_A * DH_A
MOBA_BLOCK = 256
MOBA_TOPK = 3
H_M = 4
DH_M = 128
W_M = H_M * DH_M
MLSTM_CHUNK = 64
N_EXPERTS = 32
TOP_K = 4
SWIGLU_LIMIT = 7.0
SWIGLU_ALPHA = 1.702
DN_ALPHA = 2.0 ** 0.25
LN_EPS = 1e-5
NEG_INF = -1e30

LANES = 128
Z_GA, Z_GM, Z_QA, Z_QM, Z_KM, Z_VM, Z_OM = 0, 1024, 2048, 2560, 3072, 3584, 4096
Z_WIDTH = 4608
VT_ROWS = 80
VMEM_LIMIT = 56 * 1024 * 1024


def _nt_dot(a, b, precision=None):
    return lax.dot_general(a, b, (((1,), (1,)), ((), ())), precision=precision,
                           preferred_element_type=F32)


def _dot(a, b, precision=None):
    return jnp.dot(a, b, precision=precision, preferred_element_type=F32)


def _top_mask(scores, idx_f, k, axis=-1):
    n = scores.shape[axis]
    sel = jnp.zeros(scores.shape, jnp.bool_)
    work = scores
    first_max = None
    for _ in range(k):
        mx = jnp.max(work, axis=axis, keepdims=True)
        if first_max is None:
            first_max = mx
        cand = work == mx
        pick_idx = jnp.min(jnp.where(cand, idx_f, float(n)), axis=axis, keepdims=True)
        pick = idx_f == pick_idx
        sel = jnp.logical_or(sel, pick)
        work = jnp.where(pick, -jnp.inf, work)
    return sel, first_max


def _proj_kernel(x_ref, wz_ref, wkv_ref, wif_ref, bif_ref,
                 z_ref, kb_ref, vt_ref, kp_ref, vp_ref, gif_ref, kmean_ref, *, ps):
    x = x_ref[...]
    tm = x.shape[0]
    xb = x.astype(BF16)
    x_lo = (x - xb.astype(F32)).astype(BF16)
    w_hi = wif_ref[:, :LANES]
    gif_ref[...] = (_dot(xb, w_hi) + _dot(xb, wif_ref[:, LANES:]) + _dot(x_lo, w_hi)) + bif_ref[...]
    for c in range(Z_WIDTH // 512):
        z_ref[:, c * 512:(c + 1) * 512] = _dot(xb, wz_ref[:, c * 512:(c + 1) * 512])
    for c, pref in ((0, kp_ref), (1, vp_ref)):
        r = _dot(xb, wkv_ref[:, c * W_A:(c + 1) * W_A])
        if c == 0:
            kb_ref[...] = r.astype(BF16)
        else:
            rt = r.T
            tail = (lax.broadcasted_iota(jnp.int32, (VT_ROWS - DH_A, tm), 0) == 0).astype(F32)
            for h in range(H_A):
                vt_ref[0, h] = jnp.concatenate([rt[h * DH_A:(h + 1) * DH_A], tail], axis=0).astype(BF16)
        for h in range(H_A):
            pref[:, h, :, :] = r[:, h * DH_A:(h + 1) * DH_A].reshape(tm // ps, ps, DH_A)
        if c == 0:
            kmean_ref[...] = jnp.mean(r.reshape(tm // MOBA_BLOCK, MOBA_BLOCK, W_A), axis=1, keepdims=True)


def _project(x2d, wz, wkv, wif, bif, *, ps, tm):
    t = x2d.shape[0]
    assert tm == MOBA_BLOCK, "one MoBA block of keys per row tile"
    grid = (t // tm,)
    const = lambda i: (0, 0)
    return pl.pallas_call(
        functools.partial(_proj_kernel, ps=ps),
        grid=grid,
        in_specs=[
            pl.BlockSpec((tm, D_MODEL), lambda i: (i, 0)),
            pl.BlockSpec(wz.shape, const),
            pl.BlockSpec(wkv.shape, const),
            pl.BlockSpec(wif.shape, const),
            pl.BlockSpec(bif.shape, const),
        ],
        out_specs=[
            pl.BlockSpec((tm, Z_WIDTH), lambda i: (i, 0)),
            pl.BlockSpec((tm, W_A), lambda i: (i, 0)),
            pl.BlockSpec((1, H_A, VT_ROWS, tm), lambda i: (i, 0, 0, 0)),
            pl.BlockSpec((tm // ps, H_A, ps, DH_A), lambda i: (i, 0, 0, 0)),
            pl.BlockSpec((tm // ps, H_A, ps, DH_A), lambda i: (i, 0, 0, 0)),
            pl.BlockSpec((tm, LANES), lambda i: (i, 0)),
            pl.BlockSpec((tm // MOBA_BLOCK, 1, W_A), lambda i: (i, 0, 0)),
        ],
        out_shape=[
            jax.ShapeDtypeStruct((t, Z_WIDTH), F32),
            jax.ShapeDtypeStruct((t, W_A), BF16),
            jax.ShapeDtypeStruct((t // tm, H_A, VT_ROWS, tm), BF16),
            jax.ShapeDtypeStruct((t // ps, H_A, ps, DH_A), F32),
            jax.ShapeDtypeStruct((t // ps, H_A, ps, DH_A), F32),
            jax.ShapeDtypeStruct((t, LANES), F32),
            jax.ShapeDtypeStruct((t // MOBA_BLOCK, 1, W_A), F32),
        ],
        compiler_params=pltpu.CompilerParams(
            dimension_semantics=("arbitrary",), vmem_limit_bytes=VMEM_LIMIT),
        name="proj",
    )(x2d, wz, wkv, wif, bif)


PAST_UNROLL = 16
PAST_TAIL_UNROLL = 8


LOG2E = 1.4426950408889634


def _split3(x):
    hi = x.astype(BF16)
    r1 = x - hi.astype(F32)
    mid = r1.astype(BF16)
    lo = (r1 - mid.astype(F32)).astype(BF16)
    return hi, mid, lo


def _moba_prompt_kernel(q_ref, k_ref, vt_ref, km_ref, o_ref, m_sc, acc_sc, sel_sc):
    hp = pl.program_id(0)
    j = pl.program_id(1)
    bq = MOBA_BLOCK
    nb = km_ref.shape[0]
    q = q_ref[...]
    lane = lax.broadcasted_iota(jnp.int32, (1, LANES), 1)
    blk_f = lax.broadcasted_iota(jnp.int32, (nb, 1), 0).astype(F32)
    jf = j.astype(F32)
    rk = lax.broadcasted_iota(jnp.int32, (bq, bq), 0)
    rq = lax.broadcasted_iota(jnp.int32, (bq, bq), 1)
    causal = rq >= rk
    pos = lax.broadcasted_iota(jnp.int32, (bq, 1), 0).astype(F32)
    k_extra = jnp.where(lane < 3, pos, jnp.where(lane < 6, 1.0, 0.0)).astype(BF16)
    k_own = jnp.concatenate([k_ref[pl.ds(pl.multiple_of(j * bq, bq), bq), :], k_extra], axis=1)
    vt_own = vt_ref[j]
    q_augs = []
    coefs = []
    for hh in range(2):
        hmask = jnp.logical_and(lane >= hh * DH_A, lane < (hh + 1) * DH_A)
        qh = jnp.where(hmask, q, 0.0)
        head_f = (2 * hp + hh + 1).astype(F32)
        coef = jnp.exp2(jnp.zeros((1, 1), F32) - head_f) * LOG2E
        coefs.append(coef)
        scores = _nt_dot(km_ref[...], qh, precision=HIGHEST)
        scores = jnp.where(blk_f < jf, scores, NEG_INF)
        sel, _ = _top_mask(scores, blk_f, min(MOBA_TOPK, nb), axis=0)
        sel = jnp.logical_and(sel, blk_f < jf)
        sel_sc[hh] = jnp.where(sel, 1.0, 0.0)
        c3 = _split3(jnp.broadcast_to(coef, (bq, 1)))
        q3 = _split3(-(coef * pos))
        q_extra = jnp.zeros((bq, LANES), F32)
        for li, term in enumerate(c3 + q3):
            q_extra = jnp.where(lane == li, term.astype(F32), q_extra)
        qb = (qh * (DH_A ** -0.5 * LOG2E)).astype(BF16)
        q_aug = jnp.concatenate([qb, q_extra.astype(BF16)], axis=1)
        q_augs.append(q_aug)
        logits = jnp.where(causal, _nt_dot(k_own, q_aug), NEG_INF)
        m = jnp.max(logits, axis=0, keepdims=True)
        p = jnp.exp2(logits - m)
        m_sc[hh] = m
        acc_sc[hh] = _dot(vt_own[hh], p.astype(BF16))

    def group(first, unroll):
        ns, kns, vtns, offs, valids = [], [], [], [], []
        for u in range(unroll):
            n_raw = first + u
            n = jnp.minimum(n_raw, j - 1)
            ns.append(n)
            valids.append(n_raw < j)
            kns.append(jnp.concatenate([k_ref[pl.ds(pl.multiple_of(n * bq, bq), bq), :], k_extra], axis=1))
            vtns.append(vt_ref[n])
            offs.append((j - n).astype(F32) * float(bq))
        logits = [[], []]
        for hh in range(2):
            for u in range(unroll):
                picked = sel_sc[hh, pl.ds(ns[u], 1), :]
                keep = jnp.logical_and(picked > 0.5, valids[u])
                rb = jnp.where(keep, -(coefs[hh] * offs[u]), NEG_INF)
                logits[hh].append(_nt_dot(kns[u], q_augs[hh]) + rb)
        m_news, acc_news = [], []
        for hh in range(2):
            m_old = m_sc[hh]
            m_new = m_old
            for lg in logits[hh]:
                m_new = jnp.maximum(m_new, jnp.max(lg, axis=0, keepdims=True))
            m_news.append(m_new)
            acc_news.append(jnp.exp2(m_old - m_new) * acc_sc[hh])
        for hh in range(2):
            acc_new = acc_news[hh]
            for u in range(unroll):
                p = jnp.exp2(logits[hh][u] - m_news[hh])
                acc_new = acc_new + _dot(vtns[u][hh], p.astype(BF16))
            acc_sc[hh] = acc_new
            m_sc[hh] = m_news[hh]

    n_full = lax.div(j, PAST_UNROLL)

    def full_body(it, carry):
        group(it * PAST_UNROLL, PAST_UNROLL)
        return carry

    def tail_body(it, carry):
        group(n_full * PAST_UNROLL + it * PAST_TAIL_UNROLL, PAST_TAIL_UNROLL)
        return carry

    lax.fori_loop(0, n_full, full_body, 0)
    rem = j - n_full * PAST_UNROLL
    lax.fori_loop(0, lax.div(rem + (PAST_TAIL_UNROLL - 1), PAST_TAIL_UNROLL), tail_body, 0)
    acc0 = acc_sc[0]
    acc1 = acc_sc[1]
    o0 = acc0[:DH_A] / acc0[DH_A:DH_A + 1, :]
    o1 = acc1[:DH_A] / acc1[DH_A:DH_A + 1, :]
    o_ref[...] = jnp.concatenate([o0, o1], axis=0).T


def _moba_prompt(z, kb, vt, kmean):
    t = z.shape[0]
    nb = t // MOBA_BLOCK
    bq = MOBA_BLOCK
    qcol = Z_QA // LANES
    return pl.pallas_call(
        _moba_prompt_kernel,
        grid=(H_A // 2, nb),
        in_specs=[
            pl.BlockSpec((bq, LANES), lambda hp, j: (j, qcol + hp)),
            pl.BlockSpec((t, LANES), lambda hp, j: (0, hp)),
            pl.BlockSpec((nb, 2, VT_ROWS, bq), lambda hp, j: (0, hp, 0, 0)),
            pl.BlockSpec((nb, LANES), lambda hp, j: (0, hp)),
        ],
        out_specs=pl.BlockSpec((bq, LANES), lambda hp, j: (j, hp)),
        out_shape=jax.ShapeDtypeStruct((t, W_A), F32),
        scratch_shapes=[
            pltpu.VMEM((2, 1, bq), F32),
            pltpu.VMEM((2, VT_ROWS, bq), F32),
            pltpu.VMEM((2, nb, bq), F32),
        ],
        compiler_params=pltpu.CompilerParams(
            dimension_semantics=("arbitrary", "arbitrary"), vmem_limit_bytes=VMEM_LIMIT),
        name="moba_prompt",
    )(z, kb, vt, kmean)


def _moba_sample_kernel(pt_ref, q_ref, kn_ref, vn_ref, *refs, ppg, past):
    k_refs = refs[:ppg]
    v_refs = refs[ppg:2 * ppg]
    o_ref = refs[2 * ppg]
    qs_sc, qf_sc, ksum_sc, m_sc, l_sc, o_sc = refs[2 * ppg + 1:]
    g = pl.program_id(1)
    ng = pl.num_programs(1)
    t = q_ref.shape[0]
    rows = H_A * t
    npb = o_sc.shape[0]
    ppb = MOBA_BLOCK // PAGE_SIZE
    row_i = lax.broadcasted_iota(jnp.int32, (rows, 1), 0)
    slope = jnp.exp2(-(lax.div(row_i, t) + 1).astype(F32))
    tok = lax.rem(row_i, t)
    blk_lane = lax.broadcasted_iota(jnp.int32, (1, LANES), 1)
    feat_head = lax.div(lax.broadcasted_iota(jnp.int32, (1, W_A), 1), DH_A)

    @pl.when(g == 0)
    def _():
        q = q_ref[...]
        qbd = jnp.concatenate([jnp.where(feat_head == h, q, 0.0) for h in range(H_A)], axis=0)
        qf_sc[...] = qbd
        qs_sc[...] = qbd * (DH_A ** -0.5)
        ksum_sc[...] = jnp.zeros(ksum_sc.shape, F32)
        m_sc[...] = jnp.zeros(m_sc.shape, F32)
        l_sc[...] = jnp.zeros(l_sc.shape, F32)

    qs = qs_sc[...]
    rk = lax.broadcasted_iota(jnp.int32, (1, MOBA_BLOCK), 1)
    ksum_all = ksum_sc[...]
    m_all = m_sc[...]
    l_all = l_sc[...]
    nblk = ppg // ppb
    s_blks = []
    for b in range(nblk):
        kt = jnp.concatenate([k_refs[b * ppb + p][...].reshape(W_A, PAGE_SIZE) for p in range(ppb)], axis=1)
        s_blks.append(_dot(qs, kt))
        ksum = jnp.sum(kt, axis=-1, keepdims=True)
        ksum_all = jnp.where(blk_lane == g * nblk + b, ksum, ksum_all)
    p_blks = []
    for b in range(nblk):
        n = g * nblk + b
        dist = ((past - n * MOBA_BLOCK) + tok - rk).astype(F32)
        logits = s_blks[b] - slope * dist
        m_b = jnp.max(logits, axis=-1, keepdims=True)
        p_ = jnp.exp(logits - m_b)
        p_blks.append(p_)
        m_all = jnp.where(blk_lane == n, m_b, m_all)
        l_all = jnp.where(blk_lane == n, jnp.sum(p_, axis=-1, keepdims=True), l_all)
    for b in range(nblk):
        vt = jnp.concatenate([v_refs[b * ppb + p][...].reshape(W_A, PAGE_SIZE) for p in range(ppb)], axis=1)
        o_sc[g * nblk + b] = _nt_dot(p_blks[b], vt)
    ksum_sc[...] = ksum_all
    m_sc[...] = m_all
    l_sc[...] = l_all

    @pl.when(g == ng - 1)
    def _():
        kmean = ksum_sc[...] * (1.0 / MOBA_BLOCK)
        scores = _dot(qf_sc[...], kmean, precision=HIGHEST)
        scores = jnp.where(blk_lane < npb, scores, -jnp.inf)
        sel, _ = _top_mask(scores, blk_lane.astype(F32), min(MOBA_TOPK, npb))
        kn = jnp.concatenate([kn_ref[h] for h in range(H_A)], axis=1)
        vn = jnp.concatenate([vn_ref[h] for h in range(H_A)], axis=1)
        s_own = _nt_dot(qs, kn)
        d_own = (tok - lax.broadcasted_iota(jnp.int32, (1, t), 1)).astype(F32)
        lg_own = jnp.where(d_own >= 0, s_own - slope * d_own, NEG_INF)
        m_all = m_sc[...]
        m_sel = jnp.max(jnp.where(sel, m_all, -jnp.inf), axis=-1, keepdims=True)
        m_fin = jnp.maximum(m_sel, jnp.max(lg_own, axis=-1, keepdims=True))
        w = jnp.where(sel, jnp.exp(m_all - m_fin), 0.0)
        p_own = jnp.exp(lg_own - m_fin)
        l_fin = jnp.sum(w * l_sc[...], axis=-1, keepdims=True) + jnp.sum(p_own, axis=-1, keepdims=True)
        o = _dot(p_own, vn)
        for n in range(npb):
            o = o + w[:, n:n + 1] * o_sc[n]
        o = o / l_fin
        out = jnp.zeros((t, W_A), F32)
        for h in range(H_A):
            out = out + jnp.where(feat_head == h, o[h * t:(h + 1) * t, :], 0.0)
        o_ref[...] = out


SAMPLE_PAGES_PER_STEP = 32


def _moba_sample(z, k_new, v_new, cache_k, cache_v, page_table):
    db, n_pages = page_table.shape
    t = z.shape[0] // db
    ppb = MOBA_BLOCK // PAGE_SIZE
    npb = n_pages // ppb
    assert n_pages == npb * ppb, "cached pages must fill whole MoBA blocks"
    ppg = SAMPLE_PAGES_PER_STEP if n_pages % SAMPLE_PAGES_PER_STEP == 0 else ppb
    ng = n_pages // ppg

    def page_map(b, g, pt, *, i):
        return (pt[b * n_pages + g * ppg + i], 0, 0, 0)

    page_specs = [pl.BlockSpec((None, H_A, DH_A, PAGE_SIZE), functools.partial(page_map, i=i))
                  for i in range(ppg)]
    grid_spec = pltpu.PrefetchScalarGridSpec(
        num_scalar_prefetch=1,
        grid=(db, ng),
        in_specs=[
            pl.BlockSpec((t, W_A), lambda b, g, pt: (b, Z_QA // W_A)),
            pl.BlockSpec((None, H_A, t, DH_A), lambda b, g, pt: (b, 0, 0, 0)),
            pl.BlockSpec((None, H_A, t, DH_A), lambda b, g, pt: (b, 0, 0, 0)),
        ] + page_specs + page_specs,
        out_specs=pl.BlockSpec((t, W_A), lambda b, g, pt: (b, 0)),
        scratch_shapes=[
            pltpu.VMEM((H_A * t, W_A), F32),
            pltpu.VMEM((H_A * t, W_A), F32),
            pltpu.VMEM((W_A, LANES), F32),
            pltpu.VMEM((H_A * t, LANES), F32),
            pltpu.VMEM((H_A * t, LANES), F32),
            pltpu.VMEM((npb, H_A * t, W_A), F32),
        ],
    )
    return pl.pallas_call(
        functools.partial(_moba_sample_kernel, ppg=ppg, past=n_pages * PAGE_SIZE),
        grid_spec=grid_spec,
        out_shape=jax.ShapeDtypeStruct((db * t, W_A), F32),
        compiler_params=pltpu.CompilerParams(
            dimension_semantics=("arbitrary", "arbitrary"), vmem_limit_bytes=VMEM_LIMIT),
        name="moba_sample",
    )(page_table.reshape(-1), z, k_new, v_new, *([cache_k] * ppg), *([cache_v] * ppg))


def _mlstm_kernel(q_ref, k_ref, v_ref, g_ref, c0_ref, n0_ref, m0_ref,
                  h_ref, c_ref, n_ref, m_ref, c_sc, n_sc, m_sc, *, lc, cps):
    step = pl.program_id(1)
    nsteps = pl.num_programs(1)
    lp = max(lc, MLSTM_CHUNK)

    @pl.when(step == 0)
    def _():
        c_sc[...] = c0_ref[0]
        n_sc[...] = n0_ref[0]
        m_sc[...] = jnp.broadcast_to(m0_ref[0], m_sc.shape)

    def padded(a):
        if a.shape[0] == lp:
            return a
        return jnp.concatenate([a, jnp.zeros((lp - a.shape[0],) + a.shape[1:], a.dtype)], axis=0)

    row = lax.broadcasted_iota(jnp.int32, (lp, lp), 0)
    col = lax.broadcasted_iota(jnp.int32, (lp, lp), 1)
    causal = row >= col
    tril = jnp.where(causal, 1.0, 0.0).astype(BF16)
    pick_8 = jnp.where(lax.broadcasted_iota(jnp.int32, (16, LANES), 0)
                       == lax.broadcasted_iota(jnp.int32, (16, LANES), 1), 1.0, 0.0).astype(BF16)
    gate_lane = lax.broadcasted_iota(jnp.int32, (1, LANES), 1)
    eye = jnp.where(lax.broadcasted_iota(jnp.int32, (DH_M, DH_M), 0)
                    == lax.broadcasted_iota(jnp.int32, (DH_M, DH_M), 1), 1.0, 0.0).astype(BF16)
    rvalid = lax.broadcasted_iota(jnp.int32, (lp, 1), 0) < lc
    cvalid = lax.broadcasted_iota(jnp.int32, (1, lp), 1) < lc
    m_all = m_sc[...]
    n_all = n_sc[...]
    c_all = [c_sc[h] for h in range(H_M)]
    pre = []
    for ci in range(cps):
        rs = slice(ci * lc, (ci + 1) * lc)
        q = padded(q_ref[rs, :])
        k = padded(k_ref[rs, :]) * (DH_M ** -0.5)
        v = padded(v_ref[rs, :])
        gts = padded(g_ref[rs, :])
        lf = jax.nn.log_sigmoid(gts)
        bcum = sum(_dot(tril, part) for part in _split3(lf))
        comb_t = sum(_nt_dot(pick_8, part) for part in _split3(jnp.where(gate_lane < H_M, gts, bcum)))
        heads = []
        for h in range(H_M):
            qh = q[:, h * DH_M:(h + 1) * DH_M]
            kh = k[:, h * DH_M:(h + 1) * DH_M]
            vh = v[:, h * DH_M:(h + 1) * DH_M]
            bc = bcum[:, H_M + h:H_M + h + 1]
            ic = gts[:, h:h + 1]
            br = comb_t[H_M + h:H_M + h + 1, :]
            ir = comb_t[h:h + 1, :]
            dmat = jnp.where(causal, bc - br + ir, -jnp.inf)
            qhb = qh.astype(BF16)
            wide = lambda col: jnp.broadcast_to(col, (lp, LANES))
            heads.append(dict(
                qh=qh, qhb=qhb, kh=kh, vhb=vh.astype(BF16), br=br, ir=ir, dmat=dmat,
                bc_w=wide(bc), ic_w=wide(ic),
                m_intra_w=wide(jnp.max(dmat, axis=-1, keepdims=True)),
                qk=_nt_dot(qhb, kh.astype(BF16)),
                kh_t=sum(_nt_dot(eye, part) for part in _split3(kh)),
            ))
        pre.append(heads)
    for ci in range(cps):
        rs = slice(ci * lc, (ci + 1) * lc)
        m_out, n_out, c_out = [], [], []
        for h in range(H_M):
            d = pre[ci][h]
            bc_w, ic_w, dmat = d["bc_w"], d["ic_w"], d["dmat"]
            m_prev = m_all[h:h + 1, :]
            c_prev = c_all[h]
            n_prev = n_all[h:h + 1, :]
            a_inter = bc_w + m_prev
            m_t = jnp.maximum(a_inter, d["m_intra_w"])
            w_inter = jnp.exp(a_inter - m_t)
            s = d["qk"] * jnp.exp(dmat - m_t[:, :lp])
            num = w_inter * _dot(d["qhb"], c_prev.astype(BF16)) + _dot(s.astype(BF16), d["vhb"])
            den = (w_inter[:, :1] * jnp.sum(d["qh"] * n_prev, axis=-1, keepdims=True)
                   + jnp.sum(s, axis=-1, keepdims=True))
            h_out = num / jnp.maximum(jnp.abs(den), jnp.exp(-m_t[:, :1]))
            h_ref[rs, h * DH_M:(h + 1) * DH_M] = h_out[:lc]
            m_new = m_t[lc - 1:lc, :]
            b_last = bc_w[lc - 1:lc, :]
            g_c = jnp.exp(b_last + m_prev - m_new)
            g_s = jnp.where(rvalid, jnp.exp(b_last - bc_w + ic_w - m_new), 0.0)
            g_s_row = jnp.where(cvalid, jnp.exp(b_last[:, :lp] - d["br"] + d["ir"] - m_new[:, :lp]), 0.0)
            kg_t = (d["kh_t"] * g_s_row).astype(BF16)
            c_out.append(g_c * c_prev + _dot(kg_t, d["vhb"]))
            n_out.append(g_c * n_prev + jnp.sum(g_s * d["kh"], axis=0, keepdims=True))
            m_out.append(m_new)
        m_all = jnp.concatenate(m_out, axis=0)
        n_all = jnp.concatenate(n_out, axis=0)
        c_all = c_out
    for h in range(H_M):
        c_sc[h] = c_all[h]
    n_sc[...] = n_all
    m_sc[...] = m_all

    @pl.when(step == nsteps - 1)
    def _():
        c_ref[0] = c_sc[...]
        n_ref[0] = n_sc[...]
        m_ref[0] = m_sc[:, 0:1]


def _mlstm(z, gif, c0, n0, m0, *, lc, cps):
    b = c0.shape[0]
    t = z.shape[0] // b
    rows = lc * cps
    nc = t // rows
    assert nc * rows == t
    row = lambda bi, ci: bi * nc + ci
    wcol = lambda off: off // W_M
    return pl.pallas_call(
        functools.partial(_mlstm_kernel, lc=lc, cps=cps),
        grid=(b, nc),
        in_specs=[
            pl.BlockSpec((rows, W_M), lambda bi, ci: (row(bi, ci), wcol(Z_QM))),
            pl.BlockSpec((rows, W_M), lambda bi, ci: (row(bi, ci), wcol(Z_KM))),
            pl.BlockSpec((rows, W_M), lambda bi, ci: (row(bi, ci), wcol(Z_VM))),
            pl.BlockSpec((rows, LANES), lambda bi, ci: (row(bi, ci), 0)),
            pl.BlockSpec((1, H_M, DH_M, DH_M), lambda bi, ci: (bi, 0, 0, 0)),
            pl.BlockSpec((1, H_M, DH_M), lambda bi, ci: (bi, 0, 0)),
            pl.BlockSpec((1, H_M, 1), lambda bi, ci: (bi, 0, 0)),
        ],
        out_specs=[
            pl.BlockSpec((rows, W_M), lambda bi, ci: (row(bi, ci), 0)),
            pl.BlockSpec((1, H_M, DH_M, DH_M), lambda bi, ci: (bi, 0, 0, 0)),
            pl.BlockSpec((1, H_M, DH_M), lambda bi, ci: (bi, 0, 0)),
            pl.BlockSpec((1, H_M, 1), lambda bi, ci: (bi, 0, 0)),
        ],
        out_shape=[
            jax.ShapeDtypeStruct((b * t, W_M), F32),
            jax.ShapeDtypeStruct((b, H_M, DH_M, DH_M), F32),
            jax.ShapeDtypeStruct((b, H_M, DH_M), F32),
            jax.ShapeDtypeStruct((b, H_M, 1), F32),
        ],
        scratch_shapes=[
            pltpu.VMEM((H_M, DH_M, DH_M), F32),
            pltpu.VMEM((H_M, DH_M), F32),
            pltpu.VMEM((H_M, LANES), F32),
        ],
        compiler_params=pltpu.CompilerParams(
            dimension_semantics=("arbitrary", "arbitrary"), vmem_limit_bytes=VMEM_LIMIT),
        name="mlstm",
    )(z, z, z, gif, c0, n0, m0[..., None])


def _layernorm(y, g, b):
    mu = jnp.mean(y, axis=-1, keepdims=True)
    var = jnp.mean(jnp.square(y - mu), axis=-1, keepdims=True)
    return (y - mu) * lax.rsqrt(var + LN_EPS) * g + b


ROUTER_ROWS = 256
FINISH_ROWS = 512


def _finish_kernel(x_ref, a_ref, mh_ref, om_ref, ga_ref, gm_ref, wa_ref, wm_ref, wo_ref,
                   g1_ref, b1_ref, wr_ref, br_ref,
                   h_ref, hb_ref, gates_t_ref, cnt_ref):
    sub = ROUTER_ROWS
    parts = [slice(s * sub, (s + 1) * sub) for s in range(x_ref.shape[0] // sub)]
    a = [_dot(a_ref[p, :].astype(BF16), wa_ref[...]) for p in parts]
    m = [_dot((mh_ref[p, :] * jax.nn.sigmoid(om_ref[p, :])).astype(BF16), wm_ref[...]) for p in parts]
    merged = [jax.nn.sigmoid(ga_ref[p, :]) * a_ + jax.nn.sigmoid(gm_ref[p, :]) * m_
              for p, a_, m_ in zip(parts, a, m)]
    y = [DN_ALPHA * x_ref[p, :] + _dot(mg.astype(BF16), wo_ref[...]) for p, mg in zip(parts, merged)]
    h = [_layernorm(y_, g1_ref[...], b1_ref[...]) for y_ in y]
    for p, h_ in zip(parts, h):
        h_ref[p, :] = h_
        hb_ref[p, :] = h_.astype(BF16)
    logits = [_dot(h_, wr_ref[...], precision=HIGHEST) + br_ref[...] for h_ in h]
    lane_f = lax.broadcasted_iota(jnp.int32, (1, LANES), 1).astype(F32)
    for s, (p, lg) in enumerate(zip(parts, logits)):
        sel, top = _top_mask(lg, lane_f, TOP_K)
        e = jnp.where(sel, jnp.exp(lg - top), 0.0)
        gates = e / jnp.sum(e, axis=-1, keepdims=True)
        gates_t_ref[:, p] = gates.T
        cnt_ref[s] = jnp.sum(jnp.where(gates > 0.0, 1.0, 0.0), axis=0, keepdims=True).astype(jnp.int32)


def _finish(x2d, a, mh, z, wa, wm, wo, g1, b1, wr, br, *, tm):
    t = x2d.shape[0]
    nt = t // tm
    const = lambda i: (0, 0)
    return pl.pallas_call(
        _finish_kernel,
        grid=(nt,),
        in_specs=[
            pl.BlockSpec((tm, D_MODEL), lambda i: (i, 0)),
            pl.BlockSpec((tm, W_A), lambda i: (i, 0)),
            pl.BlockSpec((tm, W_M), lambda i: (i, 0)),
            pl.BlockSpec((tm, W_M), lambda i: (i, Z_OM // W_M)),
            pl.BlockSpec((tm, D_MODEL), lambda i: (i, Z_GA // D_MODEL)),
            pl.BlockSpec((tm, D_MODEL), lambda i: (i, Z_GM // D_MODEL)),
            pl.BlockSpec(wa.shape, const),
            pl.BlockSpec(wm.shape, const),
            pl.BlockSpec(wo.shape, const),
            pl.BlockSpec(g1.shape, const),
            pl.BlockSpec(b1.shape, const),
            pl.BlockSpec(wr.shape, const),
            pl.BlockSpec(br.shape, const),
        ],
        out_specs=[
            pl.BlockSpec((tm, D_MODEL), lambda i: (i, 0)),
            pl.BlockSpec((tm, D_MODEL), lambda i: (i, 0)),
            pl.BlockSpec((LANES, tm), lambda i: (0, i)),
            pl.BlockSpec((tm // ROUTER_ROWS, 1, LANES), lambda i: (i, 0, 0)),
        ],
        out_shape=[
            jax.ShapeDtypeStruct((t, D_MODEL), F32),
            jax.ShapeDtypeStruct((t, D_MODEL), BF16),
            jax.ShapeDtypeStruct((LANES, t), F32),
            jax.ShapeDtypeStruct((t // ROUTER_ROWS, 1, LANES), jnp.int32),
        ],
        compiler_params=pltpu.CompilerParams(
            dimension_semantics=("arbitrary",), vmem_limit_bytes=VMEM_LIMIT),
        name="finish",
    )(x2d, a, mh, z, z, z, wa, wm, wo, g1, b1, wr, br)


MLSTM_CHUNKS_PER_STEP = 4
MOE_TILE = 2048
MOE_SEG = 512
MOE_SEG_ROWS = 80
MOE_RANK_BLOCK = ROUTER_ROWS


def _moe_kernel(cnt_ref, hb_ref, gt_ref, wg_ref, bg_ref, wu_ref, bu_ref, wd_ref, bd_ref,
                y_ref, posr_sc, *, cnt_per_seg, seg):
    i = pl.program_id(0)
    e = pl.program_id(1)
    tt = hb_ref.shape[0]
    nseg = tt // seg
    rseg = MOE_SEG_ROWS

    @pl.when(e == 0)
    def _():
        y_ref[...] = jnp.zeros(y_ref.shape, F32)

    cnt_max = 0
    for sg in range(nseg):
        first = (i * nseg + sg) * cnt_per_seg
        cnt_seg = cnt_ref[first, e]
        for r in range(1, cnt_per_seg):
            cnt_seg = cnt_seg + cnt_ref[first + r, e]
        cnt_max = jnp.maximum(cnt_max, cnt_seg)

    @pl.when(cnt_max > 0)
    def _():
        grow = gt_ref[pl.ds(lax.rem(e, 8), 1), :]
        selr = jnp.where(grow > 0.0, 1.0, 0.0)
        ru = lax.broadcasted_iota(jnp.int32, (LANES, LANES), 0)
        cu = lax.broadcasted_iota(jnp.int32, (LANES, LANES), 1)
        strict_upper = jnp.where(ru < cu, 1.0, 0.0).astype(BF16)
        for ci in range(tt // LANES):
            if (ci * LANES) % seg == 0:
                offr = jnp.zeros((16, 1), F32)
            sr = jnp.broadcast_to(selr[:, ci * LANES:(ci + 1) * LANES], (16, LANES))
            rank = _dot(sr.astype(BF16), strict_upper) + offr
            posr_sc[:, ci * LANES:(ci + 1) * LANES] = jnp.where(sr > 0.0, rank, -1.0)
            offr = offr + jnp.sum(sr, axis=-1, keepdims=True)
        slot_sub = lax.broadcasted_iota(jnp.int32, (rseg, LANES), 0).astype(F32)
        no_slot = jnp.zeros((LANES - rseg, seg), F32)

        def one_pass(s, carry):
            base = (s * rseg).astype(F32)
            xs_parts, gs_parts, scatter = [], [], []
            for sg in range(nseg):
                pm_chunks = []
                for ci in range(sg * seg // LANES, (sg + 1) * seg // LANES):
                    pr = posr_sc[0:1, ci * LANES:(ci + 1) * LANES] - base
                    pm_chunks.append(pr == slot_sub)
                pmask = jnp.concatenate(pm_chunks, axis=1)
                gs_parts.append(jnp.sum(jnp.where(pmask, grow[:, sg * seg:(sg + 1) * seg], 0.0),
                                        axis=-1, keepdims=True))
                onehot = jnp.where(pmask, 1.0, 0.0)
                xs_parts.append(_dot(onehot.astype(BF16), hb_ref[sg * seg:(sg + 1) * seg, :]).astype(BF16))
                scatter.append(jnp.concatenate([onehot, no_slot], axis=0).T.astype(BF16))
            xs = jnp.concatenate(xs_parts, axis=0)
            gs = jnp.concatenate(gs_parts, axis=0)
            gp = jnp.minimum(_dot(xs, wg_ref[0]) + bg_ref[0], SWIGLU_LIMIT)
            up = jnp.clip(_dot(xs, wu_ref[0]) + bu_ref[0], -SWIGLU_LIMIT, SWIGLU_LIMIT)
            hdn = (up + 1.0) * (gp * jax.nn.sigmoid(SWIGLU_ALPHA * gp))
            yb = ((_dot(hdn.astype(BF16), wd_ref[0]) + bd_ref[0]) * gs).astype(BF16)
            pad = jnp.zeros((LANES - rseg, yb.shape[1]), BF16)
            for sg in range(nseg):
                rows = slice(sg * seg, (sg + 1) * seg)
                y_seg = jnp.concatenate([yb[sg * rseg:(sg + 1) * rseg], pad], axis=0)
                y_ref[rows, :] += _dot(scatter[sg], y_seg)
            return carry

        lax.fori_loop(0, lax.div(cnt_max + (rseg - 1), rseg), one_pass, 0)


def _moe(hb, gates_t, cnt, wg, bg, wu, bu, wd, bd, *, tt):
    t = hb.shape[0]
    nt = t // tt
    seg = min(MOE_SEG, tt)
    cnt_per_seg = cnt.shape[0] * seg // t
    assert cnt_per_seg * t == cnt.shape[0] * seg and tt % seg == 0 and seg % MOE_RANK_BLOCK == 0
    wspec = pl.BlockSpec((1, D_MODEL, D_MODEL), lambda i, e, c: (e, 0, 0))
    bspec = pl.BlockSpec((1, 1, D_MODEL), lambda i, e, c: (e, 0, 0))
    grid_spec = pltpu.PrefetchScalarGridSpec(
        num_scalar_prefetch=1,
        grid=(nt, N_EXPERTS),
        in_specs=[
            pl.BlockSpec((tt, D_MODEL), lambda i, e, c: (i, 0)),
            pl.BlockSpec((8, tt), lambda i, e, c: (e // 8, i)),
            wspec, bspec, wspec, bspec, wspec, bspec,
        ],
        out_specs=pl.BlockSpec((tt, D_MODEL), lambda i, e, c: (i, 0)),
        scratch_shapes=[
            pltpu.VMEM((16, tt), F32),
        ],
    )
    return pl.pallas_call(
        functools.partial(_moe_kernel, cnt_per_seg=cnt_per_seg, seg=seg),
        grid_spec=grid_spec,
        out_shape=jax.ShapeDtypeStruct((t, D_MODEL), F32),
        compiler_params=pltpu.CompilerParams(
            dimension_semantics=("arbitrary", "arbitrary"), vmem_limit_bytes=VMEM_LIMIT),
        name="moe",
    )(cnt, hb, gates_t, wg, bg, wu, bu, wd, bd)


def _ln2_kernel(h_ref, y_ref, g_ref, b_ref, o_ref):
    o_ref[...] = _layernorm(DN_ALPHA * h_ref[...] + y_ref[...], g_ref[...], b_ref[...])


def _ln2(h, y, g, b, *, tm):
    t = h.shape[0]
    spec = pl.BlockSpec((tm, D_MODEL), lambda i: (i, 0))
    const = pl.BlockSpec((1, D_MODEL), lambda i: (0, 0))
    return pl.pallas_call(
        _ln2_kernel,
        grid=(t // tm,),
        in_specs=[spec, spec, const, const],
        out_specs=spec,
        out_shape=jax.ShapeDtypeStruct((t, D_MODEL), F32),
        compiler_params=pltpu.CompilerParams(dimension_semantics=("arbitrary",)),
        name="ln2",
    )(h, y, g, b)


def _group(x2d, weights, *, ps, tm, moe_tile, attn, mlstm_state, lc):
    z, kb, vt, kpages, vpages, gif, kmean = _project(
        x2d, weights["wz"], weights["wkv"], weights["wif"], weights["bif"], ps=ps, tm=tm)
    a = attn(z, kb, vt, kmean.reshape(kmean.shape[0], W_A), kpages, vpages)
    t_seq = x2d.shape[0] // mlstm_state[0].shape[0]
    cps = MLSTM_CHUNKS_PER_STEP if (t_seq // lc) % MLSTM_CHUNKS_PER_STEP == 0 else 1
    mh, c_t, n_t, m_t = _mlstm(z, gif, *mlstm_state, lc=lc, cps=cps)
    h, hb, gates_t, cnt = _finish(
        x2d, a, mh, z, weights["wa"], weights["wm"], weights["wo"], weights["g1"], weights["b1"],
        weights["wr"], weights["br"],
        tm=FINISH_ROWS if x2d.shape[0] % FINISH_ROWS == 0 else ROUTER_ROWS)
    y = _moe(hb, gates_t, cnt.reshape(cnt.shape[0], LANES),
             weights["wg"], weights["bg"], weights["wu"], weights["bu"], weights["wd"], weights["bd"],
             tt=moe_tile)
    out = _ln2(h, y, weights["g2"], weights["b2"], tm=tm)
    return out, kpages, vpages, c_t, n_t, m_t[..., 0]


def _hi_lo(w):
    hi = w.astype(BF16)
    return jnp.concatenate([hi, (w - hi.astype(F32)).astype(BF16)], axis=1)


def kernel(x_prompt, x_sample, cache_k, cache_v, page_table, state_C, state_n, state_m,
           w_in, b_if, w_branch_a, w_branch_m, w_out, ln1_g, ln1_b,
           w_router, b_router, w_gate, b_gate, w_up, b_up, w_down, b_down, ln2_g, ln2_b):
    depth = w_in.shape[0]
    assert depth == 1, "single-layer step"
    bp, seq, _ = x_prompt.shape
    db, dseq, _ = x_sample.shape
    l = 0
    w = w_in[l]
    o_qa, o_ka, o_va, o_qm, o_om_end = 0, W_A, 2 * W_A, 3 * W_A, 3 * W_A + 4 * W_M
    o_if = o_om_end
    o_ga = o_if + 2 * H_M
    weights = {
        "wz": jnp.concatenate([w[:, o_ga:], w[:, o_qa:o_ka], w[:, o_qm:o_om_end]], axis=1).astype(BF16),
        "wkv": w[:, o_ka:o_qm].astype(BF16),
        "wif": _hi_lo(jnp.pad(w[:, o_if:o_ga], ((0, 0), (0, LANES - 2 * H_M)))),
        "bif": jnp.pad(b_if[l], (0, LANES - 2 * H_M))[None, :],
        "wa": w_branch_a[l].astype(BF16),
        "wm": w_branch_m[l].astype(BF16),
        "wo": w_out[l].astype(BF16),
        "g1": ln1_g[l][None, :], "b1": ln1_b[l][None, :],
        "wr": jnp.pad(w_router[l], ((0, 0), (0, LANES - N_EXPERTS))),
        "br": jnp.pad(b_router[l], (0, LANES - N_EXPERTS), constant_values=NEG_INF)[None, :],
        "wg": w_gate[l].astype(BF16), "bg": b_gate[l][:, None, :],
        "wu": w_up[l].astype(BF16), "bu": b_up[l][:, None, :],
        "wd": w_down[l].astype(BF16), "bd": b_down[l][:, None, :],
        "g2": ln2_g[l][None, :], "b2": ln2_b[l][None, :],
    }
    assert bp == 1, "prompt batch of one sequence"
    zero_state = (jnp.zeros((bp, H_M, DH_M, DH_M), F32), jnp.zeros((bp, H_M, DH_M), F32),
                  jnp.zeros((bp, H_M), F32))
    yp, kp, vp, cp, np_, mp = _group(
        x_prompt.reshape(bp * seq, D_MODEL), weights, ps=PAGE_SIZE, tm=256,
        moe_tile=min(MOE_TILE, bp * seq),
        attn=lambda z, kb, vt, kmean, kpg, vpg: _moba_prompt(z, kb, vt, kmean),
        mlstm_state=zero_state, lc=math.gcd(seq, MLSTM_CHUNK))
    ck = jnp.swapaxes(cache_k.reshape(cache_k.shape[1:]), -1, -2)
    cv = jnp.swapaxes(cache_v.reshape(cache_v.shape[1:]), -1, -2)
    ys, ks, vs, cs, ns, ms = _group(
        x_sample.reshape(db * dseq, D_MODEL), weights, ps=dseq, tm=256,
        moe_tile=min(MOE_TILE, db * dseq),
        attn=lambda z, kb, vt, kmean, kpg, vpg: _moba_sample(z, kpg, vpg, ck, cv, page_table),
        mlstm_state=(state_C[l], state_n[l], state_m[l]), lc=math.gcd(dseq, MLSTM_CHUNK))
    n_pg = seq // PAGE_SIZE
    return (yp.reshape(bp, seq, D_MODEL), ys.reshape(db, dseq, D_MODEL),
            kp.reshape(1, bp, n_pg, H_A, PAGE_SIZE, DH_A), vp.reshape(1, bp, n_pg, H_A, PAGE_SIZE, DH_A),
            ks.reshape(1, db, H_A, dseq, DH_A), vs.reshape(1, db, H_A, dseq, DH_A),
            cp[None], np_[None], mp[None], cs[None], ns[None], ms[None])
```
